```python
import jax, jax.numpy as jnp
from jax import lax
import numpy as np

D_MODEL = 1024
BATCH = 16
SEQ = 2048
DEPTH = 4
DEC_BATCH = 8
DEC_SEQ = 4096
PAST_LEN = 128

GRID_W = 64
ROPE_THETA = 10000.0
NORM_EPS = 1e-6
NEG_INF = -1e30

MLA_HEADS = 8
MLA_NOPE = 64
MLA_ROPE = 32
MLA_V = 64
MLA_DQK = MLA_NOPE + MLA_ROPE
MLA_Q_RANK = 384
MLA_KV_RANK = 256
MLA_QBLK = 128
MLA_WIDTH = MLA_HEADS * MLA_V

DIL_PAIRS = ((128, 1), (512, 4), (2048, 16))
DIL_GROUPS = 3
DIL_HPG = 4
DIL_HD = 64
DIL_HEADS = DIL_GROUPS * DIL_HPG
DIL_WIDTH = DIL_HPG * DIL_HD

NA_HEADS = 8
NA_HD = 64
NA_KH = 8
NA_KW = 16
NA_WIDTH = NA_HEADS * NA_HD

N_BRANCH = 3
IN_SIZES = (MLA_Q_RANK, MLA_KV_RANK, MLA_ROPE, MLA_WIDTH, 3 * DIL_HEADS * DIL_HD, DIL_WIDTH, 3 * NA_HEADS * NA_HD, NA_WIDTH, N_BRANCH * D_MODEL)
D_IN = 384 + 256 + 32 + 512 + 2304 + 256 + 1536 + 512 + 3072

kernel_name = 'hybrid_mla_dilated_natten_encoder'


def rmsnorm(x, g):
    x32 = x.astype(jnp.float32)
    y = x32 * lax.rsqrt(jnp.mean(x32 * x32, axis=-1, keepdims=True) + NORM_EPS)
    return (y * g.astype(jnp.float32)).astype(x.dtype)


def rope(x, pos):
    half = x.shape[-1] // 2
    inv = ROPE_THETA ** (-jnp.arange(half, dtype=jnp.float32) * 2.0 / x.shape[-1])
    ang = pos[:, None] * inv[None, :]
    cos = jnp.cos(ang)[:, None, :].astype(x.dtype)
    sin = jnp.sin(ang)[:, None, :].astype(x.dtype)
    x1, x2 = x[..., :half], x[..., half:]
    return jnp.concatenate([x1 * cos - x2 * sin, x2 * cos + x1 * sin], axis=-1)


def mla_attention(cq, ckv, kr, g_q, w_uq, g_kv, w_ukv):
    b, s, _ = cq.shape
    pos = jnp.arange(s, dtype=jnp.float32)
    q = (rmsnorm(cq, g_q) @ w_uq).reshape(b, s, MLA_HEADS, MLA_DQK)
    q = jnp.concatenate([q[..., :MLA_NOPE], rope(q[..., MLA_NOPE:], pos)], axis=-1)
    kv = (rmsnorm(ckv, g_kv) @ w_ukv).reshape(b, s, MLA_HEADS, MLA_NOPE + MLA_V)
    k_rope = rope(kr[:, :, None, :], pos)
    k = jnp.concatenate([kv[..., :MLA_NOPE], jnp.broadcast_to(k_rope, (b, s, MLA_HEADS, MLA_ROPE))], axis=-1)
    v = kv[..., MLA_NOPE:]
    scale = MLA_DQK ** -0.5
    qb = q.reshape(b, s // MLA_QBLK, MLA_QBLK, MLA_HEADS, MLA_DQK).transpose(1, 0, 2, 3, 4)

    def block(qi):
        sc = jnp.einsum('bqhd,bkhd->bhqk', qi, k).astype(jnp.float32) * scale
        p = jax.nn.softmax(sc, axis=-1).astype(v.dtype)
        return jnp.einsum('bhqk,bkhd->bqhd', p, v)

    o = lax.map(block, qb)
    return o.transpose(1, 0, 2, 3, 4).reshape(b, s, MLA_WIDTH)


def band_attention(q, k, v, n):
    b, g, L, h, hd = q.shape
    nb = -(-L // n)
    lp = nb * n
    q = jnp.pad(q, ((0, 0), (0, 0), (0, lp - L), (0, 0), (0, 0)))
    pad_kv = ((0, 0), (0, 0), (n, lp - L + n), (0, 0), (0, 0))
    k = jnp.pad(k, pad_kv)
    v = jnp.pad(v, pad_kv)

    def windows(x):
        return jnp.concatenate([x[:, :, j * n:j * n + lp].reshape(b, g, nb, n, h, hd) for j in range(3)], axis=3)

    kw, vw = windows(k), windows(v)
    qb = q.reshape(b, g, nb, n, h, hd)
    sc = jnp.einsum('bgiqhd,bgikhd->bghiqk', qb, kw).astype(jnp.float32) * (hd ** -0.5)
    qpos = jnp.arange(lp).reshape(nb, n)[:, :, None]
    kpos = (jnp.arange(nb) * n - n)[:, None, None] + jnp.arange(3 * n)[None, None, :]
    mask = (jnp.abs(kpos - qpos) <= n) & (kpos >= 0) & (kpos < L)
    sc = jnp.where(mask, sc, NEG_INF)
    lse = jax.nn.logsumexp(sc, axis=-1)
    p = jnp.exp(sc - lse[..., None]).astype(v.dtype)
    o = jnp.einsum('bghiqk,bgikhd->bgiqhd', p, vw).reshape(b, g, lp, h, hd)[:, :, :L]
    lse = lse.reshape(b, g, h, lp).transpose(0, 1, 3, 2)[:, :, :L]
    return o, lse


def dilated_group(q, k, v, d, n):
    b, s, h, hd = q.shape
    L = s // d
    to_cls = lambda x: x.reshape(b, L, d, h, hd).transpose(0, 2, 1, 3, 4)
    o, lse = band_attention(to_cls(q), to_cls(k), to_cls(v), n)
    return o.transpose(0, 2, 1, 3, 4).reshape(b, s, h, hd), lse.transpose(0, 2, 1, 3).reshape(b, s, h)


def dilated_attention(q, k, v):
    b, s = q.shape[0], q.shape[1]
    outs, lses = [], []
    for gi, (w, d) in enumerate(DIL_PAIRS):
        sl = slice(gi * DIL_HPG, (gi + 1) * DIL_HPG)
        o, l = dilated_group(q[:, :, sl], k[:, :, sl], v[:, :, sl], d, w // (2 * d))
        outs.append(o)
        lses.append(l)
    wts = jax.nn.softmax(jnp.stack(lses), axis=0)
    o = jnp.sum(wts[..., None].astype(q.dtype) * jnp.stack(outs), axis=0)
    return o.reshape(b, s, DIL_WIDTH)


def neighbourhood_attention(q, k, v, rpb):
    b, s, h, hd = q.shape
    rows = s // GRID_W
    kh = min(NA_KH, rows)
    qg = q.reshape(b, rows, GRID_W, h, hd)
    r = jnp.arange(rows)
    rs = jnp.clip(r - kh // 2, 0, rows - kh)
    row_idx = rs[:, None] + jnp.arange(kh)[None, :]
    kg = k.reshape(b, rows, GRID_W, h, hd)[:, row_idx]
    vg = v.reshape(b, rows, GRID_W, h, hd)[:, row_idx]
    sc = jnp.einsum('brchd,brjwhd->bhrcjw', qg, kg).astype(jnp.float32) * (hd ** -0.5)
    col = jnp.arange(GRID_W)
    cs = jnp.clip(col - NA_KW // 2, 0, GRID_W - NA_KW)
    colmask = (col[None, :] >= cs[:, None]) & (col[None, :] < cs[:, None] + NA_KW)
    roff = row_idx - r[:, None] + NA_KH - 1
    coff = jnp.clip(col[None, :] - col[:, None] + NA_KW - 1, 0, 2 * NA_KW - 2)
    bias = rpb[:, roff[:, :, None, None], coff[None, None, :, :]].transpose(0, 1, 3, 2, 4)
    sc = jnp.where(colmask[:, None, :], sc + bias[None].astype(jnp.float32), NEG_INF)
    p = jax.nn.softmax(sc.reshape(b, h, rows, GRID_W, kh * GRID_W), axis=-1)
    p = p.reshape(b, h, rows, GRID_W, kh, GRID_W).astype(v.dtype)
    o = jnp.einsum('bhrcjw,brjwhd->brchd', p, vg)
    return o.reshape(b, s, NA_WIDTH)


def encoder_layer(x, c_act, w_ada, b_ada, g_pre, g_post, w_in, g_q, w_uq, g_kv, w_ukv, rpb, w_pa, w_pb, w_pc, w_out):
    b, s, _ = x.shape
    shift, scale, gate = jnp.split(c_act @ w_ada + b_ada, 3, axis=-1)
    h = rmsnorm(x, g_pre) * (1 + scale[:, None, :]) + shift[:, None, :]
    z = h @ w_in
    cuts = [int(c) for c in np.cumsum(IN_SIZES)[:-1]]
    cq, ckv, kr, gate_a, qkv_b, gate_b, qkv_c, gate_c, merge = jnp.split(z, cuts, axis=-1)
    o_a = mla_attention(cq, ckv, kr, g_q, w_uq, g_kv, w_ukv)
    pos = jnp.arange(s, dtype=jnp.float32)
    qkv_b = qkv_b.reshape(b, s, 3, DIL_HEADS, DIL_HD)
    o_b = dilated_attention(rope(qkv_b[:, :, 0], pos), rope(qkv_b[:, :, 1], pos), qkv_b[:, :, 2])
    qkv_c = qkv_c.reshape(b, s, 3, NA_HEADS, NA_HD)
    o_c = neighbourhood_attention(qkv_c[:, :, 0], qkv_c[:, :, 1], qkv_c[:, :, 2], rpb)
    p_a = (o_a * jax.nn.silu(gate_a)) @ w_pa
    p_b = (o_b * jax.nn.silu(gate_b)) @ w_pb
    p_c = (o_c * jax.nn.silu(gate_c)) @ w_pc
    mg = jax.nn.sigmoid(merge.reshape(b, s, N_BRANCH, D_MODEL))
    mixed = mg[:, :, 0] * p_a + mg[:, :, 1] * p_b + mg[:, :, 2] * p_c
    out = mixed @ w_out
    return x + gate[:, None, :] * rmsnorm(out, g_post)


def trunk(x, c, w_ada, b_ada, g_pre, g_post, w_in, g_q, w_uq, g_kv, w_ukv, rpb, w_pa, w_pb, w_pc, w_out):
    c_act = jax.nn.silu(c)
    for l in range(DEPTH):
        x = encoder_layer(x, c_act, w_ada[l], b_ada[l], g_pre[l], g_post[l], w_in[l], g_q[l], w_uq[l],
                          g_kv[l], w_ukv[l], rpb[l], w_pa[l], w_pb[l], w_pc[l], w_out[l])
    return x


def setup_inputs(seed: int = 0) -> dict:
    key = jax.random.key(seed)
    ks = jax.random.split(key, 18)
    f32 = jnp.float32
    nrm = lambda k, shape, sc: jax.random.normal(k, shape, f32) * sc
    return {
        'x_prompt': nrm(ks[0], (BATCH, SEQ, D_MODEL), 1.0),
        'x_sample': nrm(ks[1], (DEC_BATCH, DEC_SEQ, D_MODEL), 1.0),
        'c_prompt': nrm(ks[2], (BATCH, D_MODEL), 1.0),
        'c_sample': nrm(ks[3], (DEC_BATCH, D_MODEL), 1.0),
        'w_ada': nrm(ks[4], (DEPTH, D_MODEL, 3 * D_MODEL), 0.5 * D_MODEL ** -0.5),
        'b_ada': nrm(ks[5], (DEPTH, 3 * D_MODEL), 0.01),
        'g_pre': 1.0 + nrm(ks[6], (DEPTH, D_MODEL), 0.02),
        'g_post': 1.0 + nrm(ks[7], (DEPTH, D_MODEL), 0.02),
        'w_in': nrm(ks[8], (DEPTH, D_MODEL, D_IN), D_MODEL ** -0.5),
        'g_q': 1.0 + nrm(ks[9], (DEPTH, MLA_Q_RANK), 0.02),
        'w_uq': nrm(ks[10], (DEPTH, MLA_Q_RANK, MLA_HEADS * MLA_DQK), MLA_Q_RANK ** -0.5),
        'g_kv': 1.0 + nrm(ks[11], (DEPTH, MLA_KV_RANK), 0.02),
        'w_ukv': nrm(ks[12], (DEPTH, MLA_KV_RANK, MLA_HEADS * (MLA_NOPE + MLA_V)), MLA_KV_RANK ** -0.5),
        'rpb': nrm(ks[13], (DEPTH, NA_HEADS, 2 * NA_KH - 1, 2 * NA_KW - 1), 0.1),
        'w_pa': nrm(ks[14], (DEPTH, MLA_WIDTH, D_MODEL), MLA_WIDTH ** -0.5),
        'w_pb': nrm(ks[15], (DEPTH, DIL_WIDTH, D_MODEL), DIL_WIDTH ** -0.5),
        'w_pc': nrm(ks[16], (DEPTH, NA_WIDTH, D_MODEL), NA_WIDTH ** -0.5),
        'w_out': nrm(ks[17], (DEPTH, D_MODEL, D_MODEL), D_MODEL ** -0.5),
    }


def reference(x_prompt, x_sample, c_prompt, c_sample, w_ada, b_ada, g_pre, g_post, w_in, g_q, w_uq, g_kv, w_ukv, rpb, w_pa, w_pb, w_pc, w_out):
    y_prompt = trunk(x_prompt, c_prompt, w_ada, b_ada, g_pre, g_post, w_in, g_q, w_uq, g_kv, w_ukv, rpb, w_pa, w_pb, w_pc, w_out)
    y_sample = trunk(x_sample, c_sample, w_ada, b_ada, g_pre, g_post, w_in, g_q, w_uq, g_kv, w_ukv, rpb, w_pa, w_pb, w_pc, w_out)
    return (y_prompt, y_sample)
```

```python
import functools

import jax
import jax.numpy as jnp
import numpy as np
from jax import lax
from jax.experimental import pallas as pl
from jax.experimental.pallas import tpu as pltpu

F32 = jnp.float32
BF16 = jnp.bfloat16

D_MODEL = 1024
DEPTH = 4
GRID_W = 64
ROPE_THETA = 10000.0
NORM_EPS = 1e-6
NEG_INF = -1e30

MLA_HEADS = 8
MLA_NOPE = 64
MLA_ROPE = 32
MLA_V = 64
MLA_DQK = MLA_NOPE + MLA_ROPE
MLA_Q_RANK = 384
MLA_KV_RANK = 256
MLA_WIDTH = MLA_HEADS * MLA_V
MLA_HEAD_PAD = 128

DIL_PAIRS = ((128, 1), (512, 4), (2048, 16))
DIL_HPG = 4
DIL_HD = 64
DIL_HEADS = 12
DIL_WIDTH = DIL_HPG * DIL_HD
DIL_BAND = 64

NA_HEADS = 8
NA_HD = 64
NA_KH = 8
NA_KW = 16
NA_WIDTH = NA_HEADS * NA_HD

LANES = 128
VMEM_LIMIT = 56 * 1024 * 1024

_CUTS = np.cumsum((MLA_Q_RANK, MLA_KV_RANK, MLA_ROPE, MLA_WIDTH, 3 * DIL_HEADS * DIL_HD, DIL_WIDTH,
                   3 * NA_HEADS * NA_HD, NA_WIDTH, 3 * D_MODEL))
ZA_WIDTH = MLA_Q_RANK + MLA_KV_RANK + LANES
GATE_WIDTH = MLA_WIDTH + DIL_WIDTH + NA_WIDTH
ZB_WIDTH = 3 * DIL_HEADS * DIL_HD
ZB_GROUP = 3 * DIL_WIDTH
ZC_WIDTH = 3 * NA_WIDTH

TM_IN = 512
TM_OUT = 256
MLA_TQ = 512
MLA_KC = 256
NA_ROWS_PER_STEP = 4


def _cparams(n_grid):
    return pltpu.CompilerParams(dimension_semantics=("arbitrary",) * n_grid, vmem_limit_bytes=VMEM_LIMIT)


def _sigmoid(x):
    return 0.5 * (1.0 + jnp.tanh(0.5 * x))


def _rope_lanes(x, cos, sin_lo, sin_hi, half):
    return x * cos + pltpu.roll(x, half, 1) * sin_hi + pltpu.roll(x, LANES - half, 1) * sin_lo


def _nt_dot(a, b):
    return lax.dot_general(a, b, (((1,), (1,)), ((), ())), preferred_element_type=F32)


def _ada_kernel(c_ref, w_ref, b_ref, o_ref):
    c = c_ref[...]
    c_act = (c * _sigmoid(c)).astype(BF16)
    o_ref[...] = jnp.dot(c_act, w_ref[...], preferred_element_type=F32) + b_ref[...]


def _ada(c, w_ada_bf, b_ada):
    b = c.shape[0]
    return pl.pallas_call(
        _ada_kernel,
        grid=(DEPTH,),
        in_specs=[pl.BlockSpec((b, D_MODEL), lambda l: (0, 0)),
                  pl.BlockSpec((None, D_MODEL, 3 * D_MODEL), lambda l: (l, 0, 0)),
                  pl.BlockSpec((None, 1, 3 * D_MODEL), lambda l: (l, 0, 0))],
        out_specs=pl.BlockSpec((None, b, 3 * D_MODEL), lambda l: (l, 0, 0)),
        out_shape=jax.ShapeDtypeStruct((DEPTH, b, 3 * D_MODEL), F32),
        compiler_params=_cparams(1),
        name="ada",
    )(c, w_ada_bf, b_ada.reshape(DEPTH, 1, 3 * D_MODEL))


def _modulated_norm(x32, g, mod):
    y = x32 * lax.rsqrt(jnp.mean(x32 * x32, axis=-1, keepdims=True) + NORM_EPS) * g
    return y * (1.0 + mod[1:2, :]) + mod[0:1, :]


def _prenorm_kernel(x_ref, g_ref, mod_ref, h_ref):
    h_ref[...] = _modulated_norm(x_ref[...], g_ref[...], mod_ref[...]).astype(BF16)


def _prenorm(x, g, mod):
    b, s, _ = x.shape
    tm = TM_IN
    return pl.pallas_call(
        _prenorm_kernel,
        grid=(b, s // tm),
        in_specs=[pl.BlockSpec((None, tm, D_MODEL), lambda bi, i: (bi, i, 0)),
                  pl.BlockSpec((1, D_MODEL), lambda bi, i: (0, 0)),
                  pl.BlockSpec((None, 3, D_MODEL), lambda bi, i: (bi, 0, 0))],
        out_specs=pl.BlockSpec((None, tm, D_MODEL), lambda bi, i: (bi, i, 0)),
        out_shape=jax.ShapeDtypeStruct((b, s, D_MODEL), BF16),
        compiler_params=_cparams(2),
        name="prenorm",
    )(x, g.reshape(1, D_MODEL), mod)


def _proj_kernel(h_ref, w_ref, o_ref):
    o_ref[...] = jnp.dot(h_ref[...], w_ref[...], preferred_element_type=F32).astype(BF16)


def _proj(h, w, name):
    b, s, _ = h.shape
    n = w.shape[1]
    tm = TM_IN
    return pl.pallas_call(
        _proj_kernel,
        grid=(b, s // tm),
        in_specs=[pl.BlockSpec((None, tm, D_MODEL), lambda bi, i: (bi, i, 0)),
                  pl.BlockSpec((D_MODEL, n), lambda bi, i: (0, 0))],
        out_specs=pl.BlockSpec((None, tm, n), lambda bi, i: (bi, i, 0)),
        out_shape=jax.ShapeDtypeStruct((b, s, n), BF16),
        compiler_params=_cparams(2),
        name=name,
    )(h, w)


def _proj_dil_kernel(h_ref, w_ref, cos_ref, slo_ref, shi_ref, o0_ref, o1_ref, o2_ref, z_ref):
    tm = h_ref.shape[0]
    z = jnp.dot(h_ref[...], w_ref[...], preferred_element_type=F32)
    cos, slo, shi = cos_ref[...], slo_ref[...], shi_ref[...]
    slabs_per_part = DIL_WIDTH // LANES
    for sl in range(ZB_WIDTH // LANES):
        x = z[:, sl * LANES:(sl + 1) * LANES]
        if (sl // slabs_per_part) % 3 != 2:
            x = _rope_lanes(x, cos, slo, shi, DIL_HD // 2)
        z_ref[sl] = x
    slabs_per_group = ZB_GROUP // LANES
    for gi, (o_ref, (_, d)) in enumerate(zip((o0_ref, o1_ref, o2_ref), DIL_PAIRS)):
        for r in range(d):
            rows = slice(None) if d == 1 else pl.ds(r, tm // d, stride=d)
            for c in range(slabs_per_group):
                o_ref[r, :, c * LANES:(c + 1) * LANES] = z_ref[gi * slabs_per_group + c, rows, :].astype(BF16)


def _proj_dil(h, w, tabs):
    b, s, _ = h.shape
    tm = TM_IN
    cos, slo, shi = tabs
    tab_spec = pl.BlockSpec((tm, LANES), lambda i, bi: (i, 0))
    out_shapes, out_specs = [], []
    for _, d in DIL_PAIRS:
        out_shapes.append(jax.ShapeDtypeStruct((b, d, s // d, ZB_GROUP), BF16))
        out_specs.append(pl.BlockSpec((None, d, tm // d, ZB_GROUP), lambda i, bi: (bi, 0, i, 0)))
    return pl.pallas_call(
        _proj_dil_kernel,
        grid=(s // tm, b),
        in_specs=[pl.BlockSpec((None, tm, D_MODEL), lambda i, bi: (bi, i, 0)),
                  pl.BlockSpec((D_MODEL, ZB_WIDTH), lambda i, bi: (0, 0)),
                  tab_spec, tab_spec, tab_spec],
        out_specs=out_specs,
        out_shape=out_shapes,
        scratch_shapes=[pltpu.VMEM((ZB_WIDTH // LANES, tm, LANES), F32)],
        compiler_params=_cparams(2),
        name="proj_dil",
    )(h, w, cos, slo, shi)


def _mla_prep_kernel(za_ref, gq_ref, gkv_ref, wq_ref, wk_ref, wvt_ref,
                     cq_ref, sloq_ref, shiq_ref, ck_ref, slok_ref, shik_ref,
                     q_ref, k_ref, vt_ref):
    def norm(x, g):
        x32 = x.astype(F32)
        return (x32 * lax.rsqrt(jnp.mean(x32 * x32, axis=-1, keepdims=True) + NORM_EPS) * g).astype(BF16)

    cqn = norm(za_ref[:, 0:MLA_Q_RANK], gq_ref[...])
    ckvn = norm(za_ref[:, MLA_Q_RANK:MLA_Q_RANK + MLA_KV_RANK], gkv_ref[...])
    kr = za_ref[:, MLA_Q_RANK + MLA_KV_RANK:ZA_WIDTH].astype(F32)
    kr = _rope_lanes(kr, ck_ref[...], slok_ref[...], shik_ref[...], MLA_ROPE // 2)
    q = jnp.dot(cqn, wq_ref[...], preferred_element_type=F32)
    k = jnp.dot(ckvn, wk_ref[...], preferred_element_type=F32)
    cq, sloq, shiq = cq_ref[...], sloq_ref[...], shiq_ref[...]
    for h in range(MLA_HEADS):
        cols = slice(h * MLA_HEAD_PAD, (h + 1) * MLA_HEAD_PAD)
        q_ref[:, cols] = _rope_lanes(q[:, cols], cq, sloq, shiq, MLA_ROPE // 2).astype(BF16)
        k_ref[:, cols] = (k[:, cols] + kr).astype(BF16)
    vt_ref[...] = _nt_dot(wvt_ref[...], ckvn).astype(BF16)


def _mla_prep(za, g_q, g_kv, wq, wk, wvt, qtabs, ktabs):
    b, s, _ = za.shape
    tm = MLA_KC
    hp = MLA_HEADS * MLA_HEAD_PAD
    tab_spec = pl.BlockSpec((tm, LANES), lambda i, bi: (i, 0))
    const = lambda shape: pl.BlockSpec(shape, lambda i, bi: (0,) * len(shape))
    return pl.pallas_call(
        _mla_prep_kernel,
        grid=(s // tm, b),
        in_specs=[pl.BlockSpec((None, tm, ZA_WIDTH), lambda i, bi: (bi, i, 0)),
                  const((1, MLA_Q_RANK)), const((1, MLA_KV_RANK)),
                  const((MLA_Q_RANK, hp)), const((MLA_KV_RANK, hp)), const((MLA_WIDTH, MLA_KV_RANK)),
                  tab_spec, tab_spec, tab_spec, tab_spec, tab_spec, tab_spec],
        out_specs=[pl.BlockSpec((None, tm, hp), lambda i, bi: (bi, i, 0)),
                   pl.BlockSpec((None, tm, hp), lambda i, bi: (bi, i, 0)),
                   pl.BlockSpec((None, None, MLA_WIDTH, tm), lambda i, bi: (bi, i, 0, 0))],
        out_shape=[jax.ShapeDtypeStruct((b, s, hp), BF16),
                   jax.ShapeDtypeStruct((b, s, hp), BF16),
                   jax.ShapeDtypeStruct((b, s // tm, MLA_WIDTH, tm), BF16)],
        compiler_params=_cparams(2),
        name="mla_prep",
    )(za, g_q.reshape(1, -1), g_kv.reshape(1, -1), wq, wk, wvt, *qtabs, *ktabs)


def _mla_attn_kernel(q_ref, k_ref, vt_ref, o_ref):
    tq = q_ref.shape[0]
    n_chunks = vt_ref.shape[0]
    kc = vt_ref.shape[2]
    outs = []
    for hh in range(2):
        q = q_ref[:, hh * MLA_HEAD_PAD:(hh + 1) * MLA_HEAD_PAD]

        def body(j, carry, hh=hh, q=q):
            m, l, acc = carry
            start = pl.multiple_of(j * kc, kc)
            k = k_ref[pl.ds(start, kc), hh * MLA_HEAD_PAD:(hh + 1) * MLA_HEAD_PAD]
            st = _nt_dot(k, q)
            m_new = jnp.maximum(m, jnp.max(st, axis=0, keepdims=True))
            alpha = jnp.exp(m - m_new)
            p = jnp.exp(st - m_new)
            l = alpha * l + jnp.sum(p, axis=0, keepdims=True)
            vt = vt_ref[j, hh * MLA_V:(hh + 1) * MLA_V, :]
            acc = alpha * acc + jnp.dot(vt, p.astype(BF16), preferred_element_type=F32)
            return m_new, l, acc

        init = (jnp.full((1, tq), NEG_INF, F32), jnp.zeros((1, tq), F32), jnp.zeros((MLA_V, tq), F32))
        m, l, acc = lax.fori_loop(0, n_chunks, body, init)
        outs.append(acc / l)
    o_ref[...] = jnp.concatenate(outs, axis=0).T.astype(BF16)


def _mla_attn(q, k, vt):
    b, s, _ = q.shape
    n_chunks, kc = vt.shape[1], vt.shape[3]
    tq = min(MLA_TQ, s)
    pair_w = 2 * MLA_HEAD_PAD
    return pl.pallas_call(
        _mla_attn_kernel,
        grid=(b, MLA_HEADS // 2, s // tq),
        in_specs=[pl.BlockSpec((None, tq, pair_w), lambda bi, p, i: (bi, i, p)),
                  pl.BlockSpec((None, s, pair_w), lambda bi, p, i: (bi, 0, p)),
                  pl.BlockSpec((None, n_chunks, 2 * MLA_V, kc), lambda bi, p, i: (bi, 0, p, 0))],
        out_specs=pl.BlockSpec((None, tq, 2 * MLA_V), lambda bi, p, i: (bi, i, p)),
        out_shape=jax.ShapeDtypeStruct((b, s, MLA_WIDTH), BF16),
        compiler_params=_cparams(3),
        name="mla_attn",
    )(q, k, vt)


def _softmax_pv(s, v_pair):
    m = jnp.max(s, axis=-1, keepdims=True)
    p = jnp.exp(s - m)
    l = jnp.sum(p, axis=-1, keepdims=True)
    o = jnp.dot(p.astype(BF16), v_pair, preferred_element_type=F32) / l
    return o, m + jnp.log(l)


def _dil_attn_kernel(qkv_ref, o_ref, lse_ref, *, tq, win):
    length = qkv_ref.shape[0]
    q0 = pl.multiple_of(pl.program_id(2) * tq, tq)
    start = pl.multiple_of(jnp.clip(q0 - DIL_BAND, 0, length - win), DIL_BAND)
    qpos = q0 + lax.broadcasted_iota(jnp.int32, (tq, win), 0)
    kpos = start + lax.broadcasted_iota(jnp.int32, (tq, win), 1)
    valid = jnp.abs(kpos - qpos) <= DIL_BAND
    lane = lax.broadcasted_iota(jnp.int32, (tq, LANES), 1)
    low_half = lane < DIL_HD
    head_lanes = (low_half, lane >= DIL_HD)
    for p in range(DIL_WIDTH // LANES):
        q = qkv_ref[pl.ds(q0, tq), p * LANES:(p + 1) * LANES]
        k = qkv_ref[pl.ds(start, win), DIL_WIDTH + p * LANES:DIL_WIDTH + (p + 1) * LANES]
        v = qkv_ref[pl.ds(start, win), 2 * DIL_WIDTH + p * LANES:2 * DIL_WIDTH + (p + 1) * LANES]
        res = []
        for hh in range(2):
            qm = jnp.where(head_lanes[hh], q, jnp.zeros_like(q))
            s = jnp.where(valid, _nt_dot(qm, k), NEG_INF)
            res.append(_softmax_pv(s, v))
        o_ref[:, p * LANES:(p + 1) * LANES] = jnp.where(low_half, res[0][0], res[1][0]).astype(BF16)
        lse_ref[:, p * LANES:(p + 1) * LANES] = jnp.where(low_half, res[0][1], res[1][1])


def _dil_attn(zb, d):
    b, _, length, _ = zb.shape
    tq = min(2 * DIL_BAND, length)
    win = min(4 * DIL_BAND, length)
    return pl.pallas_call(
        functools.partial(_dil_attn_kernel, tq=tq, win=win),
        grid=(b, d, length // tq),
        in_specs=[pl.BlockSpec((None, None, length, ZB_GROUP), lambda bi, r, i: (bi, r, 0, 0))],
        out_specs=[pl.BlockSpec((None, None, tq, DIL_WIDTH), lambda bi, r, i: (bi, r, i, 0)),
                   pl.BlockSpec((None, None, tq, DIL_WIDTH), lambda bi, r, i: (bi, r, i, 0))],
        out_shape=[jax.ShapeDtypeStruct((b, d, length, DIL_WIDTH), BF16),
                   jax.ShapeDtypeStruct((b, d, length, DIL_WIDTH), F32)],
        compiler_params=_cparams(3),
        name=f"dil_attn_d{d}",
    )(zb)


N_BIAS_ROWS = 2 * NA_KH - 2


def _na_bias_kernel(rpb_ref, t_ref):
    h = pl.program_id(0)
    shape = (GRID_W, LANES)
    q = lax.broadcasted_iota(jnp.int32, shape, 0)
    lane = lax.broadcasted_iota(jnp.int32, shape, 1)
    w = lane & (GRID_W - 1)
    first = lane < GRID_W
    cs = jnp.clip(q - NA_KW // 2, 0, GRID_W - NA_KW)
    inside = (w >= cs) & (w < cs + NA_KW)
    off = w - q + NA_KW - 1
    for ro in range(N_BIAS_ROWS):
        acc = jnp.full(shape, NEG_INF, F32)
        for kk in range(2 * NA_KW - 1):
            val = jnp.where(first, rpb_ref[h, ro, kk], rpb_ref[h, ro + 1, kk])
            acc = jnp.where(off == kk, val, acc)
        t_ref[ro] = jnp.where(inside, acc, NEG_INF)


def _na_bias(rpb_l):
    return pl.pallas_call(
        _na_bias_kernel,
        grid=(NA_HEADS,),
        in_specs=[pl.BlockSpec(memory_space=pltpu.SMEM)],
        out_specs=pl.BlockSpec((None, N_BIAS_ROWS, GRID_W, LANES), lambda h: (h, 0, 0, 0)),
        out_shape=jax.ShapeDtypeStruct((NA_HEADS, N_BIAS_ROWS, GRID_W, LANES), F32),
        compiler_params=_cparams(1),
        name="na_bias",
    )(rpb_l)


def _na_attn_kernel(q_ref, k_ref, v_ref, t_ref, o_ref, *, rows):
    step = pl.program_id(1)
    n_keys = NA_KH * GRID_W
    lane = lax.broadcasted_iota(jnp.int32, (GRID_W, LANES), 1)
    low_half = lane < NA_HD
    head_lanes = (low_half, lane >= NA_HD)
    for i in range(NA_ROWS_PER_STEP):
        r = step * NA_ROWS_PER_STEP + i
        rs = jnp.clip(r - NA_KH // 2, 0, rows - NA_KH)
        base = rs - r + NA_KH - 1
        k0 = pl.multiple_of(rs * GRID_W, GRID_W)
        for p in range(NA_WIDTH // LANES):
            cols = slice(p * LANES, (p + 1) * LANES)
            q = q_ref[i * GRID_W:(i + 1) * GRID_W, cols]
            k = k_ref[pl.ds(k0, n_keys), cols]
            v = v_ref[pl.ds(k0, n_keys), cols]
            res = []
            for hh in range(2):
                head = 2 * p + hh
                qm = jnp.where(head_lanes[hh], q, jnp.zeros_like(q))
                bias = jnp.concatenate([t_ref[head, base + 2 * a] for a in range(NA_KH // 2)], axis=1)
                res.append(_softmax_pv(_nt_dot(qm, k) + bias, v)[0])
            o_ref[i * GRID_W:(i + 1) * GRID_W, cols] = jnp.where(low_half, res[0], res[1]).astype(BF16)


def _na_attn(zc, table):
    b, s, _ = zc.shape
    rows = s // GRID_W
    tq = NA_ROWS_PER_STEP * GRID_W
    return pl.pallas_call(
        functools.partial(_na_attn_kernel, rows=rows),
        grid=(b, rows // NA_ROWS_PER_STEP),
        in_specs=[pl.BlockSpec((None, tq, NA_WIDTH), lambda bi, i: (bi, i, 0)),
                  pl.BlockSpec((None, s, NA_WIDTH), lambda bi, i: (bi, 0, 1)),
                  pl.BlockSpec((None, s, NA_WIDTH), lambda bi, i: (bi, 0, 2)),
                  pl.BlockSpec((NA_HEADS, N_BIAS_ROWS, GRID_W, LANES), lambda bi, i: (0, 0, 0, 0))],
        out_specs=pl.BlockSpec((None, tq, NA_WIDTH), lambda bi, i: (bi, i, 0)),
        out_shape=jax.ShapeDtypeStruct((b, s, NA_WIDTH), BF16),
        compiler_params=_cparams(2),
        name="na_attn",
    )(zc, zc, zc, table)


def _out_kernel(*refs, emit_h):
    (x_ref, oa_ref, gates_ref, oc_ref, merge_ref,
     ob0_ref, ob1_ref, ob2_ref, ls0_ref, ls1_ref, ls2_ref,
     wpa_ref, wpb_ref, wpc_ref, wout_ref, gpost_ref, mod_ref) = refs[:17]
    rest = refs[17:]
    if emit_h:
        gnext_ref, modn_ref, y_ref, h_ref, so_ref, sl_ref = rest
    else:
        y_ref, so_ref, sl_ref = rest
    tm = x_ref.shape[0]

    for gi, (ob_ref, ls_ref, (_, d)) in enumerate(zip((ob0_ref, ob1_ref, ob2_ref), (ls0_ref, ls1_ref, ls2_ref),
                                                     DIL_PAIRS)):
        for r in range(d):
            rows = slice(None) if d == 1 else pl.ds(r, tm // d, stride=d)
            for c in range(DIL_WIDTH // LANES):
                cols = slice(c * LANES, (c + 1) * LANES)
                so_ref[gi, c, rows, :] = ob_ref[r, :, cols].astype(F32)
                sl_ref[gi, c, rows, :] = ls_ref[r, :, cols]
    slabs = range(DIL_WIDTH // LANES)
    lse = [jnp.concatenate([sl_ref[gi, c] for c in slabs], axis=1) for gi in range(3)]
    o_g = [jnp.concatenate([so_ref[gi, c] for c in slabs], axis=1) for gi in range(3)]
    mx = jnp.maximum(jnp.maximum(lse[0], lse[1]), lse[2])
    e = [jnp.exp(x - mx) for x in lse]
    o_b = (e[0] * o_g[0] + e[1] * o_g[1] + e[2] * o_g[2]) / (e[0] + e[1] + e[2])

    def gated(o, lo, hi):
        g = gates_ref[:, lo:hi].astype(F32)
        return (o * (g * _sigmoid(g))).astype(BF16)

    a = gated(oa_ref[...].astype(F32), 0, MLA_WIDTH)
    bb = gated(o_b, MLA_WIDTH, MLA_WIDTH + DIL_WIDTH)
    c = gated(oc_ref[...].astype(F32), MLA_WIDTH + DIL_WIDTH, GATE_WIDTH)
    mixed = (_sigmoid(merge_ref[:, 0:D_MODEL].astype(F32))
             * jnp.dot(a, wpa_ref[...], preferred_element_type=F32))
    mixed += (_sigmoid(merge_ref[:, D_MODEL:2 * D_MODEL].astype(F32))
              * jnp.dot(bb, wpb_ref[...], preferred_element_type=F32))
    mixed += (_sigmoid(merge_ref[:, 2 * D_MODEL:3 * D_MODEL].astype(F32))
              * jnp.dot(c, wpc_ref[...], preferred_element_type=F32))
    out = jnp.dot(mixed.astype(BF16), wout_ref[...], preferred_element_type=F32)
    normed = out * lax.rsqrt(jnp.mean(out * out, axis=-1, keepdims=True) + NORM_EPS) * gpost_ref[...]
    y = x_ref[...] + mod_ref[2:3, :] * normed
    y_ref[...] = y
    if emit_h:
        h_ref[...] = _modulated_norm(y, gnext_ref[...], modn_ref[...]).astype(BF16)


def _out_layer(x, oa, gates, oc, merge, obs, lses, wpa, wpb, wpc, wout, g_post, mod, g_next, mod_next):
    b, s, _ = x.shape
    tm = TM_OUT
    emit_h = g_next is not None
    tok = lambda n: pl.BlockSpec((None, tm, n), lambda bi, i: (bi, i, 0))
    const = lambda shape: pl.BlockSpec(shape, lambda bi, i: (0,) * len(shape))
    modspec = pl.BlockSpec((None, 3, D_MODEL), lambda bi, i: (bi, 0, 0))
    cls = [pl.BlockSpec((None, d, tm // d, DIL_WIDTH), lambda bi, i: (bi, 0, i, 0)) for _, d in DIL_PAIRS]
    in_specs = [tok(D_MODEL), tok(MLA_WIDTH), tok(GATE_WIDTH), tok(NA_WIDTH), tok(3 * D_MODEL),
                *cls, *cls,
                const((MLA_WIDTH, D_MODEL)), const((DIL_WIDTH, D_MODEL)), const((NA_WIDTH, D_MODEL)),
                const((D_MODEL, D_MODEL)), const((1, D_MODEL)), modspec]
    args = [x, oa, gates, oc, merge, *obs, *lses, wpa, wpb, wpc, wout, g_post.reshape(1, D_MODEL), mod]
    out_specs = [tok(D_MODEL)]
    out_shape = [jax.ShapeDtypeStruct((b, s, D_MODEL), F32)]
    if emit_h:
        in_specs += [const((1, D_MODEL)), modspec]
        args += [g_next.reshape(1, D_MODEL), mod_next]
        out_specs.append(tok(D_MODEL))
        out_shape.append(jax.ShapeDtypeStruct((b, s, D_MODEL), BF16))
    res = pl.pallas_call(
        functools.partial(_out_kernel, emit_h=emit_h),
        grid=(b, s // tm),
        in_specs=in_specs,
        out_specs=out_specs,
        out_shape=out_shape,
        scratch_shapes=[pltpu.VMEM((3, DIL_WIDTH // LANES, tm, LANES), F32)] * 2,
        compiler_params=_cparams(2),
        name="out_layer",
    )(*args)
    return (res[0], res[1]) if emit_h else (res[0], None)


def _rope_tables(s, head_dim, lane_of_dim0, period, scale):
    half = head_dim // 2
    inv = ROPE_THETA ** (-jnp.arange(half, dtype=F32) * 2.0 / head_dim)
    ang = jnp.arange(s, dtype=F32)[:, None] * inv[None, :]
    cos, sin = jnp.cos(ang), jnp.sin(ang)
    lane = np.arange(LANES)
    dim = (lane - lane_of_dim0) % period
    in_rope = dim < head_dim
    idx = dim % half
    cos_t = jnp.where(in_rope[None, :], cos[:, idx], 1.0) * scale
    sin_t = sin[:, idx] * scale
    sin_lo = jnp.where((in_rope & (dim < half))[None, :], -sin_t, 0.0)
    sin_hi = jnp.where((in_rope & (dim >= half))[None, :], sin_t, 0.0)
    return cos_t.astype(F32), sin_lo.astype(F32), sin_hi.astype(F32)


def _layout_weights(w_in, w_uq, w_ukv):
    c = [0] + [int(v) for v in _CUTS]
    cq, ckv, kr, gate_a, qkv_b, gate_b, qkv_c, gate_c, merge = [w_in[:, :, c[i]:c[i + 1]] for i in range(9)]
    zeros = lambda n: jnp.zeros((DEPTH, D_MODEL, n), w_in.dtype)
    w_a = jnp.concatenate([cq, ckv, zeros(MLA_NOPE), kr, zeros(LANES - MLA_NOPE - MLA_ROPE)], axis=-1)
    w_g = jnp.concatenate([gate_a, gate_b, gate_c], axis=-1)
    qkv_b = qkv_b.reshape(DEPTH, D_MODEL, 3, len(DIL_PAIRS), DIL_WIDTH)
    qkv_b = qkv_b * jnp.array([DIL_HD ** -0.5, 1.0, 1.0], w_in.dtype)[None, None, :, None, None]
    w_b = qkv_b.transpose(0, 1, 3, 2, 4).reshape(DEPTH, D_MODEL, ZB_WIDTH)
    qkv_c = qkv_c.reshape(DEPTH, D_MODEL, 3, NA_WIDTH)
    w_c = (qkv_c * jnp.array([NA_HD ** -0.5, 1.0, 1.0], w_in.dtype)[None, None, :, None]).reshape(DEPTH, D_MODEL, ZC_WIDTH)
    uq = w_uq.reshape(DEPTH, MLA_Q_RANK, MLA_HEADS, MLA_DQK)
    uq = jnp.pad(uq, ((0, 0), (0, 0), (0, 0), (0, MLA_HEAD_PAD - MLA_DQK)))
    w_q = uq.reshape(DEPTH, MLA_Q_RANK, MLA_HEADS * MLA_HEAD_PAD)
    ukv = w_ukv.reshape(DEPTH, MLA_KV_RANK, MLA_HEADS, MLA_NOPE + MLA_V)
    uk = jnp.pad(ukv[..., :MLA_NOPE], ((0, 0), (0, 0), (0, 0), (0, MLA_HEAD_PAD - MLA_NOPE)))
    w_k = uk.reshape(DEPTH, MLA_KV_RANK, MLA_HEADS * MLA_HEAD_PAD)
    w_vt = ukv[..., MLA_NOPE:].reshape(DEPTH, MLA_KV_RANK, MLA_WIDTH).transpose(0, 2, 1)
    bf = lambda w: w.astype(BF16)
    return dict(a=bf(w_a), g=bf(w_g), b=bf(w_b), c=bf(w_c), m=bf(merge), q=bf(w_q), k=bf(w_k), vt=bf(w_vt))


def _trunk(x, c, p):
    b, s, _ = x.shape
    mods = _ada(c, p["w_ada"], p["b_ada"]).reshape(DEPTH, b, 3, D_MODEL)
    dil_tabs = _rope_tables(s, DIL_HD, 0, DIL_HD, 1.0)
    q_tabs = _rope_tables(s, MLA_ROPE, MLA_NOPE, LANES, MLA_DQK ** -0.5)
    k_tabs = _rope_tables(s, MLA_ROPE, MLA_NOPE, LANES, 1.0)
    w = p["w"]
    h = _prenorm(x, p["g_pre"][0], mods[0])
    for l in range(DEPTH):
        za = _proj(h, w["a"][l], "proj_mla")
        gates = _proj(h, w["g"][l], "proj_gates")
        zc = _proj(h, w["c"][l], "proj_na")
        merge = _proj(h, w["m"][l], "proj_merge")
        zbs = _proj_dil(h, w["b"][l], dil_tabs)
        q, k, vt = _mla_prep(za, p["g_q"][l], p["g_kv"][l], w["q"][l], w["k"][l], w["vt"][l], q_tabs, k_tabs)
        oa = _mla_attn(q, k, vt)
        dil = [_dil_attn(zb, d) for zb, (_, d) in zip(zbs, DIL_PAIRS)]
        oc = _na_attn(zc, p["na_bias"][l])
        last = l == DEPTH - 1
        x, h = _out_layer(x, oa, gates, oc, merge, [o for o, _ in dil], [ls for _, ls in dil],
                          p["w_pa"][l], p["w_pb"][l], p["w_pc"][l], p["w_out"][l], p["g_post"][l], mods[l],
                          None if last else p["g_pre"][l + 1], None if last else mods[l + 1])
    return x


def kernel(x_prompt, x_sample, c_prompt, c_sample, w_ada, b_ada, g_pre, g_post, w_in, g_q, w_uq, g_kv, w_ukv, rpb,
           w_pa, w_pb, w_pc, w_out):
    p = dict(w_ada=w_ada.astype(BF16), b_ada=b_ada, g_pre=g_pre, g_post=g_post, g_q=g_q, g_kv=g_kv,
             w=_layout_weights(w_in, w_uq, w_ukv),
             na_bias=[_na_bias(rpb[l]) for l in range(DEPTH)],
             w_pa=w_pa.astype(BF16), w_pb=w_pb.astype(BF16), w_pc=w_pc.astype(BF16), w_out=w_out.astype(BF16))
    return (_trunk(x_prompt, c_prompt, p), _trunk(x_sample, c_sample, p))
```

```python
import functools

import jax
import jax.numpy as jnp
import numpy as np
from jax import lax
from jax.experimental import pallas as pl
from jax.experimental.pallas import tpu as pltpu

F32 = jnp.float32
BF16 = jnp.bfloat16

D_MODEL = 1024
DEPTH = 4
GRID_W = 64
ROPE_THETA = 10000.0
NORM_EPS = 1e-6
NEG_INF = -1e30
LOG2_E = float(np.log2(np.e))

MLA_HEADS = 8
MLA_NOPE = 64
MLA_ROPE = 32
MLA_V = 64
MLA_DQK = MLA_NOPE + MLA_ROPE
MLA_Q_RANK = 384
MLA_KV_RANK = 256
MLA_WIDTH = MLA_HEADS * MLA_V
MLA_HEAD_PAD = 128

DIL_PAIRS = ((128, 1), (512, 4), (2048, 16))
DIL_HPG = 4
DIL_HD = 64
DIL_HEADS = 12
DIL_WIDTH = DIL_HPG * DIL_HD
DIL_BAND = 64

NA_HEADS = 8
NA_HD = 64
NA_KH = 8
NA_KW = 16
NA_WIDTH = NA_HEADS * NA_HD

LANES = 128
VMEM_LIMIT = 56 * 1024 * 1024

_CUTS = np.cumsum((MLA_Q_RANK, MLA_KV_RANK, MLA_ROPE, MLA_WIDTH, 3 * DIL_HEADS * DIL_HD, DIL_WIDTH,
                   3 * NA_HEADS * NA_HD, NA_WIDTH, 3 * D_MODEL))
ZA_WIDTH = MLA_Q_RANK + MLA_KV_RANK + LANES
GATE_WIDTH = MLA_WIDTH + DIL_WIDTH + NA_WIDTH
ZB_WIDTH = 3 * DIL_HEADS * DIL_HD
ZB_GROUP = 3 * DIL_WIDTH
ZC_WIDTH = 3 * NA_WIDTH

TM_IN = 512
TM_OUT = 256
MLA_TQ = 512
MLA_KC = 512
MLA_UNROLL = 4


def _cparams(n_grid):
    return pltpu.CompilerParams(dimension_semantics=("arbitrary",) * n_grid, vmem_limit_bytes=VMEM_LIMIT)


def _sigmoid(x):
    return 0.5 * (1.0 + jnp.tanh(0.5 * x))


def _rope_lanes(x, cos, sin_lo, sin_hi, half):
    return x * cos + pltpu.roll(x, half, 1) * sin_hi + pltpu.roll(x, LANES - half, 1) * sin_lo


def _nt_dot(a, b):
    return lax.dot_general(a, b, (((1,), (1,)), ((), ())), preferred_element_type=F32)


def _ada_kernel(c_ref, w_ref, b_ref, o_ref):
    c = c_ref[...]
    c_act = (c * _sigmoid(c)).astype(BF16)
    o_ref[...] = jnp.dot(c_act, w_ref[...], preferred_element_type=F32) + b_ref[...]


def _ada(c, w_ada_bf, b_ada):
    b = c.shape[0]
    return pl.pallas_call(
        _ada_kernel,
        grid=(DEPTH,),
        in_specs=[pl.BlockSpec((b, D_MODEL), lambda l: (0, 0)),
                  pl.BlockSpec((None, D_MODEL, 3 * D_MODEL), lambda l: (l, 0, 0)),
                  pl.BlockSpec((None, 1, 3 * D_MODEL), lambda l: (l, 0, 0))],
        out_specs=pl.BlockSpec((None, b, 3 * D_MODEL), lambda l: (l, 0, 0)),
        out_shape=jax.ShapeDtypeStruct((DEPTH, b, 3 * D_MODEL), F32),
        compiler_params=_cparams(1),
        name="ada",
    )(c, w_ada_bf, b_ada.reshape(DEPTH, 1, 3 * D_MODEL))


def _modulated_norm(x32, g, mod):
    y = x32 * lax.rsqrt(jnp.mean(x32 * x32, axis=-1, keepdims=True) + NORM_EPS) * g
    return y * (1.0 + mod[1:2, :]) + mod[0:1, :]


def _prenorm_kernel(x_ref, g_ref, mod_ref, h_ref):
    h_ref[...] = _modulated_norm(x_ref[...], g_ref[...], mod_ref[...]).astype(BF16)


def _prenorm(x, g, mod):
    b, s, _ = x.shape
    tm = TM_IN
    return pl.pallas_call(
        _prenorm_kernel,
        grid=(b, s // tm),
        in_specs=[pl.BlockSpec((None, tm, D_MODEL), lambda bi, i: (bi, i, 0)),
                  pl.BlockSpec((1, D_MODEL), lambda bi, i: (0, 0)),
                  pl.BlockSpec((None, 3, D_MODEL), lambda bi, i: (bi, 0, 0))],
        out_specs=pl.BlockSpec((None, tm, D_MODEL), lambda bi, i: (bi, i, 0)),
        out_shape=jax.ShapeDtypeStruct((b, s, D_MODEL), BF16),
        compiler_params=_cparams(2),
        name="prenorm",
    )(x, g.reshape(1, D_MODEL), mod)


def _proj_kernel(h_ref, w_ref, o_ref):
    o_ref[...] = jnp.dot(h_ref[...], w_ref[...], preferred_element_type=F32).astype(BF16)


def _proj(h, w, name):
    b, s, _ = h.shape
    n = w.shape[1]
    tm = TM_IN
    return pl.pallas_call(
        _proj_kernel,
        grid=(b, s // tm),
        in_specs=[pl.BlockSpec((None, tm, D_MODEL), lambda bi, i: (bi, i, 0)),
                  pl.BlockSpec((D_MODEL, n), lambda bi, i: (0, 0))],
        out_specs=pl.BlockSpec((None, tm, n), lambda bi, i: (bi, i, 0)),
        out_shape=jax.ShapeDtypeStruct((b, s, n), BF16),
        compiler_params=_cparams(2),
        name=name,
    )(h, w)


def _proj_dil_kernel(h_ref, w_ref, cos_ref, slo_ref, shi_ref, o0_ref, o1_ref, o2_ref, z_ref):
    tm = h_ref.shape[0]
    z = jnp.dot(h_ref[...], w_ref[...], preferred_element_type=F32)
    cos, slo, shi = cos_ref[...], slo_ref[...], shi_ref[...]
    slabs_per_part = DIL_WIDTH // LANES
    for sl in range(ZB_WIDTH // LANES):
        x = z[:, sl * LANES:(sl + 1) * LANES]
        if (sl // slabs_per_part) % 3 != 2:
            x = _rope_lanes(x, cos, slo, shi, DIL_HD // 2)
        z_ref[sl] = x
    slabs_per_group = ZB_GROUP // LANES
    for gi, (o_ref, (_, d)) in enumerate(zip((o0_ref, o1_ref, o2_ref), DIL_PAIRS)):
        for r in range(d):
            rows = slice(None) if d == 1 else pl.ds(r, tm // d, stride=d)
            for c in range(slabs_per_group):
                o_ref[r, :, c * LANES:(c + 1) * LANES] = z_ref[gi * slabs_per_group + c, rows, :].astype(BF16)


def _proj_dil(h, w, tabs):
    b, s, _ = h.shape
    tm = TM_IN
    cos, slo, shi = tabs
    tab_spec = pl.BlockSpec((tm, LANES), lambda i, bi: (i, 0))
    out_shapes, out_specs = [], []
    for _, d in DIL_PAIRS:
        out_shapes.append(jax.ShapeDtypeStruct((b, d, s // d, ZB_GROUP), BF16))
        out_specs.append(pl.BlockSpec((None, d, tm // d, ZB_GROUP), lambda i, bi: (bi, 0, i, 0)))
    return pl.pallas_call(
        _proj_dil_kernel,
        grid=(s // tm, b),
        in_specs=[pl.BlockSpec((None, tm, D_MODEL), lambda i, bi: (bi, i, 0)),
                  pl.BlockSpec((D_MODEL, ZB_WIDTH), lambda i, bi: (0, 0)),
                  tab_spec, tab_spec, tab_spec],
        out_specs=out_specs,
        out_shape=out_shapes,
        scratch_shapes=[pltpu.VMEM((ZB_WIDTH // LANES, tm, LANES), F32)],
        compiler_params=_cparams(2),
        name="proj_dil",
    )(h, w, cos, slo, shi)


def _mla_prep_kernel(za_ref, gq_ref, gkv_ref, wq_ref, wk_ref, wvt_ref,
                     cq_ref, sloq_ref, shiq_ref, ck_ref, slok_ref, shik_ref,
                     q_ref, k_ref, vt_ref):
    def norm(x, g):
        x32 = x.astype(F32)
        return (x32 * lax.rsqrt(jnp.mean(x32 * x32, axis=-1, keepdims=True) + NORM_EPS) * g).astype(BF16)

    cqn = norm(za_ref[:, 0:MLA_Q_RANK], gq_ref[...])
    ckvn = norm(za_ref[:, MLA_Q_RANK:MLA_Q_RANK + MLA_KV_RANK], gkv_ref[...])
    kr = za_ref[:, MLA_Q_RANK + MLA_KV_RANK:ZA_WIDTH].astype(F32)
    kr = _rope_lanes(kr, ck_ref[...], slok_ref[...], shik_ref[...], MLA_ROPE // 2)
    q = jnp.dot(cqn, wq_ref[...], preferred_element_type=F32)
    k = jnp.dot(ckvn, wk_ref[...], preferred_element_type=F32)
    cq, sloq, shiq = cq_ref[...], sloq_ref[...], shiq_ref[...]
    for h in range(MLA_HEADS):
        cols = slice(h * MLA_HEAD_PAD, (h + 1) * MLA_HEAD_PAD)
        q_ref[:, cols] = _rope_lanes(q[:, cols], cq, sloq, shiq, MLA_ROPE // 2).astype(BF16)
        k_ref[:, cols] = (k[:, cols] + kr).astype(BF16)
    vt_ref[...] = _nt_dot(wvt_ref[...], ckvn).astype(BF16)


def _mla_prep(za, g_q, g_kv, wq, wk, wvt, qtabs, ktabs):
    b, s, _ = za.shape
    tm = MLA_KC
    hp = MLA_HEADS * MLA_HEAD_PAD
    tab_spec = pl.BlockSpec((tm, LANES), lambda i, bi: (i, 0))
    const = lambda shape: pl.BlockSpec(shape, lambda i, bi: (0,) * len(shape))
    return pl.pallas_call(
        _mla_prep_kernel,
        grid=(s // tm, b),
        in_specs=[pl.BlockSpec((None, tm, ZA_WIDTH), lambda i, bi: (bi, i, 0)),
                  const((1, MLA_Q_RANK)), const((1, MLA_KV_RANK)),
                  const((MLA_Q_RANK, hp)), const((MLA_KV_RANK, hp)), const((MLA_WIDTH, MLA_KV_RANK)),
                  tab_spec, tab_spec, tab_spec, tab_spec, tab_spec, tab_spec],
        out_specs=[pl.BlockSpec((None, tm, hp), lambda i, bi: (bi, i, 0)),
                   pl.BlockSpec((None, tm, hp), lambda i, bi: (bi, i, 0)),
                   pl.BlockSpec((None, None, MLA_WIDTH, tm), lambda i, bi: (bi, i, 0, 0))],
        out_shape=[jax.ShapeDtypeStruct((b, s, hp), BF16),
                   jax.ShapeDtypeStruct((b, s, hp), BF16),
                   jax.ShapeDtypeStruct((b, s // tm, MLA_WIDTH, tm), BF16)],
        compiler_params=_cparams(2),
        name="mla_prep",
    )(za, g_q.reshape(1, -1), g_kv.reshape(1, -1), wq, wk, wvt, *qtabs, *ktabs)


MLA_ONES_ROWS = 16


def _mla_attn_kernel(q_ref, k_ref, vt_ref, o_ref, s_ref):
    tq = q_ref.shape[0]
    n_chunks, _, kc = vt_ref.shape
    ones = jnp.ones((MLA_ONES_ROWS, kc), BF16)
    acc_rows = MLA_V + MLA_ONES_ROWS

    def score_step(j, mx, h):
        start = pl.multiple_of(j * kc, kc)
        cols = slice(h * MLA_HEAD_PAD, (h + 1) * MLA_HEAD_PAD)
        st = _nt_dot(k_ref[pl.ds(start, kc), cols], q_ref[:, cols])
        s_ref[h % 2, pl.ds(start, kc), :] = st
        return jnp.maximum(mx, jnp.max(st.reshape(kc // 8, 8, tq), axis=0))

    def value_step(j, acc, h, m):
        start = pl.multiple_of(j * kc, kc)
        p = jnp.exp2(s_ref[h % 2, pl.ds(start, kc), :] - m).astype(BF16)
        vt = jnp.concatenate([vt_ref[j, h * MLA_V:(h + 1) * MLA_V, :], ones], axis=0)
        return acc + jnp.dot(vt, p, preferred_element_type=F32)

    m_prev = None
    o_prev = None
    for stage in range(MLA_HEADS + 1):
        def body(j, carry, stage=stage, m_prev=m_prev):
            mx, acc = carry
            if stage < MLA_HEADS:
                mx = score_step(j, mx, stage)
            if stage > 0:
                acc = value_step(j, acc, stage - 1, m_prev)
            return mx, acc

        init = (jnp.full((8, tq), NEG_INF, F32), jnp.zeros((acc_rows, tq), F32))
        mx, acc = lax.fori_loop(0, n_chunks, body, init, unroll=MLA_UNROLL)
        if stage > 0:
            h = stage - 1
            o_t = acc[:MLA_V] / acc[MLA_V:MLA_V + 1]
            if h % 2 == 1:
                pair = jnp.concatenate([o_prev, o_t], axis=0)
                o_ref[:, (h - 1) * MLA_V:(h + 1) * MLA_V] = pair.T.astype(BF16)
            o_prev = o_t
        m_prev = jnp.max(mx, axis=0, keepdims=True)


def _mla_attn(q, k, vt):
    b, s, hp = q.shape
    n_chunks, kc = vt.shape[1], vt.shape[3]
    tq = min(MLA_TQ, s)
    return pl.pallas_call(
        _mla_attn_kernel,
        grid=(b, s // tq),
        in_specs=[pl.BlockSpec((None, tq, hp), lambda bi, i: (bi, i, 0)),
                  pl.BlockSpec((None, s, hp), lambda bi, i: (bi, 0, 0), pipeline_mode=pl.Buffered(1)),
                  pl.BlockSpec((None, n_chunks, MLA_WIDTH, kc), lambda bi, i: (bi, 0, 0, 0),
                               pipeline_mode=pl.Buffered(1))],
        out_specs=pl.BlockSpec((None, tq, MLA_WIDTH), lambda bi, i: (bi, i, 0)),
        out_shape=jax.ShapeDtypeStruct((b, s, MLA_WIDTH), BF16),
        scratch_shapes=[pltpu.VMEM((2, s, tq), F32)],
        compiler_params=_cparams(2),
        name="mla_attn",
    )(q, k, vt)


def _softmax_pv(s, v_pair):
    m = jnp.max(s, axis=-1, keepdims=True)
    p = jnp.exp(s - m)
    l = jnp.sum(p, axis=-1, keepdims=True)
    o = jnp.dot(p.astype(BF16), v_pair, preferred_element_type=F32) / l
    return o, m + jnp.log(l)


def _dil_attn_kernel(qkv_ref, o_ref, lse_ref, *, tq, win):
    length = qkv_ref.shape[0]
    q0 = pl.multiple_of(pl.program_id(2) * tq, tq)
    start = pl.multiple_of(jnp.clip(q0 - DIL_BAND, 0, length - win), DIL_BAND)
    qpos = q0 + lax.broadcasted_iota(jnp.int32, (tq, win), 0)
    kpos = start + lax.broadcasted_iota(jnp.int32, (tq, win), 1)
    valid = jnp.abs(kpos - qpos) <= DIL_BAND
    lane = lax.broadcasted_iota(jnp.int32, (tq, LANES), 1)
    low_half = lane < DIL_HD
    head_lanes = (low_half, lane >= DIL_HD)
    for p in range(DIL_WIDTH // LANES):
        q = qkv_ref[pl.ds(q0, tq), p * LANES:(p + 1) * LANES]
        k = qkv_ref[pl.ds(start, win), DIL_WIDTH + p * LANES:DIL_WIDTH + (p + 1) * LANES]
        v = qkv_ref[pl.ds(start, win), 2 * DIL_WIDTH + p * LANES:2 * DIL_WIDTH + (p + 1) * LANES]
        res = []
        for hh in range(2):
            qm = jnp.where(head_lanes[hh], q, jnp.zeros_like(q))
            s = jnp.where(valid, _nt_dot(qm, k), NEG_INF)
            res.append(_softmax_pv(s, v))
        o_ref[:, p * LANES:(p + 1) * LANES] = jnp.where(low_half, res[0][0], res[1][0]).astype(BF16)
        lse_ref[:, p * LANES:(p + 1) * LANES] = jnp.where(low_half, res[0][1], res[1][1])


def _dil_attn(zb, d):
    b, _, length, _ = zb.shape
    tq = min(2 * DIL_BAND, length)
    win = min(4 * DIL_BAND, length)
    return pl.pallas_call(
        functools.partial(_dil_attn_kernel, tq=tq, win=win),
        grid=(b, d, length // tq),
        in_specs=[pl.BlockSpec((None, None, length, ZB_GROUP), lambda bi, r, i: (bi, r, 0, 0))],
        out_specs=[pl.BlockSpec((None, None, tq, DIL_WIDTH), lambda bi, r, i: (bi, r, i, 0)),
                   pl.BlockSpec((None, None, tq, DIL_WIDTH), lambda bi, r, i: (bi, r, i, 0))],
        out_shape=[jax.ShapeDtypeStruct((b, d, length, DIL_WIDTH), BF16),
                   jax.ShapeDtypeStruct((b, d, length, DIL_WIDTH), F32)],
        compiler_params=_cparams(3),
        name=f"dil_attn_d{d}",
    )(zb)


NA_QROWS = 4
NA_UROWS = NA_QROWS + NA_KH
NA_QTOK = NA_QROWS * GRID_W
NA_UTOK = NA_UROWS * GRID_W


def _proj_na_kernel(h_ref, wqk_ref, wvt_ref, q_ref, k_ref, vt_ref):
    h = h_ref[...]
    z = jnp.dot(h, wqk_ref[...], preferred_element_type=F32)
    q_ref[...] = (z[:, :NA_WIDTH] * LOG2_E).astype(BF16)
    k_ref[...] = z[:, NA_WIDTH:].astype(BF16)
    vt = _nt_dot(wvt_ref[...], h)
    for c in range(vt_ref.shape[0]):
        vt_ref[c] = vt[:, c * NA_QTOK:(c + 1) * NA_QTOK].astype(BF16)


def _proj_na(h, wqk, wvt):
    b, s, _ = h.shape
    tm = TM_IN
    tok = pl.BlockSpec((None, tm, NA_WIDTH), lambda bi, i: (bi, i, 0))
    return pl.pallas_call(
        _proj_na_kernel,
        grid=(b, s // tm),
        in_specs=[pl.BlockSpec((None, tm, D_MODEL), lambda bi, i: (bi, i, 0)),
                  pl.BlockSpec((D_MODEL, 2 * NA_WIDTH), lambda bi, i: (0, 0)),
                  pl.BlockSpec((NA_WIDTH, D_MODEL), lambda bi, i: (0, 0))],
        out_specs=[tok, tok, pl.BlockSpec((None, tm // NA_QTOK, NA_WIDTH, NA_QTOK), lambda bi, i: (bi, i, 0, 0))],
        out_shape=[jax.ShapeDtypeStruct((b, s, NA_WIDTH), BF16), jax.ShapeDtypeStruct((b, s, NA_WIDTH), BF16),
                   jax.ShapeDtypeStruct((b, s // NA_QTOK, NA_WIDTH, NA_QTOK), BF16)],
        compiler_params=_cparams(2),
        name="proj_na",
    )(h, wqk, wvt)


NA_VARIANTS = ((0, lambda i: 0), (-NA_KH // 2, lambda i: i), (-NA_KH, lambda i: NA_KH // 2))


def _na_bias_kernel(rpb_ref, t_ref):
    h = pl.program_id(0)
    shape = (GRID_W, LANES)
    w = lax.broadcasted_iota(jnp.int32, shape, 0)
    lane = lax.broadcasted_iota(jnp.int32, shape, 1)
    c = lane & (GRID_W - 1)
    first = lane < GRID_W
    cs = jnp.clip(c - NA_KW // 2, 0, GRID_W - NA_KW)
    inside = (w >= cs) & (w < cs + NA_KW)
    off = w - c + NA_KW - 1
    neg = jnp.full(shape, NEG_INF, F32)
    for v, (delta, lo) in enumerate(NA_VARIANTS):
        for jk in range(NA_UROWS):
            for ip in range(NA_QROWS // 2):
                i0, i1 = 2 * ip, 2 * ip + 1
                ok0 = lo(i0) <= jk < lo(i0) + NA_KH
                ok1 = lo(i1) <= jk < lo(i1) + NA_KH
                dst = (v, slice(jk * GRID_W, (jk + 1) * GRID_W), slice(ip * LANES, (ip + 1) * LANES))
                if not (ok0 or ok1):
                    t_ref[dst] = neg
                    continue
                ro0 = jk - i0 + NA_KH - 1 + delta

                def body(kk, acc, ok0=ok0, ok1=ok1, ro0=ro0):
                    a = rpb_ref[h, ro0, kk] if ok0 else NEG_INF
                    b = rpb_ref[h, ro0 - 1, kk] if ok1 else NEG_INF
                    return jnp.where(off == kk, jnp.where(first, a, b), acc)

                acc = lax.fori_loop(0, 2 * NA_KW - 1, body, neg)
                t_ref[dst] = jnp.where(inside, acc * LOG2_E, NEG_INF)


def _na_bias(rpb_l):
    nv = len(NA_VARIANTS)
    return pl.pallas_call(
        _na_bias_kernel,
        grid=(NA_HEADS,),
        in_specs=[pl.BlockSpec(memory_space=pltpu.SMEM)],
        out_specs=pl.BlockSpec((nv, None, NA_UTOK, NA_QTOK), lambda h: (0, h, 0, 0)),
        out_shape=jax.ShapeDtypeStruct((nv, NA_HEADS, NA_UTOK, NA_QTOK), F32),
        compiler_params=_cparams(1),
        name="na_bias",
    )(rpb_l)


def _na_attn_kernel(q_ref, k_ref, vt_ref, t_ref, o_ref, s_ref, *, nblk):
    c0 = jnp.clip(pl.program_id(1) - 1, 0, nblk - NA_UROWS // NA_QROWS)
    k0 = pl.multiple_of(c0 * NA_QTOK, NA_QTOK)
    lane = lax.broadcasted_iota(jnp.int32, (NA_QTOK, LANES), 1)
    head_lanes = (lane < NA_HD, lane >= NA_HD)
    ones = jnp.ones((MLA_ONES_ROWS, NA_UTOK), BF16)

    def scores(h):
        cols = slice((h // 2) * LANES, (h // 2 + 1) * LANES)
        q = q_ref[:, cols]
        qm = jnp.where(head_lanes[h % 2], q, jnp.zeros_like(q))
        st = _nt_dot(k_ref[pl.ds(k0, NA_UTOK), cols], qm) + t_ref[h]
        s_ref[h % 2] = st
        return jnp.max(jnp.max(st.reshape(NA_UTOK // 8, 8, NA_QTOK), axis=0), axis=0, keepdims=True)

    def values(h, m):
        pt = jnp.exp2(s_ref[h % 2] - m).astype(BF16)
        vt = jnp.concatenate([vt_ref[c0 + c, h * NA_HD:(h + 1) * NA_HD, :] for c in range(NA_UROWS // NA_QROWS)],
                             axis=1)
        acc = jnp.dot(jnp.concatenate([vt, ones], axis=0), pt, preferred_element_type=F32)
        return acc[:NA_HD] / acc[NA_HD:NA_HD + 1]

    m_next = scores(0)
    o_prev = None
    for h in range(NA_HEADS):
        m = m_next
        if h + 1 < NA_HEADS:
            m_next = scores(h + 1)
        o_t = values(h, m)
        if h % 2 == 1:
            o_ref[:, (h - 1) * NA_HD:(h + 1) * NA_HD] = jnp.concatenate([o_prev, o_t], axis=0).T.astype(BF16)
        o_prev = o_t


def _na_attn(q, k, vt, table):
    b, s, _ = q.shape
    nblk = s // NA_QTOK
    last = nblk - 1
    variant = lambda bi, a: (jnp.where(a == 0, 0, jnp.where(a == last, 2, 1)), 0, 0, 0)
    return pl.pallas_call(
        functools.partial(_na_attn_kernel, nblk=nblk),
        grid=(b, nblk),
        in_specs=[pl.BlockSpec((None, NA_QTOK, NA_WIDTH), lambda bi, a: (bi, a, 0)),
                  pl.BlockSpec((None, s, NA_WIDTH), lambda bi, a: (bi, 0, 0), pipeline_mode=pl.Buffered(1)),
                  pl.BlockSpec((None, nblk, NA_WIDTH, NA_QTOK), lambda bi, a: (bi, 0, 0, 0),
                               pipeline_mode=pl.Buffered(1)),
                  pl.BlockSpec((None, NA_HEADS, NA_UTOK, NA_QTOK), variant)],
        out_specs=pl.BlockSpec((None, NA_QTOK, NA_WIDTH), lambda bi, a: (bi, a, 0)),
        out_shape=jax.ShapeDtypeStruct((b, s, NA_WIDTH), BF16),
        scratch_shapes=[pltpu.VMEM((2, NA_UTOK, NA_QTOK), F32)],
        compiler_params=_cparams(2),
        name="na_attn",
    )(q, k, vt, table)


def _out_kernel(*refs, emit_h):
    (x_ref, oa_ref, gates_ref, oc_ref, merge_ref,
     ob0_ref, ob1_ref, ob2_ref, ls0_ref, ls1_ref, ls2_ref,
     wpa_ref, wpb_ref, wpc_ref, wout_ref, gpost_ref, mod_ref) = refs[:17]
    rest = refs[17:]
    if emit_h:
        gnext_ref, modn_ref, y_ref, h_ref, so_ref, sl_ref = rest
    else:
        y_ref, so_ref, sl_ref = rest
    tm = x_ref.shape[0]

    for gi, (ob_ref, ls_ref, (_, d)) in enumerate(zip((ob0_ref, ob1_ref, ob2_ref), (ls0_ref, ls1_ref, ls2_ref),
                                                     DIL_PAIRS)):
        for r in range(d):
            rows = slice(None) if d == 1 else pl.ds(r, tm // d, stride=d)
            for c in range(DIL_WIDTH // LANES):
                cols = slice(c * LANES, (c + 1) * LANES)
                so_ref[gi, c, rows, :] = ob_ref[r, :, cols].astype(F32)
                sl_ref[gi, c, rows, :] = ls_ref[r, :, cols]
    slabs = range(DIL_WIDTH // LANES)
    lse = [jnp.concatenate([sl_ref[gi, c] for c in slabs], axis=1) for gi in range(3)]
    o_g = [jnp.concatenate([so_ref[gi, c] for c in slabs], axis=1) for gi in range(3)]
    mx = jnp.maximum(jnp.maximum(lse[0], lse[1]), lse[2])
    e = [jnp.exp(x - mx) for x in lse]
    o_b = (e[0] * o_g[0] + e[1] * o_g[1] + e[2] * o_g[2]) / (e[0] + e[1] + e[2])

    def gated(o, lo, hi):
        g = gates_ref[:, lo:hi].astype(F32)
        return (o * (g * _sigmoid(g))).astype(BF16)

    a = gated(oa_ref[...].astype(F32), 0, MLA_WIDTH)
    bb = gated(o_b, MLA_WIDTH, MLA_WIDTH + DIL_WIDTH)
    c = gated(oc_ref[...].astype(F32), MLA_WIDTH + DIL_WIDTH, GATE_WIDTH)
    mixed = (_sigmoid(merge_ref[:, 0:D_MODEL].astype(F32))
             * jnp.dot(a, wpa_ref[...], preferred_element_type=F32))
    mixed += (_sigmoid(merge_ref[:, D_MODEL:2 * D_MODEL].astype(F32))
              * jnp.dot(bb, wpb_ref[...], preferred_element_type=F32))
    mixed += (_sigmoid(merge_ref[:, 2 * D_MODEL:3 * D_MODEL].astype(F32))
              * jnp.dot(c, wpc_ref[...], preferred_element_type=F32))
    out = jnp.dot(mixed.astype(BF16), wout_ref[...], preferred_element_type=F32)
    normed = out * lax.rsqrt(jnp.mean(out * out, axis=-1, keepdims=True) + NORM_EPS) * gpost_ref[...]
    y = x_ref[...] + mod_ref[2:3, :] * normed
    y_ref[...] = y
    if emit_h:
        h_ref[...] = _modulated_norm(y, gnext_ref[...], modn_ref[...]).astype(BF16)


def _out_layer(x, oa, gates, oc, merge, obs, lses, wpa, wpb, wpc, wout, g_post, mod, g_next, mod_next):
    b, s, _ = x.shape
    tm = TM_OUT
    emit_h = g_next is not None
    tok = lambda n: pl.BlockSpec((None, tm, n), lambda bi, i: (bi, i, 0))
    const = lambda shape: pl.BlockSpec(shape, lambda bi, i: (0,) * len(shape))
    modspec = pl.BlockSpec((None, 3, D_MODEL), lambda bi, i: (bi, 0, 0))
    cls = [pl.BlockSpec((None, d, tm // d, DIL_WIDTH), lambda bi, i: (bi, 0, i, 0)) for _, d in DIL_PAIRS]
    in_specs = [tok(D_MODEL), tok(MLA_WIDTH), tok(GATE_WIDTH), tok(NA_WIDTH), tok(3 * D_MODEL),
                *cls, *cls,
                const((MLA_WIDTH, D_MODEL)), const((DIL_WIDTH, D_MODEL)), const((NA_WIDTH, D_MODEL)),
                const((D_MODEL, D_MODEL)), const((1, D_MODEL)), modspec]
    args = [x, oa, gates, oc, merge, *obs, *lses, wpa, wpb, wpc, wout, g_post.reshape(1, D_MODEL), mod]
    out_specs = [tok(D_MODEL)]
    out_shape = [jax.ShapeDtypeStruct((b, s, D_MODEL), F32)]
    if emit_h:
        in_specs += [const((1, D_MODEL)), modspec]
        args += [g_next.reshape(1, D_MODEL), mod_next]
        out_specs.append(tok(D_MODEL))
        out_shape.append(jax.ShapeDtypeStruct((b, s, D_MODEL), BF16))
    res = pl.pallas_call(
        functools.partial(_out_kernel, emit_h=emit_h),
        grid=(b, s // tm),
        in_specs=in_specs,
        out_specs=out_specs,
        out_shape=out_shape,
        scratch_shapes=[pltpu.VMEM((3, DIL_WIDTH // LANES, tm, LANES), F32)] * 2,
        compiler_params=_cparams(2),
        name="out_layer",
    )(*args)
    return (res[0], res[1]) if emit_h else (res[0], None)


def _rope_tables(s, head_dim, lane_of_dim0, period, scale):
    half = head_dim // 2
    inv = ROPE_THETA ** (-jnp.arange(half, dtype=F32) * 2.0 / head_dim)
    ang = jnp.arange(s, dtype=F32)[:, None] * inv[None, :]
    cos, sin = jnp.cos(ang), jnp.sin(ang)
    lane = np.arange(LANES)
    dim = (lane - lane_of_dim0) % period
    in_rope = dim < head_dim
    idx = dim % half
    cos_t = jnp.where(in_rope[None, :], cos[:, idx], 1.0) * scale
    sin_t = sin[:, idx] * scale
    sin_lo = jnp.where((in_rope & (dim < half))[None, :], -sin_t, 0.0)
    sin_hi = jnp.where((in_rope & (dim >= half))[None, :], sin_t, 0.0)
    return cos_t.astype(F32), sin_lo.astype(F32), sin_hi.astype(F32)


def _layout_weights(w_in, w_uq, w_ukv):
    c = [0] + [int(v) for v in _CUTS]
    cq, ckv, kr, gate_a, qkv_b, gate_b, qkv_c, gate_c, merge = [w_in[:, :, c[i]:c[i + 1]] for i in range(9)]
    zeros = lambda n: jnp.zeros((DEPTH, D_MODEL, n), w_in.dtype)
    w_a = jnp.concatenate([cq, ckv, zeros(MLA_NOPE), kr, zeros(LANES - MLA_NOPE - MLA_ROPE)], axis=-1)
    w_g = jnp.concatenate([gate_a, gate_b, gate_c], axis=-1)
    qkv_b = qkv_b.reshape(DEPTH, D_MODEL, 3, len(DIL_PAIRS), DIL_WIDTH)
    qkv_b = qkv_b * jnp.array([DIL_HD ** -0.5, 1.0, 1.0], w_in.dtype)[None, None, :, None, None]
    w_b = qkv_b.transpose(0, 1, 3, 2, 4).reshape(DEPTH, D_MODEL, ZB_WIDTH)
    qkv_c = qkv_c.reshape(DEPTH, D_MODEL, 3, NA_WIDTH)
    qkv_c = qkv_c * jnp.array([NA_HD ** -0.5, 1.0, 1.0], w_in.dtype)[None, None, :, None]
    w_cqk = qkv_c[:, :, :2].reshape(DEPTH, D_MODEL, 2 * NA_WIDTH)
    w_cvt = qkv_c[:, :, 2].transpose(0, 2, 1)
    uq = w_uq.reshape(DEPTH, MLA_Q_RANK, MLA_HEADS, MLA_DQK)
    uq = jnp.pad(uq, ((0, 0), (0, 0), (0, 0), (0, MLA_HEAD_PAD - MLA_DQK)))
    w_q = uq.reshape(DEPTH, MLA_Q_RANK, MLA_HEADS * MLA_HEAD_PAD)
    ukv = w_ukv.reshape(DEPTH, MLA_KV_RANK, MLA_HEADS, MLA_NOPE + MLA_V)
    uk = jnp.pad(ukv[..., :MLA_NOPE], ((0, 0), (0, 0), (0, 0), (0, MLA_HEAD_PAD - MLA_NOPE)))
    w_k = uk.reshape(DEPTH, MLA_KV_RANK, MLA_HEADS * MLA_HEAD_PAD)
    w_vt = ukv[..., MLA_NOPE:].reshape(DEPTH, MLA_KV_RANK, MLA_WIDTH).transpose(0, 2, 1)
    bf = lambda w: w.astype(BF16)
    return dict(a=bf(w_a), g=bf(w_g), b=bf(w_b), cqk=bf(w_cqk), cvt=bf(w_cvt), m=bf(merge),
                q=bf(w_q), k=bf(w_k), vt=bf(w_vt))


def _trunk(x, c, p):
    b, s, _ = x.shape
    mods = _ada(c, p["w_ada"], p["b_ada"]).reshape(DEPTH, b, 3, D_MODEL)
    dil_tabs = _rope_tables(s, DIL_HD, 0, DIL_HD, 1.0)
    q_tabs = _rope_tables(s, MLA_ROPE, MLA_NOPE, LANES, MLA_DQK ** -0.5 * LOG2_E)
    k_tabs = _rope_tables(s, MLA_ROPE, MLA_NOPE, LANES, 1.0)
    w = p["w"]
    h = _prenorm(x, p["g_pre"][0], mods[0])
    for l in range(DEPTH):
        za = _proj(h, w["a"][l], "proj_mla")
        gates = _proj(h, w["g"][l], "proj_gates")
        qc, kc, vtc = _proj_na(h, w["cqk"][l], w["cvt"][l])
        merge = _proj(h, w["m"][l], "proj_merge")
        zbs = _proj_dil(h, w["b"][l], dil_tabs)
        q, k, vt = _mla_prep(za, p["g_q"][l], p["g_kv"][l], w["q"][l], w["k"][l], w["vt"][l], q_tabs, k_tabs)
        oa = _mla_attn(q, k, vt)
        dil = [_dil_attn(zb, d) for zb, (_, d) in zip(zbs, DIL_PAIRS)]
        oc = _na_attn(qc, kc, vtc, p["na_bias"][l])
        last = l == DEPTH - 1
        x, h = _out_layer(x, oa, gates, oc, merge, [o for o, _ in dil], [ls for _, ls in dil],
                          p["w_pa"][l], p["w_pb"][l], p["w_pc"][l], p["w_out"][l], p["g_post"][l], mods[l],
                          None if last else p["g_pre"][l + 1], None if last else mods[l + 1])
    return x


def kernel(x_prompt, x_sample, c_prompt, c_sample, w_ada, b_ada, g_pre, g_post, w_in, g_q, w_uq, g_kv, w_ukv, rpb,
           w_pa, w_pb, w_pc, w_out):
    p = dict(w_ada=w_ada.astype(BF16), b_ada=b_ada, g_pre=g_pre, g_post=g_post, g_q=g_q, g_kv=g_kv,
             w=_layout_weights(w_in, w_uq, w_ukv),
             na_bias=[_na_bias(rpb[l]) for l in range(DEPTH)],
             w_pa=w_pa.astype(BF16), w_pb=w_pb.astype(BF16), w_pc=w_pc.astype(BF16), w_out=w_out.astype(BF16))
    return (_trunk(x_prompt, c_prompt, p), _trunk(x_sample, c_sample, p))
```

```python
import functools

import jax
import jax.numpy as jnp
import numpy as np
from jax import lax
from jax.experimental import pallas as pl
from jax.experimental.pallas import tpu as pltpu

F32 = jnp.float32
BF16 = jnp.bfloat16

D_MODEL = 1024
DEPTH = 4
GRID_W = 64
ROPE_THETA = 10000.0
NORM_EPS = 1e-6
NEG_INF = -1e30
LOG2_E = float(np.log2(np.e))

MLA_HEADS = 8
MLA_NOPE = 64
MLA_ROPE = 32
MLA_V = 64
MLA_DQK = MLA_NOPE + MLA_ROPE
MLA_Q_RANK = 384
MLA_KV_RANK = 256
MLA_WIDTH = MLA_HEADS * MLA_V
MLA_HEAD_PAD = 128

DIL_PAIRS = ((128, 1), (512, 4), (2048, 16))
DIL_HPG = 4
DIL_HD = 64
DIL_HEADS = 12
DIL_WIDTH = DIL_HPG * DIL_HD
DIL_BAND = 64

NA_HEADS = 8
NA_HD = 64
NA_KH = 8
NA_KW = 16
NA_WIDTH = NA_HEADS * NA_HD

LANES = 128
VMEM_LIMIT = 56 * 1024 * 1024

_CUTS = np.cumsum((MLA_Q_RANK, MLA_KV_RANK, MLA_ROPE, MLA_WIDTH, 3 * DIL_HEADS * DIL_HD, DIL_WIDTH,
                   3 * NA_HEADS * NA_HD, NA_WIDTH, 3 * D_MODEL))
ZA_WIDTH = MLA_Q_RANK + MLA_KV_RANK + LANES
GATE_WIDTH = MLA_WIDTH + DIL_WIDTH + NA_WIDTH
ZB_WIDTH = 3 * DIL_HEADS * DIL_HD
ZB_GROUP = 3 * DIL_WIDTH
ZC_WIDTH = 3 * NA_WIDTH

TM_IN = 512
TM_OUT = 256
MLA_TQ = 512
MLA_KC = 512
MLA_UNROLL = 4


def _cparams(n_grid):
    return pltpu.CompilerParams(dimension_semantics=("arbitrary",) * n_grid, vmem_limit_bytes=VMEM_LIMIT)


def _sigmoid(x):
    return 0.5 * (1.0 + jnp.tanh(0.5 * x))


def _rope_lanes(x, cos, sin_lo, sin_hi, half):
    return x * cos + pltpu.roll(x, half, 1) * sin_hi + pltpu.roll(x, LANES - half, 1) * sin_lo


def _nt_dot(a, b):
    return lax.dot_general(a, b, (((1,), (1,)), ((), ())), preferred_element_type=F32)


def _ada_kernel(c_ref, w_ref, b_ref, o_ref):
    c = c_ref[...]
    c_act = (c * _sigmoid(c)).astype(BF16)
    o_ref[...] = jnp.dot(c_act, w_ref[...], preferred_element_type=F32) + b_ref[...]


def _ada(c, w_ada_bf, b_ada):
    b = c.shape[0]
    return pl.pallas_call(
        _ada_kernel,
        grid=(DEPTH,),
        in_specs=[pl.BlockSpec((b, D_MODEL), lambda l: (0, 0)),
                  pl.BlockSpec((None, D_MODEL, 3 * D_MODEL), lambda l: (l, 0, 0)),
                  pl.BlockSpec((None, 1, 3 * D_MODEL), lambda l: (l, 0, 0))],
        out_specs=pl.BlockSpec((None, b, 3 * D_MODEL), lambda l: (l, 0, 0)),
        out_shape=jax.ShapeDtypeStruct((DEPTH, b, 3 * D_MODEL), F32),
        compiler_params=_cparams(1),
        name="ada",
    )(c, w_ada_bf, b_ada.reshape(DEPTH, 1, 3 * D_MODEL))


def _modulated_norm(x32, g, mod):
    y = x32 * lax.rsqrt(jnp.mean(x32 * x32, axis=-1, keepdims=True) + NORM_EPS) * g
    return y * (1.0 + mod[1:2, :]) + mod[0:1, :]


def _prenorm_kernel(x_ref, g_ref, mod_ref, h_ref):
    h_ref[...] = _modulated_norm(x_ref[...], g_ref[...], mod_ref[...]).astype(BF16)


def _prenorm(x, g, mod):
    b, s, _ = x.shape
    tm = TM_IN
    return pl.pallas_call(
        _prenorm_kernel,
        grid=(b, s // tm),
        in_specs=[pl.BlockSpec((None, tm, D_MODEL), lambda bi, i: (bi, i, 0)),
                  pl.BlockSpec((1, D_MODEL), lambda bi, i: (0, 0)),
                  pl.BlockSpec((None, 3, D_MODEL), lambda bi, i: (bi, 0, 0))],
        out_specs=pl.BlockSpec((None, tm, D_MODEL), lambda bi, i: (bi, i, 0)),
        out_shape=jax.ShapeDtypeStruct((b, s, D_MODEL), BF16),
        compiler_params=_cparams(2),
        name="prenorm",
    )(x, g.reshape(1, D_MODEL), mod)


def _proj_kernel(h_ref, w_ref, o_ref):
    o_ref[...] = jnp.dot(h_ref[...], w_ref[...], preferred_element_type=F32).astype(BF16)


def _proj(h, w, name):
    b, s, _ = h.shape
    n = w.shape[1]
    tm = TM_IN
    return pl.pallas_call(
        _proj_kernel,
        grid=(b, s // tm),
        in_specs=[pl.BlockSpec((None, tm, D_MODEL), lambda bi, i: (bi, i, 0)),
                  pl.BlockSpec((D_MODEL, n), lambda bi, i: (0, 0))],
        out_specs=pl.BlockSpec((None, tm, n), lambda bi, i: (bi, i, 0)),
        out_shape=jax.ShapeDtypeStruct((b, s, n), BF16),
        compiler_params=_cparams(2),
        name=name,
    )(h, w)


def _proj_dil_kernel(h_ref, w_ref, cos_ref, slo_ref, shi_ref, o0_ref, o1_ref, o2_ref, z_ref):
    tm = h_ref.shape[0]
    z = jnp.dot(h_ref[...], w_ref[...], preferred_element_type=F32)
    cos, slo, shi = cos_ref[...], slo_ref[...], shi_ref[...]
    slabs_per_part = DIL_WIDTH // LANES
    for sl in range(ZB_WIDTH // LANES):
        x = z[:, sl * LANES:(sl + 1) * LANES]
        part = (sl // slabs_per_part) % 3
        if part != 2:
            x = _rope_lanes(x, cos, slo, shi, DIL_HD // 2)
        if part == 0:
            x = x * LOG2_E
        z_ref[sl] = x
    slabs_per_group = ZB_GROUP // LANES
    for gi, (o_ref, (_, d)) in enumerate(zip((o0_ref, o1_ref, o2_ref), DIL_PAIRS)):
        for r in range(d):
            rows = slice(None) if d == 1 else pl.ds(r, tm // d, stride=d)
            for c in range(slabs_per_group):
                o_ref[r, :, c * LANES:(c + 1) * LANES] = z_ref[gi * slabs_per_group + c, rows, :].astype(BF16)


def _proj_dil(h, w, tabs):
    b, s, _ = h.shape
    tm = TM_IN
    cos, slo, shi = tabs
    tab_spec = pl.BlockSpec((tm, LANES), lambda i, bi: (i, 0))
    out_shapes, out_specs = [], []
    for _, d in DIL_PAIRS:
        out_shapes.append(jax.ShapeDtypeStruct((b, d, s // d, ZB_GROUP), BF16))
        out_specs.append(pl.BlockSpec((None, d, tm // d, ZB_GROUP), lambda i, bi: (bi, 0, i, 0)))
    return pl.pallas_call(
        _proj_dil_kernel,
        grid=(s // tm, b),
        in_specs=[pl.BlockSpec((None, tm, D_MODEL), lambda i, bi: (bi, i, 0)),
                  pl.BlockSpec((D_MODEL, ZB_WIDTH), lambda i, bi: (0, 0)),
                  tab_spec, tab_spec, tab_spec],
        out_specs=out_specs,
        out_shape=out_shapes,
        scratch_shapes=[pltpu.VMEM((ZB_WIDTH // LANES, tm, LANES), F32)],
        compiler_params=_cparams(2),
        name="proj_dil",
    )(h, w, cos, slo, shi)


def _mla_prep_kernel(za_ref, gq_ref, gkv_ref, wq_ref, wk_ref, wvt_ref,
                     cq_ref, sloq_ref, shiq_ref, ck_ref, slok_ref, shik_ref,
                     q_ref, k_ref, vt_ref):
    def norm(x, g):
        x32 = x.astype(F32)
        return (x32 * lax.rsqrt(jnp.mean(x32 * x32, axis=-1, keepdims=True) + NORM_EPS) * g).astype(BF16)

    cqn = norm(za_ref[:, 0:MLA_Q_RANK], gq_ref[...])
    ckvn = norm(za_ref[:, MLA_Q_RANK:MLA_Q_RANK + MLA_KV_RANK], gkv_ref[...])
    kr = za_ref[:, MLA_Q_RANK + MLA_KV_RANK:ZA_WIDTH].astype(F32)
    kr = _rope_lanes(kr, ck_ref[...], slok_ref[...], shik_ref[...], MLA_ROPE // 2)
    q = jnp.dot(cqn, wq_ref[...], preferred_element_type=F32)
    k = jnp.dot(ckvn, wk_ref[...], preferred_element_type=F32)
    cq, sloq, shiq = cq_ref[...], sloq_ref[...], shiq_ref[...]
    for h in range(MLA_HEADS):
        cols = slice(h * MLA_HEAD_PAD, (h + 1) * MLA_HEAD_PAD)
        q_ref[:, cols] = _rope_lanes(q[:, cols], cq, sloq, shiq, MLA_ROPE // 2).astype(BF16)
        k_ref[:, cols] = (k[:, cols] + kr).astype(BF16)
    vt_ref[...] = _nt_dot(wvt_ref[...], ckvn).astype(BF16)


def _mla_prep(za, g_q, g_kv, wq, wk, wvt, qtabs, ktabs):
    b, s, _ = za.shape
    tm = MLA_KC
    hp = MLA_HEADS * MLA_HEAD_PAD
    tab_spec = pl.BlockSpec((tm, LANES), lambda i, bi: (i, 0))
    const = lambda shape: pl.BlockSpec(shape, lambda i, bi: (0,) * len(shape))
    return pl.pallas_call(
        _mla_prep_kernel,
        grid=(s // tm, b),
        in_specs=[pl.BlockSpec((None, tm, ZA_WIDTH), lambda i, bi: (bi, i, 0)),
                  const((1, MLA_Q_RANK)), const((1, MLA_KV_RANK)),
                  const((MLA_Q_RANK, hp)), const((MLA_KV_RANK, hp)), const((MLA_WIDTH, MLA_KV_RANK)),
                  tab_spec, tab_spec, tab_spec, tab_spec, tab_spec, tab_spec],
        out_specs=[pl.BlockSpec((None, tm, hp), lambda i, bi: (bi, i, 0)),
                   pl.BlockSpec((None, tm, hp), lambda i, bi: (bi, i, 0)),
                   pl.BlockSpec((None, None, MLA_WIDTH, tm), lambda i, bi: (bi, i, 0, 0))],
        out_shape=[jax.ShapeDtypeStruct((b, s, hp), BF16),
                   jax.ShapeDtypeStruct((b, s, hp), BF16),
                   jax.ShapeDtypeStruct((b, s // tm, MLA_WIDTH, tm), BF16)],
        compiler_params=_cparams(2),
        name="mla_prep",
    )(za, g_q.reshape(1, -1), g_kv.reshape(1, -1), wq, wk, wvt, *qtabs, *ktabs)


MLA_ONES_ROWS = 16


def _mla_attn_kernel(q_ref, k_ref, vt_ref, o_ref, s_ref):
    tq = q_ref.shape[0]
    n_chunks, _, kc = vt_ref.shape
    ones = jnp.ones((MLA_ONES_ROWS, kc), BF16)
    acc_rows = MLA_V + MLA_ONES_ROWS

    def score_step(j, mx, h):
        start = pl.multiple_of(j * kc, kc)
        cols = slice(h * MLA_HEAD_PAD, (h + 1) * MLA_HEAD_PAD)
        st = _nt_dot(k_ref[pl.ds(start, kc), cols], q_ref[:, cols])
        s_ref[h % 2, pl.ds(start, kc), :] = st
        return jnp.maximum(mx, jnp.max(st.reshape(kc // 8, 8, tq), axis=0))

    def value_step(j, acc, h, m):
        start = pl.multiple_of(j * kc, kc)
        p = jnp.exp2(s_ref[h % 2, pl.ds(start, kc), :] - m).astype(BF16)
        vt = jnp.concatenate([vt_ref[j, h * MLA_V:(h + 1) * MLA_V, :], ones], axis=0)
        return acc + jnp.dot(vt, p, preferred_element_type=F32)

    m_prev = None
    o_prev = None
    for stage in range(MLA_HEADS + 1):
        def body(j, carry, stage=stage, m_prev=m_prev):
            mx, acc = carry
            if stage < MLA_HEADS:
                mx = score_step(j, mx, stage)
            if stage > 0:
                acc = value_step(j, acc, stage - 1, m_prev)
            return mx, acc

        init = (jnp.full((8, tq), NEG_INF, F32), jnp.zeros((acc_rows, tq), F32))
        mx, acc = lax.fori_loop(0, n_chunks, body, init, unroll=MLA_UNROLL)
        if stage > 0:
            h = stage - 1
            o_t = acc[:MLA_V] / acc[MLA_V:MLA_V + 1]
            if h % 2 == 1:
                pair = jnp.concatenate([o_prev, o_t], axis=0)
                o_ref[:, (h - 1) * MLA_V:(h + 1) * MLA_V] = pair.T.astype(BF16)
            o_prev = o_t
        m_prev = jnp.max(mx, axis=0, keepdims=True)


def _mla_attn(q, k, vt):
    b, s, hp = q.shape
    n_chunks, kc = vt.shape[1], vt.shape[3]
    tq = min(MLA_TQ, s)
    return pl.pallas_call(
        _mla_attn_kernel,
        grid=(b, s // tq),
        in_specs=[pl.BlockSpec((None, tq, hp), lambda bi, i: (bi, i, 0)),
                  pl.BlockSpec((None, s, hp), lambda bi, i: (bi, 0, 0), pipeline_mode=pl.Buffered(1)),
                  pl.BlockSpec((None, n_chunks, MLA_WIDTH, kc), lambda bi, i: (bi, 0, 0, 0),
                               pipeline_mode=pl.Buffered(1))],
        out_specs=pl.BlockSpec((None, tq, MLA_WIDTH), lambda bi, i: (bi, i, 0)),
        out_shape=jax.ShapeDtypeStruct((b, s, MLA_WIDTH), BF16),
        scratch_shapes=[pltpu.VMEM((2, s, tq), F32)],
        compiler_params=_cparams(2),
        name="mla_attn",
    )(q, k, vt)


def _dil_attn_kernel(qkv_ref, o_ref, lse_ref, s_ref, *, tq, win):
    d, length, _ = qkv_ref.shape
    tiles = length // tq
    n_total = d * tiles
    heads = range(DIL_HPG)
    key_rel = lax.broadcasted_iota(jnp.int32, (tq, win), 1) - lax.broadcasted_iota(jnp.int32, (tq, win), 0)
    lane = lax.broadcasted_iota(jnp.int32, (tq, LANES), 1)
    low_half = lane < DIL_HD
    head_lanes = (low_half, lane >= DIL_HD)

    def coords(n):
        r = n // tiles
        q0 = pl.multiple_of((n - r * tiles) * tq, tq)
        start = pl.multiple_of(jnp.clip(q0 - DIL_BAND, 0, length - win), DIL_BAND)
        return r, q0, start

    def score_step(n):
        r, q0, start = coords(n)
        valid = jnp.abs(key_rel + (start - q0)) <= DIL_BAND
        ms = []
        for h in heads:
            cols = slice((h // 2) * LANES, (h // 2 + 1) * LANES)
            q = qkv_ref[r, pl.ds(q0, tq), cols]
            k = qkv_ref[r, pl.ds(start, win), DIL_WIDTH + cols.start:DIL_WIDTH + cols.stop]
            qm = jnp.where(head_lanes[h % 2], q, jnp.zeros_like(q))
            sc = jnp.where(valid, _nt_dot(qm, k), NEG_INF)
            s_ref[n % 2, h] = sc
            ms.append(jnp.max(sc, axis=-1, keepdims=True))
        return tuple(ms)

    def value_step(n, ms):
        r, q0, start = coords(n)
        res = []
        for h in heads:
            cols = slice(2 * DIL_WIDTH + (h // 2) * LANES, 2 * DIL_WIDTH + (h // 2 + 1) * LANES)
            p = jnp.exp2(s_ref[n % 2, h] - ms[h])
            l = jnp.sum(p, axis=-1, keepdims=True)
            o = jnp.dot(p.astype(BF16), qkv_ref[r, pl.ds(start, win), cols], preferred_element_type=F32) / l
            res.append((o, ms[h] + jnp.log2(l)))
        for pr in range(DIL_HPG // 2):
            cols = slice(pr * LANES, (pr + 1) * LANES)
            o_ref[r, pl.ds(q0, tq), cols] = jnp.where(low_half, res[2 * pr][0], res[2 * pr + 1][0]).astype(BF16)
            lse_ref[r, pl.ds(q0, tq), cols] = jnp.where(low_half, res[2 * pr][1], res[2 * pr + 1][1])

    def body(n, ms):
        ms_next = score_step(jnp.minimum(n + 1, n_total - 1))
        value_step(n, ms)
        return ms_next

    lax.fori_loop(0, n_total, body, score_step(0), unroll=2)


def _dil_attn(zb, d):
    b, _, length, _ = zb.shape
    tq = min(2 * DIL_BAND, length)
    win = min(4 * DIL_BAND, length)
    whole = lambda width: pl.BlockSpec((None, d, length, width), lambda bi: (bi, 0, 0, 0))
    return pl.pallas_call(
        functools.partial(_dil_attn_kernel, tq=tq, win=win),
        grid=(b,),
        in_specs=[whole(ZB_GROUP)],
        out_specs=[whole(DIL_WIDTH), whole(DIL_WIDTH)],
        out_shape=[jax.ShapeDtypeStruct((b, d, length, DIL_WIDTH), BF16),
                   jax.ShapeDtypeStruct((b, d, length, DIL_WIDTH), F32)],
        scratch_shapes=[pltpu.VMEM((2, DIL_HPG, tq, win), F32)],
        compiler_params=_cparams(1),
        name=f"dil_attn_d{d}",
    )(zb)


NA_QROWS = 4
NA_UROWS = NA_QROWS + NA_KH
NA_QTOK = NA_QROWS * GRID_W
NA_UTOK = NA_UROWS * GRID_W


def _proj_na_kernel(h_ref, wqk_ref, wvt_ref, q_ref, k_ref, vt_ref):
    h = h_ref[...]
    z = jnp.dot(h, wqk_ref[...], preferred_element_type=F32)
    q_ref[...] = (z[:, :NA_WIDTH] * LOG2_E).astype(BF16)
    k_ref[...] = z[:, NA_WIDTH:].astype(BF16)
    vt = _nt_dot(wvt_ref[...], h)
    for c in range(vt_ref.shape[0]):
        vt_ref[c] = vt[:, c * NA_QTOK:(c + 1) * NA_QTOK].astype(BF16)


def _proj_na(h, wqk, wvt):
    b, s, _ = h.shape
    tm = TM_IN
    tok = pl.BlockSpec((None, tm, NA_WIDTH), lambda bi, i: (bi, i, 0))
    return pl.pallas_call(
        _proj_na_kernel,
        grid=(b, s // tm),
        in_specs=[pl.BlockSpec((None, tm, D_MODEL), lambda bi, i: (bi, i, 0)),
                  pl.BlockSpec((D_MODEL, 2 * NA_WIDTH), lambda bi, i: (0, 0)),
                  pl.BlockSpec((NA_WIDTH, D_MODEL), lambda bi, i: (0, 0))],
        out_specs=[tok, tok, pl.BlockSpec((None, tm // NA_QTOK, NA_WIDTH, NA_QTOK), lambda bi, i: (bi, i, 0, 0))],
        out_shape=[jax.ShapeDtypeStruct((b, s, NA_WIDTH), BF16), jax.ShapeDtypeStruct((b, s, NA_WIDTH), BF16),
                   jax.ShapeDtypeStruct((b, s // NA_QTOK, NA_WIDTH, NA_QTOK), BF16)],
        compiler_params=_cparams(2),
        name="proj_na",
    )(h, wqk, wvt)


NA_VARIANTS = ((0, lambda i: 0), (-NA_KH // 2, lambda i: i), (-NA_KH, lambda i: NA_KH // 2))


def _na_bias_kernel(rpb_ref, t_ref):
    h = pl.program_id(0)
    shape = (GRID_W, LANES)
    w = lax.broadcasted_iota(jnp.int32, shape, 0)
    lane = lax.broadcasted_iota(jnp.int32, shape, 1)
    c = lane & (GRID_W - 1)
    first = lane < GRID_W
    cs = jnp.clip(c - NA_KW // 2, 0, GRID_W - NA_KW)
    inside = (w >= cs) & (w < cs + NA_KW)
    off = w - c + NA_KW - 1
    neg = jnp.full(shape, NEG_INF, F32)
    tiles = {(None, None): neg}
    for v, (delta, lo) in enumerate(NA_VARIANTS):
        for jk in range(NA_UROWS):
            for ip in range(NA_QROWS // 2):
                ro = []
                for i in (2 * ip, 2 * ip + 1):
                    ok = lo(i) <= jk < lo(i) + NA_KH
                    ro.append(jk - i + NA_KH - 1 + delta if ok else None)
                ro = tuple(ro)
                if ro not in tiles:
                    def body(kk, acc, ro=ro):
                        a = NEG_INF if ro[0] is None else rpb_ref[h, ro[0], kk]
                        b = NEG_INF if ro[1] is None else rpb_ref[h, ro[1], kk]
                        return jnp.where(off == kk, jnp.where(first, a, b), acc)

                    acc = lax.fori_loop(0, 2 * NA_KW - 1, body, neg)
                    tiles[ro] = jnp.where(inside, acc * LOG2_E, NEG_INF)
                t_ref[v, jk * GRID_W:(jk + 1) * GRID_W, ip * LANES:(ip + 1) * LANES] = tiles[ro]


def _na_bias(rpb_l):
    nv = len(NA_VARIANTS)
    return pl.pallas_call(
        _na_bias_kernel,
        grid=(NA_HEADS,),
        in_specs=[pl.BlockSpec(memory_space=pltpu.SMEM)],
        out_specs=pl.BlockSpec((nv, None, NA_UTOK, NA_QTOK), lambda h: (0, h, 0, 0)),
        out_shape=jax.ShapeDtypeStruct((nv, NA_HEADS, NA_UTOK, NA_QTOK), F32),
        compiler_params=_cparams(1),
        name="na_bias",
    )(rpb_l)


def _na_attn_kernel(q_ref, k_ref, vt_ref, t_ref, o_ref, s_ref, *, nblk):
    c0 = jnp.clip(pl.program_id(1) - 1, 0, nblk - NA_UROWS // NA_QROWS)
    k0 = pl.multiple_of(c0 * NA_QTOK, NA_QTOK)
    lane = lax.broadcasted_iota(jnp.int32, (NA_QTOK, LANES), 1)
    head_lanes = (lane < NA_HD, lane >= NA_HD)
    ones = jnp.ones((MLA_ONES_ROWS, NA_UTOK), BF16)

    def scores(h):
        cols = slice((h // 2) * LANES, (h // 2 + 1) * LANES)
        q = q_ref[:, cols]
        qm = jnp.where(head_lanes[h % 2], q, jnp.zeros_like(q))
        st = _nt_dot(k_ref[pl.ds(k0, NA_UTOK), cols], qm) + t_ref[h]
        s_ref[h % 2] = st
        return jnp.max(jnp.max(st.reshape(NA_UTOK // 8, 8, NA_QTOK), axis=0), axis=0, keepdims=True)

    def values(h, m):
        pt = jnp.exp2(s_ref[h % 2] - m).astype(BF16)
        vt = jnp.concatenate([vt_ref[c0 + c, h * NA_HD:(h + 1) * NA_HD, :] for c in range(NA_UROWS // NA_QROWS)],
                             axis=1)
        acc = jnp.dot(jnp.concatenate([vt, ones], axis=0), pt, preferred_element_type=F32)
        return acc[:NA_HD] / acc[NA_HD:NA_HD + 1]

    m_next = scores(0)
    o_prev = None
    for h in range(NA_HEADS):
        m = m_next
        if h + 1 < NA_HEADS:
            m_next = scores(h + 1)
        o_t = values(h, m)
        if h % 2 == 1:
            o_ref[:, (h - 1) * NA_HD:(h + 1) * NA_HD] = jnp.concatenate([o_prev, o_t], axis=0).T.astype(BF16)
        o_prev = o_t


def _na_attn(q, k, vt, table):
    b, s, _ = q.shape
    nblk = s // NA_QTOK
    last = nblk - 1
    variant = lambda bi, a: (jnp.where(a == 0, 0, jnp.where(a == last, 2, 1)), 0, 0, 0)
    return pl.pallas_call(
        functools.partial(_na_attn_kernel, nblk=nblk),
        grid=(b, nblk),
        in_specs=[pl.BlockSpec((None, NA_QTOK, NA_WIDTH), lambda bi, a: (bi, a, 0)),
                  pl.BlockSpec((None, s, NA_WIDTH), lambda bi, a: (bi, 0, 0), pipeline_mode=pl.Buffered(1)),
                  pl.BlockSpec((None, nblk, NA_WIDTH, NA_QTOK), lambda bi, a: (bi, 0, 0, 0),
                               pipeline_mode=pl.Buffered(1)),
                  pl.BlockSpec((None, NA_HEADS, NA_UTOK, NA_QTOK), variant)],
        out_specs=pl.BlockSpec((None, NA_QTOK, NA_WIDTH), lambda bi, a: (bi, a, 0)),
        out_shape=jax.ShapeDtypeStruct((b, s, NA_WIDTH), BF16),
        scratch_shapes=[pltpu.VMEM((2, NA_UTOK, NA_QTOK), F32)],
        compiler_params=_cparams(2),
        name="na_attn",
    )(q, k, vt, table)


def _out_kernel(*refs, emit_h):
    (x_ref, oa_ref, gates_ref, oc_ref, merge_ref,
     ob0_ref, ob1_ref, ob2_ref, ls0_ref, ls1_ref, ls2_ref,
     wpa_ref, wpb_ref, wpc_ref, wout_ref, gpost_ref, mod_ref) = refs[:17]
    rest = refs[17:]
    if emit_h:
        gnext_ref, modn_ref, y_ref, h_ref, so_ref, sl_ref = rest
    else:
        y_ref, so_ref, sl_ref = rest
    tm = x_ref.shape[0]

    for gi, (ob_ref, ls_ref, (_, d)) in enumerate(zip((ob0_ref, ob1_ref, ob2_ref), (ls0_ref, ls1_ref, ls2_ref),
                                                     DIL_PAIRS)):
        for r in range(d):
            rows = slice(None) if d == 1 else pl.ds(r, tm // d, stride=d)
            for c in range(DIL_WIDTH // LANES):
                cols = slice(c * LANES, (c + 1) * LANES)
                so_ref[gi, c, rows, :] = ob_ref[r, :, cols].astype(F32)
                sl_ref[gi, c, rows, :] = ls_ref[r, :, cols]
    slabs = range(DIL_WIDTH // LANES)
    lse = [jnp.concatenate([sl_ref[gi, c] for c in slabs], axis=1) for gi in range(3)]
    o_g = [jnp.concatenate([so_ref[gi, c] for c in slabs], axis=1) for gi in range(3)]
    mx = jnp.maximum(jnp.maximum(lse[0], lse[1]), lse[2])
    e = [jnp.exp2(x - mx) for x in lse]
    o_b = (e[0] * o_g[0] + e[1] * o_g[1] + e[2] * o_g[2]) / (e[0] + e[1] + e[2])

    def gated(o, lo, hi):
        g = gates_ref[:, lo:hi].astype(F32)
        return (o * (g * (1.0 + jnp.tanh(g)))).astype(BF16)

    def merged(idx, act, w_ref):
        t = jnp.tanh(merge_ref[:, idx * D_MODEL:(idx + 1) * D_MODEL].astype(F32))
        return (1.0 + t) * jnp.dot(act, w_ref[...], preferred_element_type=F32)

    a = gated(oa_ref[...].astype(F32), 0, MLA_WIDTH)
    bb = gated(o_b, MLA_WIDTH, MLA_WIDTH + DIL_WIDTH)
    c = gated(oc_ref[...].astype(F32), MLA_WIDTH + DIL_WIDTH, GATE_WIDTH)
    mixed = merged(0, a, wpa_ref) + merged(1, bb, wpb_ref) + merged(2, c, wpc_ref)
    out = jnp.dot(mixed.astype(BF16), wout_ref[...], preferred_element_type=F32)
    normed = out * lax.rsqrt(jnp.mean(out * out, axis=-1, keepdims=True) + NORM_EPS) * gpost_ref[...]
    y = x_ref[...] + mod_ref[2:3, :] * normed
    y_ref[...] = y
    if emit_h:
        h_ref[...] = _modulated_norm(y, gnext_ref[...], modn_ref[...]).astype(BF16)


def _out_layer(x, oa, gates, oc, merge, obs, lses, wpa, wpb, wpc, wout, g_post, mod, g_next, mod_next):
    b, s, _ = x.shape
    tm = TM_OUT
    emit_h = g_next is not None
    tok = lambda n: pl.BlockSpec((None, tm, n), lambda bi, i: (bi, i, 0))
    const = lambda shape: pl.BlockSpec(shape, lambda bi, i: (0,) * len(shape))
    modspec = pl.BlockSpec((None, 3, D_MODEL), lambda bi, i: (bi, 0, 0))
    cls = [pl.BlockSpec((None, d, tm // d, DIL_WIDTH), lambda bi, i: (bi, 0, i, 0)) for _, d in DIL_PAIRS]
    in_specs = [tok(D_MODEL), tok(MLA_WIDTH), tok(GATE_WIDTH), tok(NA_WIDTH), tok(3 * D_MODEL),
                *cls, *cls,
                const((MLA_WIDTH, D_MODEL)), const((DIL_WIDTH, D_MODEL)), const((NA_WIDTH, D_MODEL)),
                const((D_MODEL, D_MODEL)), const((1, D_MODEL)), modspec]
    args = [x, oa, gates, oc, merge, *obs, *lses, wpa, wpb, wpc, wout, g_post.reshape(1, D_MODEL), mod]
    out_specs = [tok(D_MODEL)]
    out_shape = [jax.ShapeDtypeStruct((b, s, D_MODEL), F32)]
    if emit_h:
        in_specs += [const((1, D_MODEL)), modspec]
        args += [g_next.reshape(1, D_MODEL), mod_next]
        out_specs.append(tok(D_MODEL))
        out_shape.append(jax.ShapeDtypeStruct((b, s, D_MODEL), BF16))
    res = pl.pallas_call(
        functools.partial(_out_kernel, emit_h=emit_h),
        grid=(b, s // tm),
        in_specs=in_specs,
        out_specs=out_specs,
        out_shape=out_shape,
        scratch_shapes=[pltpu.VMEM((3, DIL_WIDTH // LANES, tm, LANES), F32)] * 2,
        compiler_params=_cparams(2),
        name="out_layer",
    )(*args)
    return (res[0], res[1]) if emit_h else (res[0], None)


def _rope_tables(s, head_dim, lane_of_dim0, period, scale):
    half = head_dim // 2
    inv = ROPE_THETA ** (-jnp.arange(half, dtype=F32) * 2.0 / head_dim)
    ang = jnp.arange(s, dtype=F32)[:, None] * inv[None, :]
    cos, sin = jnp.cos(ang), jnp.sin(ang)
    lane = np.arange(LANES)
    dim = (lane - lane_of_dim0) % period
    in_rope = dim < head_dim
    idx = dim % half
    cos_t = jnp.where(in_rope[None, :], cos[:, idx], 1.0) * scale
    sin_t = sin[:, idx] * scale
    sin_lo = jnp.where((in_rope & (dim < half))[None, :], -sin_t, 0.0)
    sin_hi = jnp.where((in_rope & (dim >= half))[None, :], sin_t, 0.0)
    return cos_t.astype(F32), sin_lo.astype(F32), sin_hi.astype(F32)


def _layout_weights(w_in, w_uq, w_ukv):
    c = [0] + [int(v) for v in _CUTS]
    cq, ckv, kr, gate_a, qkv_b, gate_b, qkv_c, gate_c, merge = [w_in[:, :, c[i]:c[i + 1]] for i in range(9)]
    zeros = lambda n: jnp.zeros((DEPTH, D_MODEL, n), w_in.dtype)
    w_a = jnp.concatenate([cq, ckv, zeros(MLA_NOPE), kr, zeros(LANES - MLA_NOPE - MLA_ROPE)], axis=-1)
    w_g = 0.5 * jnp.concatenate([gate_a, gate_b, gate_c], axis=-1)
    merge = 0.5 * merge
    qkv_b = qkv_b.reshape(DEPTH, D_MODEL, 3, len(DIL_PAIRS), DIL_WIDTH)
    qkv_b = qkv_b * jnp.array([DIL_HD ** -0.5, 1.0, 1.0], w_in.dtype)[None, None, :, None, None]
    w_b = qkv_b.transpose(0, 1, 3, 2, 4).reshape(DEPTH, D_MODEL, ZB_WIDTH)
    qkv_c = qkv_c.reshape(DEPTH, D_MODEL, 3, NA_WIDTH)
    qkv_c = qkv_c * jnp.array([NA_HD ** -0.5, 1.0, 1.0], w_in.dtype)[None, None, :, None]
    w_cqk = qkv_c[:, :, :2].reshape(DEPTH, D_MODEL, 2 * NA_WIDTH)
    w_cvt = qkv_c[:, :, 2].transpose(0, 2, 1)
    uq = w_uq.reshape(DEPTH, MLA_Q_RANK, MLA_HEADS, MLA_DQK)
    uq = jnp.pad(uq, ((0, 0), (0, 0), (0, 0), (0, MLA_HEAD_PAD - MLA_DQK)))
    w_q = uq.reshape(DEPTH, MLA_Q_RANK, MLA_HEADS * MLA_HEAD_PAD)
    ukv = w_ukv.reshape(DEPTH, MLA_KV_RANK, MLA_HEADS, MLA_NOPE + MLA_V)
    uk = jnp.pad(ukv[..., :MLA_NOPE], ((0, 0), (0, 0), (0, 0), (0, MLA_HEAD_PAD - MLA_NOPE)))
    w_k = uk.reshape(DEPTH, MLA_KV_RANK, MLA_HEADS * MLA_HEAD_PAD)
    w_vt = ukv[..., MLA_NOPE:].reshape(DEPTH, MLA_KV_RANK, MLA_WIDTH).transpose(0, 2, 1)
    bf = lambda w: w.astype(BF16)
    return dict(a=bf(w_a), g=bf(w_g), b=bf(w_b), cqk=bf(w_cqk), cvt=bf(w_cvt), m=bf(merge),
                q=bf(w_q), k=bf(w_k), vt=bf(w_vt))


def _trunk(x, c, p):
    b, s, _ = x.shape
    mods = _ada(c, p["w_ada"], p["b_ada"]).reshape(DEPTH, b, 3, D_MODEL)
    dil_tabs = _rope_tables(s, DIL_HD, 0, DIL_HD, 1.0)
    q_tabs = _rope_tables(s, MLA_ROPE, MLA_NOPE, LANES, MLA_DQK ** -0.5 * LOG2_E)
    k_tabs = _rope_tables(s, MLA_ROPE, MLA_NOPE, LANES, 1.0)
    w = p["w"]
    h = _prenorm(x, p["g_pre"][0], mods[0])
    for l in range(DEPTH):
        za = _proj(h, w["a"][l], "proj_mla")
        gates = _proj(h, w["g"][l], "proj_gates")
        qc, kc, vtc = _proj_na(h, w["cqk"][l], w["cvt"][l])
        merge = _proj(h, w["m"][l], "proj_merge")
        zbs = _proj_dil(h, w["b"][l], dil_tabs)
        q, k, vt = _mla_prep(za, p["g_q"][l], p["g_kv"][l], w["q"][l], w["k"][l], w["vt"][l], q_tabs, k_tabs)
        oa = _mla_attn(q, k, vt)
        dil = [_dil_attn(zb, d) for zb, (_, d) in zip(zbs, DIL_PAIRS)]
        oc = _na_attn(qc, kc, vtc, p["na_bias"][l])
        last = l == DEPTH - 1
        x, h = _out_layer(x, oa, gates, oc, merge, [o for o, _ in dil], [ls for _, ls in dil],
                          p["w_pa"][l], p["w_pb"][l], p["w_pc"][l], p["w_out"][l], p["g_post"][l], mods[l],
                          None if last else p["g_pre"][l + 1], None if last else mods[l + 1])
    return x


def _prepare(w_ada, b_ada, g_pre, g_post, w_in, g_q, w_uq, g_kv, w_ukv, rpb, w_pa, w_pb, w_pc, w_out):
    return dict(w_ada=w_ada.astype(BF16), b_ada=b_ada, g_pre=g_pre, g_post=g_post, g_q=g_q, g_kv=g_kv,
                w=_layout_weights(w_in, w_uq, w_ukv),
                na_bias=[_na_bias(rpb[l]) for l in range(DEPTH)],
                w_pa=(0.5 * w_pa).astype(BF16), w_pb=(0.5 * w_pb).astype(BF16), w_pc=(0.5 * w_pc).astype(BF16),
                w_out=w_out.astype(BF16))


def kernel(x_prompt, x_sample, c_prompt, c_sample, w_ada, b_ada, g_pre, g_post, w_in, g_q, w_uq, g_kv, w_ukv, rpb,
           w_pa, w_pb, w_pc, w_out):
    p = _prepare(w_ada, b_ada, g_pre, g_post, w_in, g_q, w_uq, g_kv, w_ukv, rpb, w_pa, w_pb, w_pc, w_out)
    return (_trunk(x_prompt, c_prompt, p), _trunk(x_sample, c_sample, p))
```

```python
import functools

import jax
import jax.numpy as jnp
import numpy as np
from jax import lax
from jax.experimental import pallas as pl
from jax.experimental.pallas import tpu as pltpu

F32 = jnp.float32
BF16 = jnp.bfloat16

D_MODEL = 1024
DEPTH = 4
GRID_W = 64
ROPE_THETA = 10000.0
NORM_EPS = 1e-6
NEG_INF = -1e30
LOG2_E = float(np.log2(np.e))

MLA_HEADS = 8
MLA_NOPE = 64
MLA_ROPE = 32
MLA_V = 64
MLA_DQK = MLA_NOPE + MLA_ROPE
MLA_Q_RANK = 384
MLA_KV_RANK = 256
MLA_WIDTH = MLA_HEADS * MLA_V
MLA_HEAD_PAD = 128

DIL_PAIRS = ((128, 1), (512, 4), (2048, 16))
DIL_HPG = 4
DIL_HD = 64
DIL_HEADS = 12
DIL_WIDTH = DIL_HPG * DIL_HD
DIL_BAND = 64

NA_HEADS = 8
NA_HD = 64
NA_KH = 8
NA_KW = 16
NA_WIDTH = NA_HEADS * NA_HD

LANES = 128
VMEM_LIMIT = 56 * 1024 * 1024

_CUTS = np.cumsum((MLA_Q_RANK, MLA_KV_RANK, MLA_ROPE, MLA_WIDTH, 3 * DIL_HEADS * DIL_HD, DIL_WIDTH,
                   3 * NA_HEADS * NA_HD, NA_WIDTH, 3 * D_MODEL))
ZA_WIDTH = MLA_Q_RANK + MLA_KV_RANK + LANES
GATE_WIDTH = MLA_WIDTH + DIL_WIDTH + NA_WIDTH
ZB_WIDTH = 3 * DIL_HEADS * DIL_HD
ZB_GROUP = 3 * DIL_WIDTH
ZC_WIDTH = 3 * NA_WIDTH

TM_IN = 512
TM_OUT = 256
MLA_TQ = 512
MLA_KC = 512
MLA_UNROLL = 4


def _cparams(n_grid):
    return pltpu.CompilerParams(dimension_semantics=("arbitrary",) * n_grid, vmem_limit_bytes=VMEM_LIMIT)


def _sigmoid(x):
    return 0.5 * (1.0 + jnp.tanh(0.5 * x))


def _rope_lanes(x, cos, sin_lo, sin_hi, half):
    return x * cos + pltpu.roll(x, half, 1) * sin_hi + pltpu.roll(x, LANES - half, 1) * sin_lo


def _nt_dot(a, b):
    return lax.dot_general(a, b, (((1,), (1,)), ((), ())), preferred_element_type=F32)


def _ada_kernel(c_ref, w_ref, b_ref, o_ref):
    c = c_ref[...]
    c_act = (c * _sigmoid(c)).astype(BF16)
    o_ref[...] = jnp.dot(c_act, w_ref[...], preferred_element_type=F32) + b_ref[...]


def _ada(c, w_ada_bf, b_ada):
    b = c.shape[0]
    return pl.pallas_call(
        _ada_kernel,
        grid=(DEPTH,),
        in_specs=[pl.BlockSpec((b, D_MODEL), lambda l: (0, 0)),
                  pl.BlockSpec((None, D_MODEL, 3 * D_MODEL), lambda l: (l, 0, 0)),
                  pl.BlockSpec((None, 1, 3 * D_MODEL), lambda l: (l, 0, 0))],
        out_specs=pl.BlockSpec((None, b, 3 * D_MODEL), lambda l: (l, 0, 0)),
        out_shape=jax.ShapeDtypeStruct((DEPTH, b, 3 * D_MODEL), F32),
        compiler_params=_cparams(1),
        name="ada",
    )(c, w_ada_bf, b_ada.reshape(DEPTH, 1, 3 * D_MODEL))


def _modulated_norm(x32, g, mod):
    y = x32 * lax.rsqrt(jnp.mean(x32 * x32, axis=-1, keepdims=True) + NORM_EPS) * g
    return y * (1.0 + mod[1:2, :]) + mod[0:1, :]


def _prenorm_kernel(x_ref, g_ref, mod_ref, h_ref):
    h_ref[...] = _modulated_norm(x_ref[...], g_ref[...], mod_ref[...]).astype(BF16)


def _prenorm(x, g, mod):
    b, s, _ = x.shape
    tm = TM_IN
    return pl.pallas_call(
        _prenorm_kernel,
        grid=(b, s // tm),
        in_specs=[pl.BlockSpec((None, tm, D_MODEL), lambda bi, i: (bi, i, 0)),
                  pl.BlockSpec((1, D_MODEL), lambda bi, i: (0, 0)),
                  pl.BlockSpec((None, 3, D_MODEL), lambda bi, i: (bi, 0, 0))],
        out_specs=pl.BlockSpec((None, tm, D_MODEL), lambda bi, i: (bi, i, 0)),
        out_shape=jax.ShapeDtypeStruct((b, s, D_MODEL), BF16),
        compiler_params=_cparams(2),
        name="prenorm",
    )(x, g.reshape(1, D_MODEL), mod)


def _proj_kernel(h_ref, w_ref, o_ref):
    o_ref[...] = jnp.dot(h_ref[...], w_ref[...], preferred_element_type=F32).astype(BF16)


def _proj(h, w, name):
    b, s, _ = h.shape
    n = w.shape[1]
    tm = TM_IN
    return pl.pallas_call(
        _proj_kernel,
        grid=(b, s // tm),
        in_specs=[pl.BlockSpec((None, tm, D_MODEL), lambda bi, i: (bi, i, 0)),
                  pl.BlockSpec((D_MODEL, n), lambda bi, i: (0, 0))],
        out_specs=pl.BlockSpec((None, tm, n), lambda bi, i: (bi, i, 0)),
        out_shape=jax.ShapeDtypeStruct((b, s, n), BF16),
        compiler_params=_cparams(2),
        name=name,
    )(h, w)


def _proj_dil_kernel(h_ref, w_ref, cos_ref, slo_ref, shi_ref, o0_ref, o1_ref, o2_ref, z_ref):
    tm = h_ref.shape[0]
    z = jnp.dot(h_ref[...], w_ref[...], preferred_element_type=F32)
    cos, slo, shi = cos_ref[...], slo_ref[...], shi_ref[...]
    slabs_per_part = DIL_WIDTH // LANES
    for sl in range(ZB_WIDTH // LANES):
        x = z[:, sl * LANES:(sl + 1) * LANES]
        part = (sl // slabs_per_part) % 3
        if part != 2:
            x = _rope_lanes(x, cos, slo, shi, DIL_HD // 2)
        if part == 0:
            x = x * LOG2_E
        z_ref[sl] = x
    slabs_per_group = ZB_GROUP // LANES
    for gi, (o_ref, (_, d)) in enumerate(zip((o0_ref, o1_ref, o2_ref), DIL_PAIRS)):
        for r in range(d):
            rows = slice(None) if d == 1 else pl.ds(r, tm // d, stride=d)
            for c in range(slabs_per_group):
                o_ref[r, :, c * LANES:(c + 1) * LANES] = z_ref[gi * slabs_per_group + c, rows, :].astype(BF16)


def _proj_dil(h, w, tabs):
    b, s, _ = h.shape
    tm = TM_IN
    cos, slo, shi = tabs
    tab_spec = pl.BlockSpec((tm, LANES), lambda i, bi: (i, 0))
    out_shapes, out_specs = [], []
    for _, d in DIL_PAIRS:
        out_shapes.append(jax.ShapeDtypeStruct((b, d, s // d, ZB_GROUP), BF16))
        out_specs.append(pl.BlockSpec((None, d, tm // d, ZB_GROUP), lambda i, bi: (bi, 0, i, 0)))
    return pl.pallas_call(
        _proj_dil_kernel,
        grid=(s // tm, b),
        in_specs=[pl.BlockSpec((None, tm, D_MODEL), lambda i, bi: (bi, i, 0)),
                  pl.BlockSpec((D_MODEL, ZB_WIDTH), lambda i, bi: (0, 0)),
                  tab_spec, tab_spec, tab_spec],
        out_specs=out_specs,
        out_shape=out_shapes,
        scratch_shapes=[pltpu.VMEM((ZB_WIDTH // LANES, tm, LANES), F32)],
        compiler_params=_cparams(2),
        name="proj_dil",
    )(h, w, cos, slo, shi)


def _mla_prep_kernel(za_ref, gq_ref, gkv_ref, wq_ref, wk_ref, wvt_ref,
                     cq_ref, sloq_ref, shiq_ref, ck_ref, slok_ref, shik_ref,
                     q_ref, k_ref, vt_ref):
    def norm(x, g):
        x32 = x.astype(F32)
        return (x32 * lax.rsqrt(jnp.mean(x32 * x32, axis=-1, keepdims=True) + NORM_EPS) * g).astype(BF16)

    cqn = norm(za_ref[:, 0:MLA_Q_RANK], gq_ref[...])
    ckvn = norm(za_ref[:, MLA_Q_RANK:MLA_Q_RANK + MLA_KV_RANK], gkv_ref[...])
    kr = za_ref[:, MLA_Q_RANK + MLA_KV_RANK:ZA_WIDTH].astype(F32)
    kr = _rope_lanes(kr, ck_ref[...], slok_ref[...], shik_ref[...], MLA_ROPE // 2)
    q = jnp.dot(cqn, wq_ref[...], preferred_element_type=F32)
    k = jnp.dot(ckvn, wk_ref[...], preferred_element_type=F32)
    cq, sloq, shiq = cq_ref[...], sloq_ref[...], shiq_ref[...]
    vt = _nt_dot(wvt_ref[...], ckvn).astype(BF16)
    for h in range(MLA_HEADS):
        cols = slice(h * MLA_HEAD_PAD, (h + 1) * MLA_HEAD_PAD)
        q_ref[h] = _rope_lanes(q[:, cols], cq, sloq, shiq, MLA_ROPE // 2).astype(BF16)
        k_ref[h] = (k[:, cols] + kr).astype(BF16)
        vt_ref[h] = vt[h * MLA_V:(h + 1) * MLA_V, :]


def _mla_prep(za, g_q, g_kv, wq, wk, wvt, qtabs, ktabs):
    b, s, _ = za.shape
    tm = MLA_KC
    hp = MLA_HEADS * MLA_HEAD_PAD
    tab_spec = pl.BlockSpec((tm, LANES), lambda i, bi: (i, 0))
    const = lambda shape: pl.BlockSpec(shape, lambda i, bi: (0,) * len(shape))
    return pl.pallas_call(
        _mla_prep_kernel,
        grid=(s // tm, b),
        in_specs=[pl.BlockSpec((None, tm, ZA_WIDTH), lambda i, bi: (bi, i, 0)),
                  const((1, MLA_Q_RANK)), const((1, MLA_KV_RANK)),
                  const((MLA_Q_RANK, hp)), const((MLA_KV_RANK, hp)), const((MLA_WIDTH, MLA_KV_RANK)),
                  tab_spec, tab_spec, tab_spec, tab_spec, tab_spec, tab_spec],
        out_specs=[pl.BlockSpec((None, MLA_HEADS, tm, MLA_HEAD_PAD), lambda i, bi: (bi, 0, i, 0)),
                   pl.BlockSpec((None, MLA_HEADS, tm, MLA_HEAD_PAD), lambda i, bi: (bi, 0, i, 0)),
                   pl.BlockSpec((None, None, MLA_HEADS, MLA_V, tm), lambda i, bi: (bi, i, 0, 0, 0))],
        out_shape=[jax.ShapeDtypeStruct((b, MLA_HEADS, s, MLA_HEAD_PAD), BF16),
                   jax.ShapeDtypeStruct((b, MLA_HEADS, s, MLA_HEAD_PAD), BF16),
                   jax.ShapeDtypeStruct((b, s // tm, MLA_HEADS, MLA_V, tm), BF16)],
        compiler_params=_cparams(2),
        name="mla_prep",
    )(za, g_q.reshape(1, -1), g_kv.reshape(1, -1), wq, wk, wvt, *qtabs, *ktabs)


MLA_ONES_ROWS = 16


def _mla_attn_kernel(q_ref, k_ref, vt_ref, o_ref, s0_ref, s1_ref, ot_ref):
    tq = q_ref.shape[1]
    n_chunks, _, _, kc = vt_ref.shape
    ones = jnp.ones((MLA_ONES_ROWS, kc), BF16)
    acc_rows = MLA_V + MLA_ONES_ROWS
    s_refs = (s0_ref, s1_ref)

    def stage(h_score, h_value, m_value, parity):
        def body(j, carry):
            mx, acc = carry
            start = pl.multiple_of(j * kc, kc)
            if h_score is not None:
                st = _nt_dot(k_ref[h_score, pl.ds(start, kc), :], q_ref[h_score])
                s_refs[parity][pl.ds(start, kc), :] = st
                mx = jnp.maximum(mx, jnp.max(st.reshape(kc // 8, 8, tq), axis=0))
            if h_value is not None:
                p = jnp.exp2(s_refs[1 - parity][pl.ds(start, kc), :] - m_value).astype(BF16)
                vt = jnp.concatenate([vt_ref[j, h_value], ones], axis=0)
                acc = acc + jnp.dot(vt, p, preferred_element_type=F32)
            return mx, acc

        init = (jnp.full((8, tq), NEG_INF, F32), jnp.zeros((acc_rows, tq), F32))
        mx, acc = lax.fori_loop(0, n_chunks, body, init, unroll=MLA_UNROLL)
        if h_value is not None:
            ot_ref[h_value] = acc[:MLA_V] / acc[MLA_V:MLA_V + 1]
        return jnp.max(mx, axis=0, keepdims=True)

    def stage_pair(u, m):
        m = stage(2 * u + 1, 2 * u, m, 1)
        return stage(2 * u + 2, 2 * u + 1, m, 0)

    m = stage(0, None, None, 0)
    m = lax.fori_loop(0, MLA_HEADS // 2 - 1, stage_pair, m)
    m = stage(MLA_HEADS - 1, MLA_HEADS - 2, m, 1)
    stage(None, MLA_HEADS - 1, m, 0)
    for pr in range(MLA_HEADS // 2):
        pair = jnp.concatenate([ot_ref[2 * pr], ot_ref[2 * pr + 1]], axis=0)
        o_ref[:, 2 * pr * MLA_V:(2 * pr + 2) * MLA_V] = pair.T.astype(BF16)


def _mla_attn(q, k, vt):
    b, _, s, _ = q.shape
    n_chunks, kc = vt.shape[1], vt.shape[4]
    tq = min(MLA_TQ, s)
    return pl.pallas_call(
        _mla_attn_kernel,
        grid=(b, s // tq),
        in_specs=[pl.BlockSpec((None, MLA_HEADS, tq, MLA_HEAD_PAD), lambda bi, i: (bi, 0, i, 0)),
                  pl.BlockSpec((None, MLA_HEADS, s, MLA_HEAD_PAD), lambda bi, i: (bi, 0, 0, 0),
                               pipeline_mode=pl.Buffered(1)),
                  pl.BlockSpec((None, n_chunks, MLA_HEADS, MLA_V, kc), lambda bi, i: (bi, 0, 0, 0, 0),
                               pipeline_mode=pl.Buffered(1))],
        out_specs=pl.BlockSpec((None, tq, MLA_WIDTH), lambda bi, i: (bi, i, 0)),
        out_shape=jax.ShapeDtypeStruct((b, s, MLA_WIDTH), BF16),
        scratch_shapes=[pltpu.VMEM((s, tq), F32), pltpu.VMEM((s, tq), F32), pltpu.VMEM((MLA_HEADS, MLA_V, tq), F32)],
        compiler_params=_cparams(2),
        name="mla_attn",
    )(q, k, vt)


def _dil_attn_kernel(qkv_ref, o_ref, lse_ref, s_ref, *, tq, win):
    d, length, _ = qkv_ref.shape
    tiles = length // tq
    n_total = d * tiles
    heads = range(DIL_HPG)
    key_rel = lax.broadcasted_iota(jnp.int32, (tq, win), 1) - lax.broadcasted_iota(jnp.int32, (tq, win), 0)
    lane = lax.broadcasted_iota(jnp.int32, (tq, LANES), 1)
    low_half = lane < DIL_HD
    head_lanes = (low_half, lane >= DIL_HD)

    def coords(n):
        r = n // tiles
        q0 = pl.multiple_of((n - r * tiles) * tq, tq)
        start = pl.multiple_of(jnp.clip(q0 - DIL_BAND, 0, length - win), DIL_BAND)
        return r, q0, start

    def score_step(n):
        r, q0, start = coords(n)
        valid = jnp.abs(key_rel + (start - q0)) <= DIL_BAND
        ms = []
        for h in heads:
            cols = slice((h // 2) * LANES, (h // 2 + 1) * LANES)
            q = qkv_ref[r, pl.ds(q0, tq), cols]
            k = qkv_ref[r, pl.ds(start, win), DIL_WIDTH + cols.start:DIL_WIDTH + cols.stop]
            qm = jnp.where(head_lanes[h % 2], q, jnp.zeros_like(q))
            sc = jnp.where(valid, _nt_dot(qm, k), NEG_INF)
            s_ref[n % 2, h] = sc
            ms.append(jnp.max(sc, axis=-1, keepdims=True))
        return tuple(ms)

    def value_step(n, ms):
        r, q0, start = coords(n)
        res = []
        for h in heads:
            cols = slice(2 * DIL_WIDTH + (h // 2) * LANES, 2 * DIL_WIDTH + (h // 2 + 1) * LANES)
            p = jnp.exp2(s_ref[n % 2, h] - ms[h])
            l = jnp.sum(p, axis=-1, keepdims=True)
            o = jnp.dot(p.astype(BF16), qkv_ref[r, pl.ds(start, win), cols], preferred_element_type=F32) / l
            res.append((o, ms[h] + jnp.log2(l)))
        for pr in range(DIL_HPG // 2):
            cols = slice(pr * LANES, (pr + 1) * LANES)
            o_ref[r, pl.ds(q0, tq), cols] = jnp.where(low_half, res[2 * pr][0], res[2 * pr + 1][0]).astype(BF16)
            lse_ref[r, pl.ds(q0, tq), cols] = jnp.where(low_half, res[2 * pr][1], res[2 * pr + 1][1])

    def body(n, ms):
        ms_next = score_step(jnp.minimum(n + 1, n_total - 1))
        value_step(n, ms)
        return ms_next

    lax.fori_loop(0, n_total, body, score_step(0), unroll=2)


def _dil_attn(zb, d):
    b, _, length, _ = zb.shape
    tq = min(2 * DIL_BAND, length)
    win = min(4 * DIL_BAND, length)
    whole = lambda width: pl.BlockSpec((None, d, length, width), lambda bi: (bi, 0, 0, 0))
    return pl.pallas_call(
        functools.partial(_dil_attn_kernel, tq=tq, win=win),
        grid=(b,),
        in_specs=[whole(ZB_GROUP)],
        out_specs=[whole(DIL_WIDTH), whole(DIL_WIDTH)],
        out_shape=[jax.ShapeDtypeStruct((b, d, length, DIL_WIDTH), BF16),
                   jax.ShapeDtypeStruct((b, d, length, DIL_WIDTH), F32)],
        scratch_shapes=[pltpu.VMEM((2, DIL_HPG, tq, win), F32)],
        compiler_params=_cparams(1),
        name=f"dil_attn_d{d}",
    )(zb)


NA_QROWS = 4
NA_UROWS = NA_QROWS + NA_KH
NA_QTOK = NA_QROWS * GRID_W
NA_UTOK = NA_UROWS * GRID_W


def _proj_na_kernel(h_ref, wqk_ref, wvt_ref, q_ref, k_ref, vt_ref):
    h = h_ref[...]
    z = jnp.dot(h, wqk_ref[...], preferred_element_type=F32)
    q_ref[...] = (z[:, :NA_WIDTH] * LOG2_E).astype(BF16)
    k_ref[...] = z[:, NA_WIDTH:].astype(BF16)
    vt = _nt_dot(wvt_ref[...], h)
    for c in range(vt_ref.shape[0]):
        vt_ref[c] = vt[:, c * NA_QTOK:(c + 1) * NA_QTOK].astype(BF16)


def _proj_na(h, wqk, wvt):
    b, s, _ = h.shape
    tm = TM_IN
    tok = pl.BlockSpec((None, tm, NA_WIDTH), lambda bi, i: (bi, i, 0))
    return pl.pallas_call(
        _proj_na_kernel,
        grid=(b, s // tm),
        in_specs=[pl.BlockSpec((None, tm, D_MODEL), lambda bi, i: (bi, i, 0)),
                  pl.BlockSpec((D_MODEL, 2 * NA_WIDTH), lambda bi, i: (0, 0)),
                  pl.BlockSpec((NA_WIDTH, D_MODEL), lambda bi, i: (0, 0))],
        out_specs=[tok, tok, pl.BlockSpec((None, tm // NA_QTOK, NA_WIDTH, NA_QTOK), lambda bi, i: (bi, i, 0, 0))],
        out_shape=[jax.ShapeDtypeStruct((b, s, NA_WIDTH), BF16), jax.ShapeDtypeStruct((b, s, NA_WIDTH), BF16),
                   jax.ShapeDtypeStruct((b, s // NA_QTOK, NA_WIDTH, NA_QTOK), BF16)],
        compiler_params=_cparams(2),
        name="proj_na",
    )(h, wqk, wvt)


NA_VARIANTS = ((0, lambda i: 0), (-NA_KH // 2, lambda i: i), (-NA_KH, lambda i: NA_KH // 2))


def _na_bias_kernel(rpb_ref, t_ref):
    h = pl.program_id(0)
    shape = (GRID_W, LANES)
    w = lax.broadcasted_iota(jnp.int32, shape, 0)
    lane = lax.broadcasted_iota(jnp.int32, shape, 1)
    c = lane & (GRID_W - 1)
    first = lane < GRID_W
    cs = jnp.clip(c - NA_KW // 2, 0, GRID_W - NA_KW)
    inside = (w >= cs) & (w < cs + NA_KW)
    off = w - c + NA_KW - 1
    neg = jnp.full(shape, NEG_INF, F32)
    tiles = {(None, None): neg}
    for v, (delta, lo) in enumerate(NA_VARIANTS):
        for jk in range(NA_UROWS):
            for ip in range(NA_QROWS // 2):
                ro = []
                for i in (2 * ip, 2 * ip + 1):
                    ok = lo(i) <= jk < lo(i) + NA_KH
                    ro.append(jk - i + NA_KH - 1 + delta if ok else None)
                ro = tuple(ro)
                if ro not in tiles:
                    def body(kk, acc, ro=ro):
                        a = NEG_INF if ro[0] is None else rpb_ref[h, ro[0], kk]
                        b = NEG_INF if ro[1] is None else rpb_ref[h, ro[1], kk]
                        return jnp.where(off == kk, jnp.where(first, a, b), acc)

                    acc = lax.fori_loop(0, 2 * NA_KW - 1, body, neg)
                    tiles[ro] = jnp.where(inside, acc * LOG2_E, NEG_INF)
                t_ref[v, jk * GRID_W:(jk + 1) * GRID_W, ip * LANES:(ip + 1) * LANES] = tiles[ro]


def _na_bias(rpb_l):
    nv = len(NA_VARIANTS)
    return pl.pallas_call(
        _na_bias_kernel,
        grid=(NA_HEADS,),
        in_specs=[pl.BlockSpec(memory_space=pltpu.SMEM)],
        out_specs=pl.BlockSpec((nv, None, NA_UTOK, NA_QTOK), lambda h: (0, h, 0, 0)),
        out_shape=jax.ShapeDtypeStruct((nv, NA_HEADS, NA_UTOK, NA_QTOK), F32),
        compiler_params=_cparams(1),
        name="na_bias",
    )(rpb_l)


def _na_attn_kernel(q_ref, k_ref, vt_ref, t_ref, o_ref, s_ref, *, nblk):
    c0 = jnp.clip(pl.program_id(1) - 1, 0, nblk - NA_UROWS // NA_QROWS)
    k0 = pl.multiple_of(c0 * NA_QTOK, NA_QTOK)
    lane = lax.broadcasted_iota(jnp.int32, (NA_QTOK, LANES), 1)
    head_lanes = (lane < NA_HD, lane >= NA_HD)
    ones = jnp.ones((MLA_ONES_ROWS, NA_UTOK), BF16)

    def scores(h):
        cols = slice((h // 2) * LANES, (h // 2 + 1) * LANES)
        q = q_ref[:, cols]
        qm = jnp.where(head_lanes[h % 2], q, jnp.zeros_like(q))
        st = _nt_dot(k_ref[pl.ds(k0, NA_UTOK), cols], qm) + t_ref[h]
        s_ref[h % 2] = st
        return jnp.max(jnp.max(st.reshape(NA_UTOK // 8, 8, NA_QTOK), axis=0), axis=0, keepdims=True)

    def values(h, m):
        pt = jnp.exp2(s_ref[h % 2] - m).astype(BF16)
        vt = jnp.concatenate([vt_ref[c0 + c, h * NA_HD:(h + 1) * NA_HD, :] for c in range(NA_UROWS // NA_QROWS)],
                             axis=1)
        acc = jnp.dot(jnp.concatenate([vt, ones], axis=0), pt, preferred_element_type=F32)
        return acc[:NA_HD] / acc[NA_HD:NA_HD + 1]

    m_next = scores(0)
    o_prev = None
    for h in range(NA_HEADS):
        m = m_next
        if h + 1 < NA_HEADS:
            m_next = scores(h + 1)
        o_t = values(h, m)
        if h % 2 == 1:
            o_ref[:, (h - 1) * NA_HD:(h + 1) * NA_HD] = jnp.concatenate([o_prev, o_t], axis=0).T.astype(BF16)
        o_prev = o_t


def _na_attn(q, k, vt, table):
    b, s, _ = q.shape
    nblk = s // NA_QTOK
    last = nblk - 1
    variant = lambda bi, a: (jnp.where(a == 0, 0, jnp.where(a == last, 2, 1)), 0, 0, 0)
    return pl.pallas_call(
        functools.partial(_na_attn_kernel, nblk=nblk),
        grid=(b, nblk),
        in_specs=[pl.BlockSpec((None, NA_QTOK, NA_WIDTH), lambda bi, a: (bi, a, 0)),
                  pl.BlockSpec((None, s, NA_WIDTH), lambda bi, a: (bi, 0, 0), pipeline_mode=pl.Buffered(1)),
                  pl.BlockSpec((None, nblk, NA_WIDTH, NA_QTOK), lambda bi, a: (bi, 0, 0, 0),
                               pipeline_mode=pl.Buffered(1)),
                  pl.BlockSpec((None, NA_HEADS, NA_UTOK, NA_QTOK), variant)],
        out_specs=pl.BlockSpec((None, NA_QTOK, NA_WIDTH), lambda bi, a: (bi, a, 0)),
        out_shape=jax.ShapeDtypeStruct((b, s, NA_WIDTH), BF16),
        scratch_shapes=[pltpu.VMEM((2, NA_UTOK, NA_QTOK), F32)],
        compiler_params=_cparams(2),
        name="na_attn",
    )(q, k, vt, table)


def _out_kernel(*refs, emit_h):
    (x_ref, oa_ref, gates_ref, oc_ref, merge_ref,
     ob0_ref, ob1_ref, ob2_ref, ls0_ref, ls1_ref, ls2_ref,
     wpa_ref, wpb_ref, wpc_ref, wout_ref, gpost_ref, mod_ref) = refs[:17]
    rest = refs[17:]
    if emit_h:
        gnext_ref, modn_ref, y_ref, h_ref, so_ref, sl_ref = rest
    else:
        y_ref, so_ref, sl_ref = rest
    tm = x_ref.shape[0]

    for gi, (ob_ref, ls_ref, (_, d)) in enumerate(zip((ob0_ref, ob1_ref, ob2_ref), (ls0_ref, ls1_ref, ls2_ref),
                                                     DIL_PAIRS)):
        for r in range(d):
            rows = slice(None) if d == 1 else pl.ds(r, tm // d, stride=d)
            for c in range(DIL_WIDTH // LANES):
                cols = slice(c * LANES, (c + 1) * LANES)
                so_ref[gi, c, rows, :] = ob_ref[r, :, cols].astype(F32)
                sl_ref[gi, c, rows, :] = ls_ref[r, :, cols]
    slabs = range(DIL_WIDTH // LANES)
    lse = [jnp.concatenate([sl_ref[gi, c] for c in slabs], axis=1) for gi in range(3)]
    o_g = [jnp.concatenate([so_ref[gi, c] for c in slabs], axis=1) for gi in range(3)]
    mx = jnp.maximum(jnp.maximum(lse[0], lse[1]), lse[2])
    e = [jnp.exp2(x - mx) for x in lse]
    o_b = (e[0] * o_g[0] + e[1] * o_g[1] + e[2] * o_g[2]) / (e[0] + e[1] + e[2])

    def gated(o, lo, hi):
        g = gates_ref[:, lo:hi].astype(F32)
        return (o * (g * (1.0 + jnp.tanh(g)))).astype(BF16)

    def merged(idx, act, w_ref):
        t = jnp.tanh(merge_ref[:, idx * D_MODEL:(idx + 1) * D_MODEL].astype(F32))
        return (1.0 + t) * jnp.dot(act, w_ref[...], preferred_element_type=F32)

    a = gated(oa_ref[...].astype(F32), 0, MLA_WIDTH)
    bb = gated(o_b, MLA_WIDTH, MLA_WIDTH + DIL_WIDTH)
    c = gated(oc_ref[...].astype(F32), MLA_WIDTH + DIL_WIDTH, GATE_WIDTH)
    mixed = merged(0, a, wpa_ref) + merged(1, bb, wpb_ref) + merged(2, c, wpc_ref)
    out = jnp.dot(mixed.astype(BF16), wout_ref[...], preferred_element_type=F32)
    normed = out * lax.rsqrt(jnp.mean(out * out, axis=-1, keepdims=True) + NORM_EPS) * gpost_ref[...]
    y = x_ref[...] + mod_ref[2:3, :] * normed
    y_ref[...] = y
    if emit_h:
        h_ref[...] = _modulated_norm(y, gnext_ref[...], modn_ref[...]).astype(BF16)


def _out_layer(x, oa, gates, oc, merge, obs, lses, wpa, wpb, wpc, wout, g_post, mod, g_next, mod_next):
    b, s, _ = x.shape
    tm = TM_OUT
    emit_h = g_next is not None
    tok = lambda n: pl.BlockSpec((None, tm, n), lambda bi, i: (bi, i, 0))
    const = lambda shape: pl.BlockSpec(shape, lambda bi, i: (0,) * len(shape))
    modspec = pl.BlockSpec((None, 3, D_MODEL), lambda bi, i: (bi, 0, 0))
    cls = [pl.BlockSpec((None, d, tm // d, DIL_WIDTH), lambda bi, i: (bi, 0, i, 0)) for _, d in DIL_PAIRS]
    in_specs = [tok(D_MODEL), tok(MLA_WIDTH), tok(GATE_WIDTH), tok(NA_WIDTH), tok(3 * D_MODEL),
                *cls, *cls,
                const((MLA_WIDTH, D_MODEL)), const((DIL_WIDTH, D_MODEL)), const((NA_WIDTH, D_MODEL)),
                const((D_MODEL, D_MODEL)), const((1, D_MODEL)), modspec]
    args = [x, oa, gates, oc, merge, *obs, *lses, wpa, wpb, wpc, wout, g_post.reshape(1, D_MODEL), mod]
    out_specs = [tok(D_MODEL)]
    out_shape = [jax.ShapeDtypeStruct((b, s, D_MODEL), F32)]
    if emit_h:
        in_specs += [const((1, D_MODEL)), modspec]
        args += [g_next.reshape(1, D_MODEL), mod_next]
        out_specs.append(tok(D_MODEL))
        out_shape.append(jax.ShapeDtypeStruct((b, s, D_MODEL), BF16))
    res = pl.pallas_call(
        functools.partial(_out_kernel, emit_h=emit_h),
        grid=(b, s // tm),
        in_specs=in_specs,
        out_specs=out_specs,
        out_shape=out_shape,
        scratch_shapes=[pltpu.VMEM((3, DIL_WIDTH // LANES, tm, LANES), F32)] * 2,
        compiler_params=_cparams(2),
        name="out_layer",
    )(*args)
    return (res[0], res[1]) if emit_h else (res[0], None)


def _rope_tables(s, head_dim, lane_of_dim0, period, scale):
    half = head_dim // 2
    inv = ROPE_THETA ** (-jnp.arange(half, dtype=F32) * 2.0 / head_dim)
    ang = jnp.arange(s, dtype=F32)[:, None] * inv[None, :]
    cos, sin = jnp.cos(ang), jnp.sin(ang)
    lane = np.arange(LANES)
    dim = (lane - lane_of_dim0) % period
    in_rope = dim < head_dim
    idx = dim % half
    cos_t = jnp.where(in_rope[None, :], cos[:, idx], 1.0) * scale
    sin_t = sin[:, idx] * scale
    sin_lo = jnp.where((in_rope & (dim < half))[None, :], -sin_t, 0.0)
    sin_hi = jnp.where((in_rope & (dim >= half))[None, :], sin_t, 0.0)
    return cos_t.astype(F32), sin_lo.astype(F32), sin_hi.astype(F32)


def _layout_weights(w_in, w_uq, w_ukv):
    c = [0] + [int(v) for v in _CUTS]
    cq, ckv, kr, gate_a, qkv_b, gate_b, qkv_c, gate_c, merge = [w_in[:, :, c[i]:c[i + 1]] for i in range(9)]
    zeros = lambda n: jnp.zeros((DEPTH, D_MODEL, n), w_in.dtype)
    w_a = jnp.concatenate([cq, ckv, zeros(MLA_NOPE), kr, zeros(LANES - MLA_NOPE - MLA_ROPE)], axis=-1)
    w_g = 0.5 * jnp.concatenate([gate_a, gate_b, gate_c], axis=-1)
    merge = 0.5 * merge
    qkv_b = qkv_b.reshape(DEPTH, D_MODEL, 3, len(DIL_PAIRS), DIL_WIDTH)
    qkv_b = qkv_b * jnp.array([DIL_HD ** -0.5, 1.0, 1.0], w_in.dtype)[None, None, :, None, None]
    w_b = qkv_b.transpose(0, 1, 3, 2, 4).reshape(DEPTH, D_MODEL, ZB_WIDTH)
    qkv_c = qkv_c.reshape(DEPTH, D_MODEL, 3, NA_WIDTH)
    qkv_c = qkv_c * jnp.array([NA_HD ** -0.5, 1.0, 1.0], w_in.dtype)[None, None, :, None]
    w_cqk = qkv_c[:, :, :2].reshape(DEPTH, D_MODEL, 2 * NA_WIDTH)
    w_cvt = qkv_c[:, :, 2].transpose(0, 2, 1)
    uq = w_uq.reshape(DEPTH, MLA_Q_RANK, MLA_HEADS, MLA_DQK)
    uq = jnp.pad(uq, ((0, 0), (0, 0), (0, 0), (0, MLA_HEAD_PAD - MLA_DQK)))
    w_q = uq.reshape(DEPTH, MLA_Q_RANK, MLA_HEADS * MLA_HEAD_PAD)
    ukv = w_ukv.reshape(DEPTH, MLA_KV_RANK, MLA_HEADS, MLA_NOPE + MLA_V)
    uk = jnp.pad(ukv[..., :MLA_NOPE], ((0, 0), (0, 0), (0, 0), (0, MLA_HEAD_PAD - MLA_NOPE)))
    w_k = uk.reshape(DEPTH, MLA_KV_RANK, MLA_HEADS * MLA_HEAD_PAD)
    w_vt = ukv[..., MLA_NOPE:].reshape(DEPTH, MLA_KV_RANK, MLA_WIDTH).transpose(0, 2, 1)
    bf = lambda w: w.astype(BF16)
    return dict(a=bf(w_a), g=bf(w_g), b=bf(w_b), cqk=bf(w_cqk), cvt=bf(w_cvt), m=bf(merge),
                q=bf(w_q), k=bf(w_k), vt=bf(w_vt))


def _trunk(x, c, p):
    b, s, _ = x.shape
    mods = _ada(c, p["w_ada"], p["b_ada"]).reshape(DEPTH, b, 3, D_MODEL)
    dil_tabs = _rope_tables(s, DIL_HD, 0, DIL_HD, 1.0)
    q_tabs = _rope_tables(s, MLA_ROPE, MLA_NOPE, LANES, MLA_DQK ** -0.5 * LOG2_E)
    k_tabs = _rope_tables(s, MLA_ROPE, MLA_NOPE, LANES, 1.0)
    w = p["w"]
    h = _prenorm(x, p["g_pre"][0], mods[0])
    for l in range(DEPTH):
        za = _proj(h, w["a"][l], "proj_mla")
        gates = _proj(h, w["g"][l], "proj_gates")
        qc, kc, vtc = _proj_na(h, w["cqk"][l], w["cvt"][l])
        merge = _proj(h, w["m"][l], "proj_merge")
        zbs = _proj_dil(h, w["b"][l], dil_tabs)
        q, k, vt = _mla_prep(za, p["g_q"][l], p["g_kv"][l], w["q"][l], w["k"][l], w["vt"][l], q_tabs, k_tabs)
        oa = _mla_attn(q, k, vt)
        dil = [_dil_attn(zb, d) for zb, (_, d) in zip(zbs, DIL_PAIRS)]
        oc = _na_attn(qc, kc, vtc, p["na_bias"][l])
        last = l == DEPTH - 1
        x, h = _out_layer(x, oa, gates, oc, merge, [o for o, _ in dil], [ls for _, ls in dil],
                          p["w_pa"][l], p["w_pb"][l], p["w_pc"][l], p["w_out"][l], p["g_post"][l], mods[l],
                          None if last else p["g_pre"][l + 1], None if last else mods[l + 1])
    return x


def _prepare(w_ada, b_ada, g_pre, g_post, w_in, g_q, w_uq, g_kv, w_ukv, rpb, w_pa, w_pb, w_pc, w_out):
    return dict(w_ada=w_ada.astype(BF16), b_ada=b_ada, g_pre=g_pre, g_post=g_post, g_q=g_q, g_kv=g_kv,
                w=_layout_weights(w_in, w_uq, w_ukv),
                na_bias=[_na_bias(rpb[l]) for l in range(DEPTH)],
                w_pa=(0.5 * w_pa).astype(BF16), w_pb=(0.5 * w_pb).astype(BF16), w_pc=(0.5 * w_pc).astype(BF16),
                w_out=w_out.astype(BF16))


def kernel(x_prompt, x_sample, c_prompt, c_sample, w_ada, b_ada, g_pre, g_post, w_in, g_q, w_uq, g_kv, w_ukv, rpb,
           w_pa, w_pb, w_pc, w_out):
    p = _prepare(w_ada, b_ada, g_pre, g_post, w_in, g_q, w_uq, g_kv, w_ukv, rpb, w_pa, w_pb, w_pc, w_out)
    return (_trunk(x_prompt, c_prompt, p), _trunk(x_sample, c_sample, p))
```

```python
import functools

import jax
import jax.numpy as jnp
import numpy as np
from jax import lax
from jax.experimental import pallas as pl
from jax.experimental.pallas import tpu as pltpu

F32 = jnp.float32
BF16 = jnp.bfloat16

D_MODEL = 1024
DEPTH = 4
GRID_W = 64
ROPE_THETA = 10000.0
NORM_EPS = 1e-6
NEG_INF = -1e30
LOG2_E = float(np.log2(np.e))

MLA_HEADS = 8
MLA_NOPE = 64
MLA_ROPE = 32
MLA_V = 64
MLA_DQK = MLA_NOPE + MLA_ROPE
MLA_Q_RANK = 384
MLA_KV_RANK = 256
MLA_WIDTH = MLA_HEADS * MLA_V
MLA_HEAD_PAD = 128

DIL_PAIRS = ((128, 1), (512, 4), (2048, 16))
DIL_HPG = 4
DIL_HD = 64
DIL_HEADS = 12
DIL_WIDTH = DIL_HPG * DIL_HD
DIL_BAND = 64

NA_HEADS = 8
NA_HD = 64
NA_KH = 8
NA_KW = 16
NA_WIDTH = NA_HEADS * NA_HD

LANES = 128
VMEM_LIMIT = 56 * 1024 * 1024

_CUTS = np.cumsum((MLA_Q_RANK, MLA_KV_RANK, MLA_ROPE, MLA_WIDTH, 3 * DIL_HEADS * DIL_HD, DIL_WIDTH,
                   3 * NA_HEADS * NA_HD, NA_WIDTH, 3 * D_MODEL))
ZA_WIDTH = MLA_Q_RANK + MLA_KV_RANK + LANES
GATE_WIDTH = MLA_WIDTH + DIL_WIDTH + NA_WIDTH
ZB_WIDTH = 3 * DIL_HEADS * DIL_HD
ZB_GROUP = 3 * DIL_WIDTH
ZC_WIDTH = 3 * NA_WIDTH

TM_IN = 512
TM_OUT = 256
MLA_TQ = 512
MLA_KC = 512
MLA_UNROLL = 4


def _cparams(n_grid):
    return pltpu.CompilerParams(dimension_semantics=("arbitrary",) * n_grid, vmem_limit_bytes=VMEM_LIMIT)


def _sigmoid(x):
    return 0.5 * (1.0 + jnp.tanh(0.5 * x))


def _rope_lanes(x, cos, sin_lo, sin_hi, half):
    return x * cos + pltpu.roll(x, half, 1) * sin_hi + pltpu.roll(x, LANES - half, 1) * sin_lo


def _nt_dot(a, b):
    return lax.dot_general(a, b, (((1,), (1,)), ((), ())), preferred_element_type=F32)


def _ada_kernel(c_ref, w_ref, b_ref, o_ref):
    c = c_ref[...]
    c_act = (c * _sigmoid(c)).astype(BF16)
    o_ref[...] = jnp.dot(c_act, w_ref[...], preferred_element_type=F32) + b_ref[...]


def _ada(c, w_ada_bf, b_ada):
    b = c.shape[0]
    return pl.pallas_call(
        _ada_kernel,
        grid=(DEPTH,),
        in_specs=[pl.BlockSpec((b, D_MODEL), lambda l: (0, 0)),
                  pl.BlockSpec((None, D_MODEL, 3 * D_MODEL), lambda l: (l, 0, 0)),
                  pl.BlockSpec((None, 1, 3 * D_MODEL), lambda l: (l, 0, 0))],
        out_specs=pl.BlockSpec((None, b, 3 * D_MODEL), lambda l: (l, 0, 0)),
        out_shape=jax.ShapeDtypeStruct((DEPTH, b, 3 * D_MODEL), F32),
        compiler_params=_cparams(1),
        name="ada",
    )(c, w_ada_bf, b_ada.reshape(DEPTH, 1, 3 * D_MODEL))


def _modulated_norm(x32, g, mod):
    y = x32 * lax.rsqrt(jnp.mean(x32 * x32, axis=-1, keepdims=True) + NORM_EPS) * g
    return y * (1.0 + mod[1:2, :]) + mod[0:1, :]


def _prenorm_kernel(x_ref, g_ref, mod_ref, h_ref):
    h_ref[...] = _modulated_norm(x_ref[...], g_ref[...], mod_ref[...]).astype(BF16)


def _prenorm(x, g, mod):
    b, s, _ = x.shape
    tm = TM_IN
    return pl.pallas_call(
        _prenorm_kernel,
        grid=(b, s // tm),
        in_specs=[pl.BlockSpec((None, tm, D_MODEL), lambda bi, i: (bi, i, 0)),
                  pl.BlockSpec((1, D_MODEL), lambda bi, i: (0, 0)),
                  pl.BlockSpec((None, 3, D_MODEL), lambda bi, i: (bi, 0, 0))],
        out_specs=pl.BlockSpec((None, tm, D_MODEL), lambda bi, i: (bi, i, 0)),
        out_shape=jax.ShapeDtypeStruct((b, s, D_MODEL), BF16),
        compiler_params=_cparams(2),
        name="prenorm",
    )(x, g.reshape(1, D_MODEL), mod)


def _proj_kernel(h_ref, w_ref, o_ref):
    o_ref[...] = jnp.dot(h_ref[...], w_ref[...], preferred_element_type=F32).astype(BF16)


def _proj(h, w, name):
    b, s, _ = h.shape
    n = w.shape[1]
    tm = TM_IN
    return pl.pallas_call(
        _proj_kernel,
        grid=(b, s // tm),
        in_specs=[pl.BlockSpec((None, tm, D_MODEL), lambda bi, i: (bi, i, 0)),
                  pl.BlockSpec((D_MODEL, n), lambda bi, i: (0, 0))],
        out_specs=pl.BlockSpec((None, tm, n), lambda bi, i: (bi, i, 0)),
        out_shape=jax.ShapeDtypeStruct((b, s, n), BF16),
        compiler_params=_cparams(2),
        name=name,
    )(h, w)


def _proj_dil_kernel(h_ref, w_ref, cos_ref, slo_ref, shi_ref, o0_ref, o1_ref, o2_ref, z_ref):
    tm = h_ref.shape[0]
    z = jnp.dot(h_ref[...], w_ref[...], preferred_element_type=F32)
    cos, slo, shi = cos_ref[...], slo_ref[...], shi_ref[...]
    slabs_per_part = DIL_WIDTH // LANES
    for sl in range(ZB_WIDTH // LANES):
        x = z[:, sl * LANES:(sl + 1) * LANES]
        part = (sl // slabs_per_part) % 3
        if part != 2:
            x = _rope_lanes(x, cos, slo, shi, DIL_HD // 2)
        if part == 0:
            x = x * LOG2_E
        z_ref[sl] = x
    slabs_per_group = ZB_GROUP // LANES
    for gi, (o_ref, (_, d)) in enumerate(zip((o0_ref, o1_ref, o2_ref), DIL_PAIRS)):
        for r in range(d):
            rows = slice(None) if d == 1 else pl.ds(r, tm // d, stride=d)
            for c in range(slabs_per_group):
                o_ref[r, :, c * LANES:(c + 1) * LANES] = z_ref[gi * slabs_per_group + c, rows, :].astype(BF16)


def _proj_dil(h, w, tabs):
    b, s, _ = h.shape
    tm = TM_IN
    cos, slo, shi = tabs
    tab_spec = pl.BlockSpec((tm, LANES), lambda i, bi: (i, 0))
    out_shapes, out_specs = [], []
    for _, d in DIL_PAIRS:
        out_shapes.append(jax.ShapeDtypeStruct((b, d, s // d, ZB_GROUP), BF16))
        out_specs.append(pl.BlockSpec((None, d, tm // d, ZB_GROUP), lambda i, bi: (bi, 0, i, 0)))
    return pl.pallas_call(
        _proj_dil_kernel,
        grid=(s // tm, b),
        in_specs=[pl.BlockSpec((None, tm, D_MODEL), lambda i, bi: (bi, i, 0)),
                  pl.BlockSpec((D_MODEL, ZB_WIDTH), lambda i, bi: (0, 0)),
                  tab_spec, tab_spec, tab_spec],
        out_specs=out_specs,
        out_shape=out_shapes,
        scratch_shapes=[pltpu.VMEM((ZB_WIDTH // LANES, tm, LANES), F32)],
        compiler_params=_cparams(2),
        name="proj_dil",
    )(h, w, cos, slo, shi)


def _mla_prep_kernel(za_ref, gq_ref, gkv_ref, wq_ref, wk_ref, wvt_ref,
                     cq_ref, sloq_ref, shiq_ref, ck_ref, slok_ref, shik_ref,
                     q_ref, k_ref, vt_ref):
    def norm(x, g):
        x32 = x.astype(F32)
        return (x32 * lax.rsqrt(jnp.mean(x32 * x32, axis=-1, keepdims=True) + NORM_EPS) * g).astype(BF16)

    cqn = norm(za_ref[:, 0:MLA_Q_RANK], gq_ref[...])
    ckvn = norm(za_ref[:, MLA_Q_RANK:MLA_Q_RANK + MLA_KV_RANK], gkv_ref[...])
    kr = za_ref[:, MLA_Q_RANK + MLA_KV_RANK:ZA_WIDTH].astype(F32)
    kr = _rope_lanes(kr, ck_ref[...], slok_ref[...], shik_ref[...], MLA_ROPE // 2)
    q = jnp.dot(cqn, wq_ref[...], preferred_element_type=F32)
    k = jnp.dot(ckvn, wk_ref[...], preferred_element_type=F32)
    cq, sloq, shiq = cq_ref[...], sloq_ref[...], shiq_ref[...]
    vt = _nt_dot(wvt_ref[...], ckvn).astype(BF16)
    for h in range(MLA_HEADS):
        cols = slice(h * MLA_HEAD_PAD, (h + 1) * MLA_HEAD_PAD)
        q_ref[h] = _rope_lanes(q[:, cols], cq, sloq, shiq, MLA_ROPE // 2).astype(BF16)
        k_ref[h] = (k[:, cols] + kr).astype(BF16)
        vt_ref[h] = vt[h * MLA_V:(h + 1) * MLA_V, :]


def _mla_prep(za, g_q, g_kv, wq, wk, wvt, qtabs, ktabs):
    b, s, _ = za.shape
    tm = MLA_KC
    hp = MLA_HEADS * MLA_HEAD_PAD
    tab_spec = pl.BlockSpec((tm, LANES), lambda i, bi: (i, 0))
    const = lambda shape: pl.BlockSpec(shape, lambda i, bi: (0,) * len(shape))
    return pl.pallas_call(
        _mla_prep_kernel,
        grid=(s // tm, b),
        in_specs=[pl.BlockSpec((None, tm, ZA_WIDTH), lambda i, bi: (bi, i, 0)),
                  const((1, MLA_Q_RANK)), const((1, MLA_KV_RANK)),
                  const((MLA_Q_RANK, hp)), const((MLA_KV_RANK, hp)), const((MLA_WIDTH, MLA_KV_RANK)),
                  tab_spec, tab_spec, tab_spec, tab_spec, tab_spec, tab_spec],
        out_specs=[pl.BlockSpec((None, MLA_HEADS, tm, MLA_HEAD_PAD), lambda i, bi: (bi, 0, i, 0)),
                   pl.BlockSpec((None, MLA_HEADS, tm, MLA_HEAD_PAD), lambda i, bi: (bi, 0, i, 0)),
                   pl.BlockSpec((None, None, MLA_HEADS, MLA_V, tm), lambda i, bi: (bi, i, 0, 0, 0))],
        out_shape=[jax.ShapeDtypeStruct((b, MLA_HEADS, s, MLA_HEAD_PAD), BF16),
                   jax.ShapeDtypeStruct((b, MLA_HEADS, s, MLA_HEAD_PAD), BF16),
                   jax.ShapeDtypeStruct((b, s // tm, MLA_HEADS, MLA_V, tm), BF16)],
        compiler_params=_cparams(2),
        name="mla_prep",
    )(za, g_q.reshape(1, -1), g_kv.reshape(1, -1), wq, wk, wvt, *qtabs, *ktabs)


MLA_ONES_ROWS = 16


def _mla_attn_kernel(q_ref, k_ref, vt_ref, o_ref, s0_ref, s1_ref, ot_ref):
    tq = q_ref.shape[1]
    n_chunks, _, _, kc = vt_ref.shape
    ones = jnp.ones((MLA_ONES_ROWS, kc), BF16)
    acc_rows = MLA_V + MLA_ONES_ROWS
    s_refs = (s0_ref, s1_ref)

    def stage(h_score, h_value, m_value, parity):
        def body(j, carry):
            mx, acc = carry
            start = pl.multiple_of(j * kc, kc)
            if h_score is not None:
                st = _nt_dot(k_ref[h_score, pl.ds(start, kc), :], q_ref[h_score])
                s_refs[parity][pl.ds(start, kc), :] = st
                mx = jnp.maximum(mx, jnp.max(st.reshape(kc // 8, 8, tq), axis=0))
            if h_value is not None:
                p = jnp.exp2(s_refs[1 - parity][pl.ds(start, kc), :] - m_value).astype(BF16)
                vt = jnp.concatenate([vt_ref[j, h_value], ones], axis=0)
                acc = acc + jnp.dot(vt, p, preferred_element_type=F32)
            return mx, acc

        init = (jnp.full((8, tq), NEG_INF, F32), jnp.zeros((acc_rows, tq), F32))
        mx, acc = lax.fori_loop(0, n_chunks, body, init, unroll=MLA_UNROLL)
        if h_value is not None:
            ot_ref[h_value] = acc[:MLA_V] / acc[MLA_V:MLA_V + 1]
        return jnp.max(mx, axis=0, keepdims=True)

    def stage_pair(u, m):
        m = stage(2 * u + 1, 2 * u, m, 1)
        return stage(2 * u + 2, 2 * u + 1, m, 0)

    m = stage(0, None, None, 0)
    m = lax.fori_loop(0, MLA_HEADS // 2 - 1, stage_pair, m)
    m = stage(MLA_HEADS - 1, MLA_HEADS - 2, m, 1)
    stage(None, MLA_HEADS - 1, m, 0)
    for pr in range(MLA_HEADS // 2):
        pair = jnp.concatenate([ot_ref[2 * pr], ot_ref[2 * pr + 1]], axis=0)
        o_ref[:, 2 * pr * MLA_V:(2 * pr + 2) * MLA_V] = pair.T.astype(BF16)


def _mla_attn(q, k, vt):
    b, _, s, _ = q.shape
    n_chunks, kc = vt.shape[1], vt.shape[4]
    tq = min(MLA_TQ, s)
    return pl.pallas_call(
        _mla_attn_kernel,
        grid=(b, s // tq),
        in_specs=[pl.BlockSpec((None, MLA_HEADS, tq, MLA_HEAD_PAD), lambda bi, i: (bi, 0, i, 0)),
                  pl.BlockSpec((None, MLA_HEADS, s, MLA_HEAD_PAD), lambda bi, i: (bi, 0, 0, 0)),
                  pl.BlockSpec((None, n_chunks, MLA_HEADS, MLA_V, kc), lambda bi, i: (bi, 0, 0, 0, 0))],
        out_specs=pl.BlockSpec((None, tq, MLA_WIDTH), lambda bi, i: (bi, i, 0)),
        out_shape=jax.ShapeDtypeStruct((b, s, MLA_WIDTH), BF16),
        scratch_shapes=[pltpu.VMEM((s, tq), F32), pltpu.VMEM((s, tq), F32), pltpu.VMEM((MLA_HEADS, MLA_V, tq), F32)],
        compiler_params=_cparams(2),
        name="mla_attn",
    )(q, k, vt)


def _dil_attn_kernel(qkv_ref, o_ref, lse_ref, s_ref, *, tq, win):
    d, length, _ = qkv_ref.shape
    tiles = length // tq
    n_total = d * tiles
    heads = range(DIL_HPG)
    key_rel = lax.broadcasted_iota(jnp.int32, (tq, win), 1) - lax.broadcasted_iota(jnp.int32, (tq, win), 0)
    lane = lax.broadcasted_iota(jnp.int32, (tq, LANES), 1)
    low_half = lane < DIL_HD
    head_lanes = (low_half, lane >= DIL_HD)

    def coords(n):
        r = n // tiles
        q0 = pl.multiple_of((n - r * tiles) * tq, tq)
        start = pl.multiple_of(jnp.clip(q0 - DIL_BAND, 0, length - win), DIL_BAND)
        return r, q0, start

    def score_step(n):
        r, q0, start = coords(n)
        valid = jnp.abs(key_rel + (start - q0)) <= DIL_BAND
        ms = []
        for h in heads:
            cols = slice((h // 2) * LANES, (h // 2 + 1) * LANES)
            q = qkv_ref[r, pl.ds(q0, tq), cols]
            k = qkv_ref[r, pl.ds(start, win), DIL_WIDTH + cols.start:DIL_WIDTH + cols.stop]
            qm = jnp.where(head_lanes[h % 2], q, jnp.zeros_like(q))
            sc = jnp.where(valid, _nt_dot(qm, k), NEG_INF)
            s_ref[n % 2, h] = sc
            ms.append(jnp.max(sc, axis=-1, keepdims=True))
        return tuple(ms)

    def value_step(n, ms):
        r, q0, start = coords(n)
        res = []
        for h in heads:
            cols = slice(2 * DIL_WIDTH + (h // 2) * LANES, 2 * DIL_WIDTH + (h // 2 + 1) * LANES)
            p = jnp.exp2(s_ref[n % 2, h] - ms[h])
            l = jnp.sum(p, axis=-1, keepdims=True)
            o = jnp.dot(p.astype(BF16), qkv_ref[r, pl.ds(start, win), cols], preferred_element_type=F32) / l
            res.append((o, ms[h] + jnp.log2(l)))
        for pr in range(DIL_HPG // 2):
            cols = slice(pr * LANES, (pr + 1) * LANES)
            o_ref[r, pl.ds(q0, tq), cols] = jnp.where(low_half, res[2 * pr][0], res[2 * pr + 1][0]).astype(BF16)
            lse_ref[r, pl.ds(q0, tq), cols] = jnp.where(low_half, res[2 * pr][1], res[2 * pr + 1][1])

    def body(n, ms):
        ms_next = score_step(jnp.minimum(n + 1, n_total - 1))
        value_step(n, ms)
        return ms_next

    lax.fori_loop(0, n_total, body, score_step(0), unroll=2)


def _dil_attn(zb, d):
    b, _, length, _ = zb.shape
    tq = min(2 * DIL_BAND, length)
    win = min(4 * DIL_BAND, length)
    whole = lambda width: pl.BlockSpec((None, d, length, width), lambda bi: (bi, 0, 0, 0))
    return pl.pallas_call(
        functools.partial(_dil_attn_kernel, tq=tq, win=win),
        grid=(b,),
        in_specs=[whole(ZB_GROUP)],
        out_specs=[whole(DIL_WIDTH), whole(DIL_WIDTH)],
        out_shape=[jax.ShapeDtypeStruct((b, d, length, DIL_WIDTH), BF16),
                   jax.ShapeDtypeStruct((b, d, length, DIL_WIDTH), F32)],
        scratch_shapes=[pltpu.VMEM((2, DIL_HPG, tq, win), F32)],
        compiler_params=_cparams(1),
        name=f"dil_attn_d{d}",
    )(zb)


NA_QROWS = 4
NA_UROWS = NA_QROWS + NA_KH
NA_QTOK = NA_QROWS * GRID_W
NA_UTOK = NA_UROWS * GRID_W


def _proj_na_kernel(h_ref, wqk_ref, wvt_ref, q_ref, k_ref, vt_ref):
    h = h_ref[...]
    z = jnp.dot(h, wqk_ref[...], preferred_element_type=F32)
    q_ref[...] = (z[:, :NA_WIDTH] * LOG2_E).astype(BF16)
    k_ref[...] = z[:, NA_WIDTH:].astype(BF16)
    vt = _nt_dot(wvt_ref[...], h)
    for c in range(vt_ref.shape[0]):
        vt_ref[c] = vt[:, c * NA_QTOK:(c + 1) * NA_QTOK].astype(BF16)


def _proj_na(h, wqk, wvt):
    b, s, _ = h.shape
    tm = TM_IN
    tok = pl.BlockSpec((None, tm, NA_WIDTH), lambda bi, i: (bi, i, 0))
    return pl.pallas_call(
        _proj_na_kernel,
        grid=(b, s // tm),
        in_specs=[pl.BlockSpec((None, tm, D_MODEL), lambda bi, i: (bi, i, 0)),
                  pl.BlockSpec((D_MODEL, 2 * NA_WIDTH), lambda bi, i: (0, 0)),
                  pl.BlockSpec((NA_WIDTH, D_MODEL), lambda bi, i: (0, 0))],
        out_specs=[tok, tok, pl.BlockSpec((None, tm // NA_QTOK, NA_WIDTH, NA_QTOK), lambda bi, i: (bi, i, 0, 0))],
        out_shape=[jax.ShapeDtypeStruct((b, s, NA_WIDTH), BF16), jax.ShapeDtypeStruct((b, s, NA_WIDTH), BF16),
                   jax.ShapeDtypeStruct((b, s // NA_QTOK, NA_WIDTH, NA_QTOK), BF16)],
        compiler_params=_cparams(2),
        name="proj_na",
    )(h, wqk, wvt)


NA_VARIANTS = ((0, lambda i: 0), (-NA_KH // 2, lambda i: i), (-NA_KH, lambda i: NA_KH // 2))


def _na_bias_kernel(rpb_ref, t_ref):
    h = pl.program_id(0)
    shape = (GRID_W, LANES)
    w = lax.broadcasted_iota(jnp.int32, shape, 0)
    lane = lax.broadcasted_iota(jnp.int32, shape, 1)
    c = lane & (GRID_W - 1)
    first = lane < GRID_W
    cs = jnp.clip(c - NA_KW // 2, 0, GRID_W - NA_KW)
    inside = (w >= cs) & (w < cs + NA_KW)
    off = w - c + NA_KW - 1
    neg = jnp.full(shape, NEG_INF, F32)
    tiles = {(None, None): neg}
    for v, (delta, lo) in enumerate(NA_VARIANTS):
        for jk in range(NA_UROWS):
            for ip in range(NA_QROWS // 2):
                ro = []
                for i in (2 * ip, 2 * ip + 1):
                    ok = lo(i) <= jk < lo(i) + NA_KH
                    ro.append(jk - i + NA_KH - 1 + delta if ok else None)
                ro = tuple(ro)
                if ro not in tiles:
                    def body(kk, acc, ro=ro):
                        a = NEG_INF if ro[0] is None else rpb_ref[h, ro[0], kk]
                        b = NEG_INF if ro[1] is None else rpb_ref[h, ro[1], kk]
                        return jnp.where(off == kk, jnp.where(first, a, b), acc)

                    acc = lax.fori_loop(0, 2 * NA_KW - 1, body, neg)
                    tiles[ro] = jnp.where(inside, acc * LOG2_E, NEG_INF)
                t_ref[v, jk * GRID_W:(jk + 1) * GRID_W, ip * LANES:(ip + 1) * LANES] = tiles[ro]


def _na_bias(rpb_l):
    nv = len(NA_VARIANTS)
    return pl.pallas_call(
        _na_bias_kernel,
        grid=(NA_HEADS,),
        in_specs=[pl.BlockSpec(memory_space=pltpu.SMEM)],
        out_specs=pl.BlockSpec((nv, None, NA_UTOK, NA_QTOK), lambda h: (0, h, 0, 0)),
        out_shape=jax.ShapeDtypeStruct((nv, NA_HEADS, NA_UTOK, NA_QTOK), F32),
        compiler_params=_cparams(1),
        name="na_bias",
    )(rpb_l)


def _na_attn_kernel(q_ref, k_ref, vt_ref, t_ref, o_ref, s_ref, *, nblk):
    c0 = jnp.clip(pl.program_id(1) - 1, 0, nblk - NA_UROWS // NA_QROWS)
    k0 = pl.multiple_of(c0 * NA_QTOK, NA_QTOK)
    lane = lax.broadcasted_iota(jnp.int32, (NA_QTOK, LANES), 1)
    head_lanes = (lane < NA_HD, lane >= NA_HD)
    ones = jnp.ones((MLA_ONES_ROWS, NA_UTOK), BF16)

    def scores(h):
        cols = slice((h // 2) * LANES, (h // 2 + 1) * LANES)
        q = q_ref[:, cols]
        qm = jnp.where(head_lanes[h % 2], q, jnp.zeros_like(q))
        st = _nt_dot(k_ref[pl.ds(k0, NA_UTOK), cols], qm) + t_ref[h]
        s_ref[h % 2] = st
        return jnp.max(jnp.max(st.reshape(NA_UTOK // 8, 8, NA_QTOK), axis=0), axis=0, keepdims=True)

    def values(h, m):
        pt = jnp.exp2(s_ref[h % 2] - m).astype(BF16)
        vt = jnp.concatenate([vt_ref[c0 + c, h * NA_HD:(h + 1) * NA_HD, :] for c in range(NA_UROWS // NA_QROWS)],
                             axis=1)
        acc = jnp.dot(jnp.concatenate([vt, ones], axis=0), pt, preferred_element_type=F32)
        return acc[:NA_HD] / acc[NA_HD:NA_HD + 1]

    m_next = scores(0)
    o_prev = None
    for h in range(NA_HEADS):
        m = m_next
        if h + 1 < NA_HEADS:
            m_next = scores(h + 1)
        o_t = values(h, m)
        if h % 2 == 1:
            o_ref[:, (h - 1) * NA_HD:(h + 1) * NA_HD] = jnp.concatenate([o_prev, o_t], axis=0).T.astype(BF16)
        o_prev = o_t


def _na_attn(q, k, vt, table):
    b, s, _ = q.shape
    nblk = s // NA_QTOK
    last = nblk - 1
    variant = lambda bi, a: (jnp.where(a == 0, 0, jnp.where(a == last, 2, 1)), 0, 0, 0)
    return pl.pallas_call(
        functools.partial(_na_attn_kernel, nblk=nblk),
        grid=(b, nblk),
        in_specs=[pl.BlockSpec((None, NA_QTOK, NA_WIDTH), lambda bi, a: (bi, a, 0)),
                  pl.BlockSpec((None, s, NA_WIDTH), lambda bi, a: (bi, 0, 0)),
                  pl.BlockSpec((None, nblk, NA_WIDTH, NA_QTOK), lambda bi, a: (bi, 0, 0, 0)),
                  pl.BlockSpec((None, NA_HEADS, NA_UTOK, NA_QTOK), variant)],
        out_specs=pl.BlockSpec((None, NA_QTOK, NA_WIDTH), lambda bi, a: (bi, a, 0)),
        out_shape=jax.ShapeDtypeStruct((b, s, NA_WIDTH), BF16),
        scratch_shapes=[pltpu.VMEM((2, NA_UTOK, NA_QTOK), F32)],
        compiler_params=_cparams(2),
        name="na_attn",
    )(q, k, vt, table)


def _out_kernel(*refs, emit_h):
    (x_ref, oa_ref, gates_ref, oc_ref, merge_ref,
     ob0_ref, ob1_ref, ob2_ref, ls0_ref, ls1_ref, ls2_ref,
     wpa_ref, wpb_ref, wpc_ref, wout_ref, gpost_ref, mod_ref) = refs[:17]
    rest = refs[17:]
    if emit_h:
        gnext_ref, modn_ref, y_ref, h_ref, so_ref, sl_ref = rest
    else:
        y_ref, so_ref, sl_ref = rest
    tm = x_ref.shape[0]

    for gi, (ob_ref, ls_ref, (_, d)) in enumerate(zip((ob0_ref, ob1_ref, ob2_ref), (ls0_ref, ls1_ref, ls2_ref),
                                                     DIL_PAIRS)):
        for r in range(d):
            rows = slice(None) if d == 1 else pl.ds(r, tm // d, stride=d)
            for c in range(DIL_WIDTH // LANES):
                cols = slice(c * LANES, (c + 1) * LANES)
                so_ref[gi, c, rows, :] = ob_ref[r, :, cols].astype(F32)
                sl_ref[gi, c, rows, :] = ls_ref[r, :, cols]
    slabs = range(DIL_WIDTH // LANES)
    lse = [jnp.concatenate([sl_ref[gi, c] for c in slabs], axis=1) for gi in range(3)]
    o_g = [jnp.concatenate([so_ref[gi, c] for c in slabs], axis=1) for gi in range(3)]
    mx = jnp.maximum(jnp.maximum(lse[0], lse[1]), lse[2])
    e = [jnp.exp2(x - mx) for x in lse]
    o_b = (e[0] * o_g[0] + e[1] * o_g[1] + e[2] * o_g[2]) / (e[0] + e[1] + e[2])

    def gated(o, lo, hi):
        g = gates_ref[:, lo:hi].astype(F32)
        return (o * (g * (1.0 + jnp.tanh(g)))).astype(BF16)

    def merged(idx, act, w_ref):
        t = jnp.tanh(merge_ref[:, idx * D_MODEL:(idx + 1) * D_MODEL].astype(F32))
        return (1.0 + t) * jnp.dot(act, w_ref[...], preferred_element_type=F32)

    a = gated(oa_ref[...].astype(F32), 0, MLA_WIDTH)
    bb = gated(o_b, MLA_WIDTH, MLA_WIDTH + DIL_WIDTH)
    c = gated(oc_ref[...].astype(F32), MLA_WIDTH + DIL_WIDTH, GATE_WIDTH)
    mixed = merged(0, a, wpa_ref) + merged(1, bb, wpb_ref) + merged(2, c, wpc_ref)
    out = jnp.dot(mixed.astype(BF16), wout_ref[...], preferred_element_type=F32)
    normed = out * lax.rsqrt(jnp.mean(out * out, axis=-1, keepdims=True) + NORM_EPS) * gpost_ref[...]
    y = x_ref[...] + mod_ref[2:3, :] * normed
    y_ref[...] = y
    if emit_h:
        h_ref[...] = _modulated_norm(y, gnext_ref[...], modn_ref[...]).astype(BF16)


def _out_layer(x, oa, gates, oc, merge, obs, lses, wpa, wpb, wpc, wout, g_post, mod, g_next, mod_next):
    b, s, _ = x.shape
    tm = TM_OUT
    emit_h = g_next is not None
    tok = lambda n: pl.BlockSpec((None, tm, n), lambda bi, i: (bi, i, 0))
    const = lambda shape: pl.BlockSpec(shape, lambda bi, i: (0,) * len(shape))
    modspec = pl.BlockSpec((None, 3, D_MODEL), lambda bi, i: (bi, 0, 0))
    cls = [pl.BlockSpec((None, d, tm // d, DIL_WIDTH), lambda bi, i: (bi, 0, i, 0)) for _, d in DIL_PAIRS]
    in_specs = [tok(D_MODEL), tok(MLA_WIDTH), tok(GATE_WIDTH), tok(NA_WIDTH), tok(3 * D_MODEL),
                *cls, *cls,
                const((MLA_WIDTH, D_MODEL)), const((DIL_WIDTH, D_MODEL)), const((NA_WIDTH, D_MODEL)),
                const((D_MODEL, D_MODEL)), const((1, D_MODEL)), modspec]
    args = [x, oa, gates, oc, merge, *obs, *lses, wpa, wpb, wpc, wout, g_post.reshape(1, D_MODEL), mod]
    out_specs = [tok(D_MODEL)]
    out_shape = [jax.ShapeDtypeStruct((b, s, D_MODEL), F32)]
    if emit_h:
        in_specs += [const((1, D_MODEL)), modspec]
        args += [g_next.reshape(1, D_MODEL), mod_next]
        out_specs.append(tok(D_MODEL))
        out_shape.append(jax.ShapeDtypeStruct((b, s, D_MODEL), BF16))
    res = pl.pallas_call(
        functools.partial(_out_kernel, emit_h=emit_h),
        grid=(b, s // tm),
        in_specs=in_specs,
        out_specs=out_specs,
        out_shape=out_shape,
        scratch_shapes=[pltpu.VMEM((3, DIL_WIDTH // LANES, tm, LANES), F32)] * 2,
        compiler_params=_cparams(2),
        name="out_layer",
    )(*args)
    return (res[0], res[1]) if emit_h else (res[0], None)


def _rope_tables(s, head_dim, lane_of_dim0, period, scale):
    half = head_dim // 2
    inv = ROPE_THETA ** (-jnp.arange(half, dtype=F32) * 2.0 / head_dim)
    ang = jnp.arange(s, dtype=F32)[:, None] * inv[None, :]
    cos, sin = jnp.cos(ang), jnp.sin(ang)
    lane = np.arange(LANES)
    dim = (lane - lane_of_dim0) % period
    in_rope = dim < head_dim
    idx = dim % half
    cos_t = jnp.where(in_rope[None, :], cos[:, idx], 1.0) * scale
    sin_t = sin[:, idx] * scale
    sin_lo = jnp.where((in_rope & (dim < half))[None, :], -sin_t, 0.0)
    sin_hi = jnp.where((in_rope & (dim >= half))[None, :], sin_t, 0.0)
    return cos_t.astype(F32), sin_lo.astype(F32), sin_hi.astype(F32)


def _layout_weights(w_in, w_uq, w_ukv):
    w_in, w_uq, w_ukv = w_in.astype(BF16), w_uq.astype(BF16), w_ukv.astype(BF16)
    c = [0] + [int(v) for v in _CUTS]
    cq, ckv, kr, gate_a, qkv_b, gate_b, qkv_c, gate_c, merge = [w_in[:, :, c[i]:c[i + 1]] for i in range(9)]
    zeros = lambda n: jnp.zeros((DEPTH, D_MODEL, n), w_in.dtype)
    w_a = jnp.concatenate([cq, ckv, zeros(MLA_NOPE), kr, zeros(LANES - MLA_NOPE - MLA_ROPE)], axis=-1)
    w_g = 0.5 * jnp.concatenate([gate_a, gate_b, gate_c], axis=-1)
    merge = 0.5 * merge
    qkv_b = qkv_b.reshape(DEPTH, D_MODEL, 3, len(DIL_PAIRS), DIL_WIDTH)
    qkv_b = qkv_b * jnp.array([DIL_HD ** -0.5, 1.0, 1.0], w_in.dtype)[None, None, :, None, None]
    w_b = qkv_b.transpose(0, 1, 3, 2, 4).reshape(DEPTH, D_MODEL, ZB_WIDTH)
    qkv_c = qkv_c.reshape(DEPTH, D_MODEL, 3, NA_WIDTH)
    qkv_c = qkv_c * jnp.array([NA_HD ** -0.5, 1.0, 1.0], w_in.dtype)[None, None, :, None]
    w_cqk = qkv_c[:, :, :2].reshape(DEPTH, D_MODEL, 2 * NA_WIDTH)
    w_cvt = qkv_c[:, :, 2].transpose(0, 2, 1)
    uq = w_uq.reshape(DEPTH, MLA_Q_RANK, MLA_HEADS, MLA_DQK)
    uq = jnp.pad(uq, ((0, 0), (0, 0), (0, 0), (0, MLA_HEAD_PAD - MLA_DQK)))
    w_q = uq.reshape(DEPTH, MLA_Q_RANK, MLA_HEADS * MLA_HEAD_PAD)
    ukv = w_ukv.reshape(DEPTH, MLA_KV_RANK, MLA_HEADS, MLA_NOPE + MLA_V)
    uk = jnp.pad(ukv[..., :MLA_NOPE], ((0, 0), (0, 0), (0, 0), (0, MLA_HEAD_PAD - MLA_NOPE)))
    w_k = uk.reshape(DEPTH, MLA_KV_RANK, MLA_HEADS * MLA_HEAD_PAD)
    w_vt = ukv[..., MLA_NOPE:].reshape(DEPTH, MLA_KV_RANK, MLA_WIDTH).transpose(0, 2, 1)
    bf = lambda w: w.astype(BF16)
    return dict(a=bf(w_a), g=bf(w_g), b=bf(w_b), cqk=bf(w_cqk), cvt=bf(w_cvt), m=bf(merge),
                q=bf(w_q), k=bf(w_k), vt=bf(w_vt))


def _trunk(x, c, p):
    b, s, _ = x.shape
    mods = _ada(c, p["w_ada"], p["b_ada"]).reshape(DEPTH, b, 3, D_MODEL)
    dil_tabs = _rope_tables(s, DIL_HD, 0, DIL_HD, 1.0)
    q_tabs = _rope_tables(s, MLA_ROPE, MLA_NOPE, LANES, MLA_DQK ** -0.5 * LOG2_E)
    k_tabs = _rope_tables(s, MLA_ROPE, MLA_NOPE, LANES, 1.0)
    w = p["w"]
    h = _prenorm(x, p["g_pre"][0], mods[0])
    for l in range(DEPTH):
        za = _proj(h, w["a"][l], "proj_mla")
        gates = _proj(h, w["g"][l], "proj_gates")
        qc, kc, vtc = _proj_na(h, w["cqk"][l], w["cvt"][l])
        merge = _proj(h, w["m"][l], "proj_merge")
        zbs = _proj_dil(h, w["b"][l], dil_tabs)
        q, k, vt = _mla_prep(za, p["g_q"][l], p["g_kv"][l], w["q"][l], w["k"][l], w["vt"][l], q_tabs, k_tabs)
        oa = _mla_attn(q, k, vt)
        dil = [_dil_attn(zb, d) for zb, (_, d) in zip(zbs, DIL_PAIRS)]
        oc = _na_attn(qc, kc, vtc, p["na_bias"][l])
        last = l == DEPTH - 1
        x, h = _out_layer(x, oa, gates, oc, merge, [o for o, _ in dil], [ls for _, ls in dil],
                          p["w_pa"][l], p["w_pb"][l], p["w_pc"][l], p["w_out"][l], p["g_post"][l], mods[l],
                          None if last else p["g_pre"][l + 1], None if last else mods[l + 1])
    return x


def _prepare(w_ada, b_ada, g_pre, g_post, w_in, g_q, w_uq, g_kv, w_ukv, rpb, w_pa, w_pb, w_pc, w_out):
    return dict(w_ada=w_ada.astype(BF16), b_ada=b_ada, g_pre=g_pre, g_post=g_post, g_q=g_q, g_kv=g_kv,
                w=_layout_weights(w_in, w_uq, w_ukv),
                na_bias=[_na_bias(rpb[l]) for l in range(DEPTH)],
                w_pa=(0.5 * w_pa).astype(BF16), w_pb=(0.5 * w_pb).astype(BF16), w_pc=(0.5 * w_pc).astype(BF16),
                w_out=w_out.astype(BF16))


def kernel(x_prompt, x_sample, c_prompt, c_sample, w_ada, b_ada, g_pre, g_post, w_in, g_q, w_uq, g_kv, w_ukv, rpb,
           w_pa, w_pb, w_pc, w_out):
    p = _prepare(w_ada, b_ada, g_pre, g_post, w_in, g_q, w_uq, g_kv, w_ukv, rpb, w_pa, w_pb, w_pc, w_out)
    return (_trunk(x_prompt, c_prompt, p), _trunk(x_sample, c_sample, p))
```

```python
import functools

import jax
import jax.numpy as jnp
import numpy as np
from jax import lax
from jax.experimental import pallas as pl
from jax.experimental.pallas import tpu as pltpu

F32 = jnp.float32
BF16 = jnp.bfloat16

D_MODEL = 1024
DEPTH = 4
GRID_W = 64
ROPE_THETA = 10000.0
NORM_EPS = 1e-6
NEG_INF = -1e30
LOG2_E = float(np.log2(np.e))

MLA_HEADS = 8
MLA_NOPE = 64
MLA_ROPE = 32
MLA_V = 64
MLA_DQK = MLA_NOPE + MLA_ROPE
MLA_Q_RANK = 384
MLA_KV_RANK = 256
MLA_WIDTH = MLA_HEADS * MLA_V
MLA_HEAD_PAD = 128

DIL_PAIRS = ((128, 1), (512, 4), (2048, 16))
DIL_HPG = 4
DIL_HD = 64
DIL_HEADS = 12
DIL_WIDTH = DIL_HPG * DIL_HD
DIL_BAND = 64

NA_HEADS = 8
NA_HD = 64
NA_KH = 8
NA_KW = 16
NA_WIDTH = NA_HEADS * NA_HD

LANES = 128
VMEM_LIMIT = 56 * 1024 * 1024

_CUTS = np.cumsum((MLA_Q_RANK, MLA_KV_RANK, MLA_ROPE, MLA_WIDTH, 3 * DIL_HEADS * DIL_HD, DIL_WIDTH,
                   3 * NA_HEADS * NA_HD, NA_WIDTH, 3 * D_MODEL))
ZA_WIDTH = MLA_Q_RANK + MLA_KV_RANK + LANES
GATE_WIDTH = MLA_WIDTH + DIL_WIDTH + NA_WIDTH
ZB_WIDTH = 3 * DIL_HEADS * DIL_HD
ZB_GROUP = 3 * DIL_WIDTH
ZC_WIDTH = 3 * NA_WIDTH

TM_IN = 512
TM_OUT = 256
MLA_TQ = 512
MLA_KC = 512
MLA_UNROLL = 4


def _cparams(n_grid):
    return pltpu.CompilerParams(dimension_semantics=("arbitrary",) * n_grid, vmem_limit_bytes=VMEM_LIMIT)


def _sigmoid(x):
    return 0.5 * (1.0 + jnp.tanh(0.5 * x))


def _rope_lanes(x, cos, sin):
    return x * cos + pltpu.roll(x, LANES // 2, 1) * sin


def _nt_dot(a, b):
    return lax.dot_general(a, b, (((1,), (1,)), ((), ())), preferred_element_type=F32)


def _ada_kernel(c_ref, w_ref, b_ref, o_ref):
    c = c_ref[...]
    c_act = (c * _sigmoid(c)).astype(BF16)
    o_ref[...] = jnp.dot(c_act, w_ref[...], preferred_element_type=F32) + b_ref[...]


def _ada(c, w_ada_bf, b_ada):
    b = c.shape[0]
    return pl.pallas_call(
        _ada_kernel,
        grid=(DEPTH,),
        in_specs=[pl.BlockSpec((b, D_MODEL), lambda l: (0, 0)),
                  pl.BlockSpec((None, D_MODEL, 3 * D_MODEL), lambda l: (l, 0, 0)),
                  pl.BlockSpec((None, 1, 3 * D_MODEL), lambda l: (l, 0, 0))],
        out_specs=pl.BlockSpec((None, b, 3 * D_MODEL), lambda l: (l, 0, 0)),
        out_shape=jax.ShapeDtypeStruct((DEPTH, b, 3 * D_MODEL), F32),
        compiler_params=_cparams(1),
        name="ada",
    )(c, w_ada_bf, b_ada.reshape(DEPTH, 1, 3 * D_MODEL))


def _modulated_norm(x32, g, mod):
    y = x32 * lax.rsqrt(jnp.mean(x32 * x32, axis=-1, keepdims=True) + NORM_EPS) * g
    return y * (1.0 + mod[1:2, :]) + mod[0:1, :]


def _prenorm_kernel(x_ref, g_ref, mod_ref, h_ref):
    h_ref[...] = _modulated_norm(x_ref[...], g_ref[...], mod_ref[...]).astype(BF16)


def _prenorm(x, g, mod):
    b, s, _ = x.shape
    tm = TM_IN
    return pl.pallas_call(
        _prenorm_kernel,
        grid=(b, s // tm),
        in_specs=[pl.BlockSpec((None, tm, D_MODEL), lambda bi, i: (bi, i, 0)),
                  pl.BlockSpec((1, D_MODEL), lambda bi, i: (0, 0)),
                  pl.BlockSpec((None, 3, D_MODEL), lambda bi, i: (bi, 0, 0))],
        out_specs=pl.BlockSpec((None, tm, D_MODEL), lambda bi, i: (bi, i, 0)),
        out_shape=jax.ShapeDtypeStruct((b, s, D_MODEL), BF16),
        compiler_params=_cparams(2),
        name="prenorm",
    )(x, g.reshape(1, D_MODEL), mod)


def _proj_kernel(h_ref, w_ref, o_ref):
    o_ref[...] = jnp.dot(h_ref[...], w_ref[...], preferred_element_type=F32).astype(BF16)


def _proj(h, w, name):
    b, s, _ = h.shape
    n = w.shape[1]
    tm = TM_IN
    return pl.pallas_call(
        _proj_kernel,
        grid=(b, s // tm),
        in_specs=[pl.BlockSpec((None, tm, D_MODEL), lambda bi, i: (bi, i, 0)),
                  pl.BlockSpec((D_MODEL, n), lambda bi, i: (0, 0))],
        out_specs=pl.BlockSpec((None, tm, n), lambda bi, i: (bi, i, 0)),
        out_shape=jax.ShapeDtypeStruct((b, s, n), BF16),
        compiler_params=_cparams(2),
        name=name,
    )(h, w)


def _proj_dil_kernel(h_ref, w_ref, cos_ref, sin_ref, o0_ref, o1_ref, o2_ref, z0_ref, z1_ref, z2_ref):
    tm = h_ref.shape[0]
    h = h_ref[...]
    cos, sin = cos_ref[...], sin_ref[...]
    slabs_per_part = DIL_WIDTH // LANES
    slabs_per_group = ZB_GROUP // LANES
    for gi, (o_ref, z_ref, (_, d)) in enumerate(zip((o0_ref, o1_ref, o2_ref), (z0_ref, z1_ref, z2_ref), DIL_PAIRS)):
        z = jnp.dot(h, w_ref[:, gi * ZB_GROUP:(gi + 1) * ZB_GROUP], preferred_element_type=F32)
        for c in range(slabs_per_group):
            x = z[:, c * LANES:(c + 1) * LANES]
            part = c // slabs_per_part
            if part != 2:
                x = _rope_lanes(x, cos, sin)
            if part == 0:
                x = x * LOG2_E
            z_ref[c] = x
        for r in range(d):
            rows = slice(None) if d == 1 else pl.ds(r, tm // d, stride=d)
            for c in range(slabs_per_group):
                o_ref[r, :, c * LANES:(c + 1) * LANES] = z_ref[c, rows, :].astype(BF16)


def _proj_dil(h, w, tabs):
    b, s, _ = h.shape
    tm = TM_IN
    cos, sin = tabs
    tab_spec = pl.BlockSpec((tm, LANES), lambda i, bi: (i, 0))
    out_shapes, out_specs = [], []
    for _, d in DIL_PAIRS:
        out_shapes.append(jax.ShapeDtypeStruct((b, d, s // d, ZB_GROUP), BF16))
        out_specs.append(pl.BlockSpec((None, d, tm // d, ZB_GROUP), lambda i, bi: (bi, 0, i, 0)))
    return pl.pallas_call(
        _proj_dil_kernel,
        grid=(s // tm, b),
        in_specs=[pl.BlockSpec((None, tm, D_MODEL), lambda i, bi: (bi, i, 0)),
                  pl.BlockSpec((D_MODEL, ZB_WIDTH), lambda i, bi: (0, 0)),
                  tab_spec, tab_spec],
        out_specs=out_specs,
        out_shape=out_shapes,
        scratch_shapes=[pltpu.VMEM((ZB_GROUP // LANES, tm, LANES), F32)] * 3,
        compiler_params=_cparams(2),
        name="proj_dil",
    )(h, w, cos, sin)


def _mla_prep_kernel(za_ref, gq_ref, gkv_ref, wq_ref, wk_ref, wvt_ref,
                     cq_ref, sq_ref, ck_ref, sk_ref, q_ref, k_ref, vt_ref):
    def norm(x, g):
        x32 = x.astype(F32)
        return (x32 * lax.rsqrt(jnp.mean(x32 * x32, axis=-1, keepdims=True) + NORM_EPS) * g).astype(BF16)

    cqn = norm(za_ref[:, 0:MLA_Q_RANK], gq_ref[...])
    ckvn = norm(za_ref[:, MLA_Q_RANK:MLA_Q_RANK + MLA_KV_RANK], gkv_ref[...])
    kr = za_ref[:, MLA_Q_RANK + MLA_KV_RANK:ZA_WIDTH].astype(F32)
    kr = _rope_lanes(kr, ck_ref[...], sk_ref[...])
    q = jnp.dot(cqn, wq_ref[...], preferred_element_type=F32)
    k = jnp.dot(ckvn, wk_ref[...], preferred_element_type=F32)
    cq, sq = cq_ref[...], sq_ref[...]
    vt = _nt_dot(wvt_ref[...], ckvn).astype(BF16)
    for h in range(MLA_HEADS):
        cols = slice(h * MLA_HEAD_PAD, (h + 1) * MLA_HEAD_PAD)
        q_ref[h] = _rope_lanes(q[:, cols], cq, sq).astype(BF16)
        k_ref[h] = (k[:, cols] + kr).astype(BF16)
        vt_ref[h] = vt[h * MLA_V:(h + 1) * MLA_V, :]


def _mla_prep(za, g_q, g_kv, wq, wk, wvt, qtabs, ktabs):
    b, s, _ = za.shape
    tm = MLA_KC
    hp = MLA_HEADS * MLA_HEAD_PAD
    tab_spec = pl.BlockSpec((tm, LANES), lambda i, bi: (i, 0))
    const = lambda shape: pl.BlockSpec(shape, lambda i, bi: (0,) * len(shape))
    return pl.pallas_call(
        _mla_prep_kernel,
        grid=(s // tm, b),
        in_specs=[pl.BlockSpec((None, tm, ZA_WIDTH), lambda i, bi: (bi, i, 0)),
                  const((1, MLA_Q_RANK)), const((1, MLA_KV_RANK)),
                  const((MLA_Q_RANK, hp)), const((MLA_KV_RANK, hp)), const((MLA_WIDTH, MLA_KV_RANK)),
                  tab_spec, tab_spec, tab_spec, tab_spec],
        out_specs=[pl.BlockSpec((None, MLA_HEADS, tm, MLA_HEAD_PAD), lambda i, bi: (bi, 0, i, 0)),
                   pl.BlockSpec((None, MLA_HEADS, tm, MLA_HEAD_PAD), lambda i, bi: (bi, 0, i, 0)),
                   pl.BlockSpec((None, None, MLA_HEADS, MLA_V, tm), lambda i, bi: (bi, i, 0, 0, 0))],
        out_shape=[jax.ShapeDtypeStruct((b, MLA_HEADS, s, MLA_HEAD_PAD), BF16),
                   jax.ShapeDtypeStruct((b, MLA_HEADS, s, MLA_HEAD_PAD), BF16),
                   jax.ShapeDtypeStruct((b, s // tm, MLA_HEADS, MLA_V, tm), BF16)],
        compiler_params=_cparams(2),
        name="mla_prep",
    )(za, g_q.reshape(1, -1), g_kv.reshape(1, -1), wq, wk, wvt, *qtabs, *ktabs)


MLA_ONES_ROWS = 16


def _mla_attn_kernel(q_ref, k_ref, vt_ref, o_ref, s0_ref, s1_ref, ot_ref):
    tq = q_ref.shape[1]
    n_chunks, _, _, kc = vt_ref.shape
    ones = jnp.ones((MLA_ONES_ROWS, kc), BF16)
    acc_rows = MLA_V + MLA_ONES_ROWS
    s_refs = (s0_ref, s1_ref)

    def stage(h_score, h_value, m_value, parity):
        def body(j, carry):
            mx, acc = carry
            start = pl.multiple_of(j * kc, kc)
            if h_score is not None:
                st = _nt_dot(k_ref[h_score, pl.ds(start, kc), :], q_ref[h_score])
                s_refs[parity][pl.ds(start, kc), :] = st
                mx = jnp.maximum(mx, jnp.max(st.reshape(kc // 8, 8, tq), axis=0))
            if h_value is not None:
                p = jnp.exp2(s_refs[1 - parity][pl.ds(start, kc), :] - m_value).astype(BF16)
                vt = jnp.concatenate([vt_ref[j, h_value], ones], axis=0)
                acc = acc + jnp.dot(vt, p, preferred_element_type=F32)
            return mx, acc

        init = (jnp.full((8, tq), NEG_INF, F32), jnp.zeros((acc_rows, tq), F32))
        mx, acc = lax.fori_loop(0, n_chunks, body, init, unroll=MLA_UNROLL)
        if h_value is not None:
            ot_ref[h_value] = acc[:MLA_V] / acc[MLA_V:MLA_V + 1]
        return jnp.max(mx, axis=0, keepdims=True)

    def stage_pair(u, m):
        m = stage(2 * u + 1, 2 * u, m, 1)
        return stage(2 * u + 2, 2 * u + 1, m, 0)

    m = stage(0, None, None, 0)
    m = lax.fori_loop(0, MLA_HEADS // 2 - 1, stage_pair, m)
    m = stage(MLA_HEADS - 1, MLA_HEADS - 2, m, 1)
    stage(None, MLA_HEADS - 1, m, 0)
    for pr in range(MLA_HEADS // 2):
        pair = jnp.concatenate([ot_ref[2 * pr], ot_ref[2 * pr + 1]], axis=0)
        o_ref[:, 2 * pr * MLA_V:(2 * pr + 2) * MLA_V] = pair.T.astype(BF16)


def _mla_attn(q, k, vt):
    b, _, s, _ = q.shape
    n_chunks, kc = vt.shape[1], vt.shape[4]
    tq = min(MLA_TQ, s)
    return pl.pallas_call(
        _mla_attn_kernel,
        grid=(b, s // tq),
        in_specs=[pl.BlockSpec((None, MLA_HEADS, tq, MLA_HEAD_PAD), lambda bi, i: (bi, 0, i, 0)),
                  pl.BlockSpec((None, MLA_HEADS, s, MLA_HEAD_PAD), lambda bi, i: (bi, 0, 0, 0)),
                  pl.BlockSpec((None, n_chunks, MLA_HEADS, MLA_V, kc), lambda bi, i: (bi, 0, 0, 0, 0))],
        out_specs=pl.BlockSpec((None, tq, MLA_WIDTH), lambda bi, i: (bi, i, 0)),
        out_shape=jax.ShapeDtypeStruct((b, s, MLA_WIDTH), BF16),
        scratch_shapes=[pltpu.VMEM((s, tq), F32), pltpu.VMEM((s, tq), F32), pltpu.VMEM((MLA_HEADS, MLA_V, tq), F32)],
        compiler_params=_cparams(2),
        name="mla_attn",
    )(q, k, vt)


def _dil_attn_kernel(qkv_ref, o_ref, lse_ref, s_ref, *, tq, win):
    d, length, _ = qkv_ref.shape
    tiles = length // tq
    n_total = d * tiles
    heads = range(DIL_HPG)
    key_rel = lax.broadcasted_iota(jnp.int32, (tq, win), 1) - lax.broadcasted_iota(jnp.int32, (tq, win), 0)
    lane = lax.broadcasted_iota(jnp.int32, (tq, LANES), 1)
    low_half = lane < DIL_HD
    qk_head = (lane // (DIL_HD // 2)) % 2
    head_lanes = (qk_head == 0, qk_head == 1)

    def coords(n):
        r = n // tiles
        q0 = pl.multiple_of((n - r * tiles) * tq, tq)
        start = pl.multiple_of(jnp.clip(q0 - DIL_BAND, 0, length - win), DIL_BAND)
        return r, q0, start

    def score_step(n):
        r, q0, start = coords(n)
        valid = jnp.abs(key_rel + (start - q0)) <= DIL_BAND
        ms = []
        for h in heads:
            cols = slice((h // 2) * LANES, (h // 2 + 1) * LANES)
            q = qkv_ref[r, pl.ds(q0, tq), cols]
            k = qkv_ref[r, pl.ds(start, win), DIL_WIDTH + cols.start:DIL_WIDTH + cols.stop]
            qm = jnp.where(head_lanes[h % 2], q, jnp.zeros_like(q))
            sc = jnp.where(valid, _nt_dot(qm, k), NEG_INF)
            s_ref[n % 2, h] = sc
            ms.append(jnp.max(sc, axis=-1, keepdims=True))
        return tuple(ms)

    def value_step(n, ms):
        r, q0, start = coords(n)
        res = []
        for h in heads:
            cols = slice(2 * DIL_WIDTH + (h // 2) * LANES, 2 * DIL_WIDTH + (h // 2 + 1) * LANES)
            p = jnp.exp2(s_ref[n % 2, h] - ms[h])
            l = jnp.sum(p, axis=-1, keepdims=True)
            o = jnp.dot(p.astype(BF16), qkv_ref[r, pl.ds(start, win), cols], preferred_element_type=F32) / l
            res.append((o, ms[h] + jnp.log2(l)))
        for pr in range(DIL_HPG // 2):
            cols = slice(pr * LANES, (pr + 1) * LANES)
            o_ref[r, pl.ds(q0, tq), cols] = jnp.where(low_half, res[2 * pr][0], res[2 * pr + 1][0]).astype(BF16)
            lse_ref[r, pl.ds(q0, tq), cols] = jnp.where(low_half, res[2 * pr][1], res[2 * pr + 1][1])

    def body(n, ms):
        ms_next = score_step(jnp.minimum(n + 1, n_total - 1))
        value_step(n, ms)
        return ms_next

    lax.fori_loop(0, n_total, body, score_step(0), unroll=4)


def _dil_attn(zb, d):
    b, _, length, _ = zb.shape
    tq = min(2 * DIL_BAND, length)
    win = min(4 * DIL_BAND, length)
    whole = lambda width: pl.BlockSpec((None, d, length, width), lambda bi: (bi, 0, 0, 0))
    return pl.pallas_call(
        functools.partial(_dil_attn_kernel, tq=tq, win=win),
        grid=(b,),
        in_specs=[whole(ZB_GROUP)],
        out_specs=[whole(DIL_WIDTH), whole(DIL_WIDTH)],
        out_shape=[jax.ShapeDtypeStruct((b, d, length, DIL_WIDTH), BF16),
                   jax.ShapeDtypeStruct((b, d, length, DIL_WIDTH), F32)],
        scratch_shapes=[pltpu.VMEM((2, DIL_HPG, tq, win), F32)],
        compiler_params=_cparams(1),
        name=f"dil_attn_d{d}",
    )(zb)


NA_QROWS = 4
NA_UROWS = NA_QROWS + NA_KH
NA_QTOK = NA_QROWS * GRID_W
NA_UTOK = NA_UROWS * GRID_W


def _proj_na_kernel(h_ref, wqk_ref, wvt_ref, q_ref, k_ref, vt_ref):
    h = h_ref[...]
    z = jnp.dot(h, wqk_ref[...], preferred_element_type=F32)
    q_ref[...] = (z[:, :NA_WIDTH] * LOG2_E).astype(BF16)
    k_ref[...] = z[:, NA_WIDTH:].astype(BF16)
    vt = _nt_dot(wvt_ref[...], h)
    for c in range(vt_ref.shape[0]):
        vt_ref[c] = vt[:, c * NA_QTOK:(c + 1) * NA_QTOK].astype(BF16)


def _proj_na(h, wqk, wvt):
    b, s, _ = h.shape
    tm = TM_IN
    tok = pl.BlockSpec((None, tm, NA_WIDTH), lambda bi, i: (bi, i, 0))
    return pl.pallas_call(
        _proj_na_kernel,
        grid=(b, s // tm),
        in_specs=[pl.BlockSpec((None, tm, D_MODEL), lambda bi, i: (bi, i, 0)),
                  pl.BlockSpec((D_MODEL, 2 * NA_WIDTH), lambda bi, i: (0, 0)),
                  pl.BlockSpec((NA_WIDTH, D_MODEL), lambda bi, i: (0, 0))],
        out_specs=[tok, tok, pl.BlockSpec((None, tm // NA_QTOK, NA_WIDTH, NA_QTOK), lambda bi, i: (bi, i, 0, 0))],
        out_shape=[jax.ShapeDtypeStruct((b, s, NA_WIDTH), BF16), jax.ShapeDtypeStruct((b, s, NA_WIDTH), BF16),
                   jax.ShapeDtypeStruct((b, s // NA_QTOK, NA_WIDTH, NA_QTOK), BF16)],
        compiler_params=_cparams(2),
        name="proj_na",
    )(h, wqk, wvt)


NA_VARIANTS = ((0, lambda i: 0), (-NA_KH // 2, lambda i: i), (-NA_KH, lambda i: NA_KH // 2))


def _na_bias_kernel(rpb_ref, t_ref):
    h = pl.program_id(0)
    shape = (GRID_W, LANES)
    w = lax.broadcasted_iota(jnp.int32, shape, 0)
    lane = lax.broadcasted_iota(jnp.int32, shape, 1)
    c = lane & (GRID_W - 1)
    first = lane < GRID_W
    cs = jnp.clip(c - NA_KW // 2, 0, GRID_W - NA_KW)
    inside = (w >= cs) & (w < cs + NA_KW)
    off = w - c + NA_KW - 1
    neg = jnp.full(shape, NEG_INF, F32)
    tiles = {(None, None): neg}
    for v, (delta, lo) in enumerate(NA_VARIANTS):
        for jk in range(NA_UROWS):
            for ip in range(NA_QROWS // 2):
                ro = []
                for i in (2 * ip, 2 * ip + 1):
                    ok = lo(i) <= jk < lo(i) + NA_KH
                    ro.append(jk - i + NA_KH - 1 + delta if ok else None)
                ro = tuple(ro)
                if ro not in tiles:
                    def body(kk, acc, ro=ro):
                        a = NEG_INF if ro[0] is None else rpb_ref[h, ro[0], kk]
                        b = NEG_INF if ro[1] is None else rpb_ref[h, ro[1], kk]
                        return jnp.where(off == kk, jnp.where(first, a, b), acc)

                    acc = lax.fori_loop(0, 2 * NA_KW - 1, body, neg)
                    tiles[ro] = jnp.where(inside, acc * LOG2_E, NEG_INF)
                t_ref[v, jk * GRID_W:(jk + 1) * GRID_W, ip * LANES:(ip + 1) * LANES] = tiles[ro]


def _na_bias(rpb_l):
    nv = len(NA_VARIANTS)
    return pl.pallas_call(
        _na_bias_kernel,
        grid=(NA_HEADS,),
        in_specs=[pl.BlockSpec(memory_space=pltpu.SMEM)],
        out_specs=pl.BlockSpec((nv, None, NA_UTOK, NA_QTOK), lambda h: (0, h, 0, 0)),
        out_shape=jax.ShapeDtypeStruct((nv, NA_HEADS, NA_UTOK, NA_QTOK), F32),
        compiler_params=_cparams(1),
        name="na_bias",
    )(rpb_l)


def _na_attn_kernel(q_ref, k_ref, vt_ref, t_ref, o_ref, s_ref, *, nblk):
    c0 = jnp.clip(pl.program_id(1) - 1, 0, nblk - NA_UROWS // NA_QROWS)
    k0 = pl.multiple_of(c0 * NA_QTOK, NA_QTOK)
    lane = lax.broadcasted_iota(jnp.int32, (NA_QTOK, LANES), 1)
    head_lanes = (lane < NA_HD, lane >= NA_HD)
    ones = jnp.ones((MLA_ONES_ROWS, NA_UTOK), BF16)

    def scores(h):
        cols = slice((h // 2) * LANES, (h // 2 + 1) * LANES)
        q = q_ref[:, cols]
        qm = jnp.where(head_lanes[h % 2], q, jnp.zeros_like(q))
        st = _nt_dot(k_ref[pl.ds(k0, NA_UTOK), cols], qm) + t_ref[h]
        s_ref[h % 2] = st
        return jnp.max(jnp.max(st.reshape(NA_UTOK // 8, 8, NA_QTOK), axis=0), axis=0, keepdims=True)

    def values(h, m):
        pt = jnp.exp2(s_ref[h % 2] - m).astype(BF16)
        vt = jnp.concatenate([vt_ref[c0 + c, h * NA_HD:(h + 1) * NA_HD, :] for c in range(NA_UROWS // NA_QROWS)],
                             axis=1)
        acc = jnp.dot(jnp.concatenate([vt, ones], axis=0), pt, preferred_element_type=F32)
        return acc[:NA_HD] / acc[NA_HD:NA_HD + 1]

    m_next = scores(0)
    o_prev = None
    for h in range(NA_HEADS):
        m = m_next
        if h + 1 < NA_HEADS:
            m_next = scores(h + 1)
        o_t = values(h, m)
        if h % 2 == 1:
            o_ref[:, (h - 1) * NA_HD:(h + 1) * NA_HD] = jnp.concatenate([o_prev, o_t], axis=0).T.astype(BF16)
        o_prev = o_t


def _na_attn(q, k, vt, table):
    b, s, _ = q.shape
    nblk = s // NA_QTOK
    last = nblk - 1
    variant = lambda bi, a: (jnp.where(a == 0, 0, jnp.where(a == last, 2, 1)), 0, 0, 0)
    return pl.pallas_call(
        functools.partial(_na_attn_kernel, nblk=nblk),
        grid=(b, nblk),
        in_specs=[pl.BlockSpec((None, NA_QTOK, NA_WIDTH), lambda bi, a: (bi, a, 0)),
                  pl.BlockSpec((None, s, NA_WIDTH), lambda bi, a: (bi, 0, 0)),
                  pl.BlockSpec((None, nblk, NA_WIDTH, NA_QTOK), lambda bi, a: (bi, 0, 0, 0)),
                  pl.BlockSpec((None, NA_HEADS, NA_UTOK, NA_QTOK), variant)],
        out_specs=pl.BlockSpec((None, NA_QTOK, NA_WIDTH), lambda bi, a: (bi, a, 0)),
        out_shape=jax.ShapeDtypeStruct((b, s, NA_WIDTH), BF16),
        scratch_shapes=[pltpu.VMEM((2, NA_UTOK, NA_QTOK), F32)],
        compiler_params=_cparams(2),
        name="na_attn",
    )(q, k, vt, table)


def _out_kernel(*refs, emit_h):
    (x_ref, oa_ref, gates_ref, oc_ref, merge_ref,
     ob0_ref, ob1_ref, ob2_ref, ls0_ref, ls1_ref, ls2_ref,
     wpa_ref, wpb_ref, wpc_ref, wout_ref, gpost_ref, mod_ref) = refs[:17]
    rest = refs[17:]
    if emit_h:
        gnext_ref, modn_ref, y_ref, h_ref, so_ref, sl_ref = rest
    else:
        y_ref, so_ref, sl_ref = rest
    tm = x_ref.shape[0]

    for gi, (ob_ref, ls_ref, (_, d)) in enumerate(zip((ob0_ref, ob1_ref, ob2_ref), (ls0_ref, ls1_ref, ls2_ref),
                                                     DIL_PAIRS)):
        for r in range(d):
            rows = slice(None) if d == 1 else pl.ds(r, tm // d, stride=d)
            for c in range(DIL_WIDTH // LANES):
                cols = slice(c * LANES, (c + 1) * LANES)
                so_ref[gi, c, rows, :] = ob_ref[r, :, cols].astype(F32)
                sl_ref[gi, c, rows, :] = ls_ref[r, :, cols]
    slabs = range(DIL_WIDTH // LANES)
    lse = [jnp.concatenate([sl_ref[gi, c] for c in slabs], axis=1) for gi in range(3)]
    o_g = [jnp.concatenate([so_ref[gi, c] for c in slabs], axis=1) for gi in range(3)]
    mx = jnp.maximum(jnp.maximum(lse[0], lse[1]), lse[2])
    e = [jnp.exp2(x - mx) for x in lse]
    o_b = (e[0] * o_g[0] + e[1] * o_g[1] + e[2] * o_g[2]) / (e[0] + e[1] + e[2])

    def gated(o, lo, hi):
        g = gates_ref[:, lo:hi].astype(F32)
        return (o * (g * (1.0 + jnp.tanh(g)))).astype(BF16)

    def merged(idx, act, w_ref):
        t = jnp.tanh(merge_ref[:, idx * D_MODEL:(idx + 1) * D_MODEL].astype(F32))
        return (1.0 + t) * jnp.dot(act, w_ref[...], preferred_element_type=F32)

    a = gated(oa_ref[...].astype(F32), 0, MLA_WIDTH)
    bb = gated(o_b, MLA_WIDTH, MLA_WIDTH + DIL_WIDTH)
    c = gated(oc_ref[...].astype(F32), MLA_WIDTH + DIL_WIDTH, GATE_WIDTH)
    mixed = merged(0, a, wpa_ref) + merged(1, bb, wpb_ref) + merged(2, c, wpc_ref)
    out = jnp.dot(mixed.astype(BF16), wout_ref[...], preferred_element_type=F32)
    normed = out * lax.rsqrt(jnp.mean(out * out, axis=-1, keepdims=True) + NORM_EPS) * gpost_ref[...]
    y = x_ref[...] + mod_ref[2:3, :] * normed
    y_ref[...] = y
    if emit_h:
        h_ref[...] = _modulated_norm(y, gnext_ref[...], modn_ref[...]).astype(BF16)


def _out_layer(x, oa, gates, oc, merge, obs, lses, wpa, wpb, wpc, wout, g_post, mod, g_next, mod_next):
    b, s, _ = x.shape
    tm = TM_OUT
    emit_h = g_next is not None
    tok = lambda n: pl.BlockSpec((None, tm, n), lambda bi, i: (bi, i, 0))
    const = lambda shape: pl.BlockSpec(shape, lambda bi, i: (0,) * len(shape))
    modspec = pl.BlockSpec((None, 3, D_MODEL), lambda bi, i: (bi, 0, 0))
    cls = [pl.BlockSpec((None, d, tm // d, DIL_WIDTH), lambda bi, i: (bi, 0, i, 0)) for _, d in DIL_PAIRS]
    in_specs = [tok(D_MODEL), tok(MLA_WIDTH), tok(GATE_WIDTH), tok(NA_WIDTH), tok(3 * D_MODEL),
                *cls, *cls,
                const((MLA_WIDTH, D_MODEL)), const((DIL_WIDTH, D_MODEL)), const((NA_WIDTH, D_MODEL)),
                const((D_MODEL, D_MODEL)), const((1, D_MODEL)), modspec]
    args = [x, oa, gates, oc, merge, *obs, *lses, wpa, wpb, wpc, wout, g_post.reshape(1, D_MODEL), mod]
    out_specs = [tok(D_MODEL)]
    out_shape = [jax.ShapeDtypeStruct((b, s, D_MODEL), F32)]
    if emit_h:
        in_specs += [const((1, D_MODEL)), modspec]
        args += [g_next.reshape(1, D_MODEL), mod_next]
        out_specs.append(tok(D_MODEL))
        out_shape.append(jax.ShapeDtypeStruct((b, s, D_MODEL), BF16))
    res = pl.pallas_call(
        functools.partial(_out_kernel, emit_h=emit_h),
        grid=(b, s // tm),
        in_specs=in_specs,
        out_specs=out_specs,
        out_shape=out_shape,
        scratch_shapes=[pltpu.VMEM((3, DIL_WIDTH // LANES, tm, LANES), F32)] * 2,
        compiler_params=_cparams(2),
        name="out_layer",
    )(*args)
    return (res[0], res[1]) if emit_h else (res[0], None)


def _rope_tables(s, head_dim, lane_dim, scale):
    half = head_dim // 2
    inv = ROPE_THETA ** (-jnp.arange(half, dtype=F32) * 2.0 / head_dim)
    ang = jnp.arange(s, dtype=F32)[:, None] * inv[None, :]
    cos, sin = jnp.cos(ang), jnp.sin(ang)
    lane_dim = np.asarray(lane_dim)
    in_rope = lane_dim >= 0
    idx = np.where(in_rope, lane_dim % half, 0)
    sign = np.where(lane_dim < half, -1.0, 1.0).astype(np.float32)
    cos_t = jnp.where(in_rope[None, :], cos[:, idx], 1.0) * scale
    sin_t = jnp.where(in_rope[None, :], sin[:, idx] * sign[None, :], 0.0) * scale
    return cos_t.astype(F32), sin_t.astype(F32)


_LANE = np.arange(LANES)
DIL_LANE_DIM = _LANE % (DIL_HD // 2) + (DIL_HD // 2) * (_LANE // (LANES // 2))
MLA_LANE_DIM = np.where(_LANE < MLA_ROPE // 2, _LANE,
                        np.where((_LANE >= LANES // 2) & (_LANE < LANES // 2 + MLA_ROPE // 2),
                                 _LANE - LANES // 2 + MLA_ROPE // 2, -1))


def _pair_rotary_order(w):
    lead = w.shape[:-1]
    w = w.reshape(*lead, DIL_HPG // 2, 2, 2, DIL_HD // 2)
    return jnp.swapaxes(w, -3, -2).reshape(*lead, DIL_WIDTH)


def _mla_head_slab(nope, rope):
    lead = (nope if nope is not None else rope).shape[:-1]
    dtype = (nope if nope is not None else rope).dtype
    z = lambda n: jnp.zeros(lead + (n,), dtype)
    r = MLA_ROPE // 2
    n0 = LANES // 2 - r
    parts = [z(r) if rope is None else rope[..., :r],
             z(n0) if nope is None else nope[..., :n0],
             z(r) if rope is None else rope[..., r:],
             z(MLA_NOPE - n0) if nope is None else nope[..., n0:],
             z(LANES - MLA_NOPE - MLA_ROPE)]
    return jnp.concatenate(parts, axis=-1)


def _layout_weights(w_in, w_uq, w_ukv):
    w_in, w_uq, w_ukv = w_in.astype(BF16), w_uq.astype(BF16), w_ukv.astype(BF16)
    c = [0] + [int(v) for v in _CUTS]
    cq, ckv, kr, gate_a, qkv_b, gate_b, qkv_c, gate_c, merge = [w_in[:, :, c[i]:c[i + 1]] for i in range(9)]
    w_a = jnp.concatenate([cq, ckv, _mla_head_slab(None, kr)], axis=-1)
    w_g = 0.5 * jnp.concatenate([gate_a, gate_b, gate_c], axis=-1)
    merge = 0.5 * merge
    qkv_b = qkv_b.reshape(DEPTH, D_MODEL, 3, len(DIL_PAIRS), DIL_WIDTH)
    qkv_b = qkv_b * jnp.array([DIL_HD ** -0.5, 1.0, 1.0], w_in.dtype)[None, None, :, None, None]
    qkv_b = jnp.concatenate([_pair_rotary_order(qkv_b[:, :, :2]), qkv_b[:, :, 2:]], axis=2)
    w_b = qkv_b.transpose(0, 1, 3, 2, 4).reshape(DEPTH, D_MODEL, ZB_WIDTH)
    qkv_c = qkv_c.reshape(DEPTH, D_MODEL, 3, NA_WIDTH)
    qkv_c = qkv_c * jnp.array([NA_HD ** -0.5, 1.0, 1.0], w_in.dtype)[None, None, :, None]
    w_cqk = qkv_c[:, :, :2].reshape(DEPTH, D_MODEL, 2 * NA_WIDTH)
    w_cvt = qkv_c[:, :, 2].transpose(0, 2, 1)
    uq = w_uq.reshape(DEPTH, MLA_Q_RANK, MLA_HEADS, MLA_DQK)
    w_q = _mla_head_slab(uq[..., :MLA_NOPE], uq[..., MLA_NOPE:]).reshape(DEPTH, MLA_Q_RANK, MLA_HEADS * MLA_HEAD_PAD)
    ukv = w_ukv.reshape(DEPTH, MLA_KV_RANK, MLA_HEADS, MLA_NOPE + MLA_V)
    w_k = _mla_head_slab(ukv[..., :MLA_NOPE], None).reshape(DEPTH, MLA_KV_RANK, MLA_HEADS * MLA_HEAD_PAD)
    w_vt = ukv[..., MLA_NOPE:].reshape(DEPTH, MLA_KV_RANK, MLA_WIDTH).transpose(0, 2, 1)
    bf = lambda w: w.astype(BF16)
    return dict(a=bf(w_a), g=bf(w_g), b=bf(w_b), cqk=bf(w_cqk), cvt=bf(w_cvt), m=bf(merge),
                q=bf(w_q), k=bf(w_k), vt=bf(w_vt))


def _trunk(x, c, p):
    b, s, _ = x.shape
    mods = _ada(c, p["w_ada"], p["b_ada"]).reshape(DEPTH, b, 3, D_MODEL)
    dil_tabs = _rope_tables(s, DIL_HD, DIL_LANE_DIM, 1.0)
    q_tabs = _rope_tables(s, MLA_ROPE, MLA_LANE_DIM, MLA_DQK ** -0.5 * LOG2_E)
    k_tabs = _rope_tables(s, MLA_ROPE, MLA_LANE_DIM, 1.0)
    w = p["w"]
    h = _prenorm(x, p["g_pre"][0], mods[0])
    for l in range(DEPTH):
        za = _proj(h, w["a"][l], "proj_mla")
        gates = _proj(h, w["g"][l], "proj_gates")
        qc, kc, vtc = _proj_na(h, w["cqk"][l], w["cvt"][l])
        merge = _proj(h, w["m"][l], "proj_merge")
        zbs = _proj_dil(h, w["b"][l], dil_tabs)
        q, k, vt = _mla_prep(za, p["g_q"][l], p["g_kv"][l], w["q"][l], w["k"][l], w["vt"][l], q_tabs, k_tabs)
        oa = _mla_attn(q, k, vt)
        dil = [_dil_attn(zb, d) for zb, (_, d) in zip(zbs, DIL_PAIRS)]
        oc = _na_attn(qc, kc, vtc, p["na_bias"][l])
        last = l == DEPTH - 1
        x, h = _out_layer(x, oa, gates, oc, merge, [o for o, _ in dil], [ls for _, ls in dil],
                          p["w_pa"][l], p["w_pb"][l], p["w_pc"][l], p["w_out"][l], p["g_post"][l], mods[l],
                          None if last else p["g_pre"][l + 1], None if last else mods[l + 1])
    return x


def _prepare(w_ada, b_ada, g_pre, g_post, w_in, g_q, w_uq, g_kv, w_ukv, rpb, w_pa, w_pb, w_pc, w_out):
    return dict(w_ada=w_ada.astype(BF16), b_ada=b_ada, g_pre=g_pre, g_post=g_post, g_q=g_q, g_kv=g_kv,
                w=_layout_weights(w_in, w_uq, w_ukv),
                na_bias=[_na_bias(rpb[l]) for l in range(DEPTH)],
                w_pa=(0.5 * w_pa).astype(BF16), w_pb=(0.5 * w_pb).astype(BF16), w_pc=(0.5 * w_pc).astype(BF16),
                w_out=w_out.astype(BF16))


def kernel(x_prompt, x_sample, c_prompt, c_sample, w_ada, b_ada, g_pre, g_post, w_in, g_q, w_uq, g_kv, w_ukv, rpb,
           w_pa, w_pb, w_pc, w_out):
    p = _prepare(w_ada, b_ada, g_pre, g_post, w_in, g_q, w_uq, g_kv, w_ukv, rpb, w_pa, w_pb, w_pc, w_out)
    return (_trunk(x_prompt, c_prompt, p), _trunk(x_sample, c_sample, p))
```

```python
import functools

import jax
import jax.numpy as jnp
import numpy as np
from jax import lax
from jax.experimental import pallas as pl
from jax.experimental.pallas import tpu as pltpu

F32 = jnp.float32
BF16 = jnp.bfloat16

D_MODEL = 1024
DEPTH = 4
GRID_W = 64
ROPE_THETA = 10000.0
NORM_EPS = 1e-6
NEG_INF = -1e30
LOG2_E = float(np.log2(np.e))

MLA_HEADS = 8
MLA_NOPE = 64
MLA_ROPE = 32
MLA_V = 64
MLA_DQK = MLA_NOPE + MLA_ROPE
MLA_Q_RANK = 384
MLA_KV_RANK = 256
MLA_WIDTH = MLA_HEADS * MLA_V
MLA_HEAD_PAD = 128

DIL_PAIRS = ((128, 1), (512, 4), (2048, 16))
DIL_HPG = 4
DIL_HD = 64
DIL_HEADS = 12
DIL_WIDTH = DIL_HPG * DIL_HD
DIL_BAND = 64

NA_HEADS = 8
NA_HD = 64
NA_KH = 8
NA_KW = 16
NA_WIDTH = NA_HEADS * NA_HD

LANES = 128
VMEM_LIMIT = 56 * 1024 * 1024

_CUTS = np.cumsum((MLA_Q_RANK, MLA_KV_RANK, MLA_ROPE, MLA_WIDTH, 3 * DIL_HEADS * DIL_HD, DIL_WIDTH,
                   3 * NA_HEADS * NA_HD, NA_WIDTH, 3 * D_MODEL))
ZA_WIDTH = MLA_Q_RANK + MLA_KV_RANK + LANES
GATE_WIDTH = MLA_WIDTH + DIL_WIDTH + NA_WIDTH
ZB_WIDTH = 3 * DIL_HEADS * DIL_HD
ZB_GROUP = 3 * DIL_WIDTH
ZC_WIDTH = 3 * NA_WIDTH

TM_IN = 512
TM_OUT = 256
MLA_TQ = 512
MLA_KC = 512
MLA_UNROLL = 4


def _cparams(n_grid):
    return pltpu.CompilerParams(dimension_semantics=("arbitrary",) * n_grid, vmem_limit_bytes=VMEM_LIMIT)


def _sigmoid(x):
    return 0.5 * (1.0 + jnp.tanh(0.5 * x))


def _rope_lanes(x, cos, sin):
    return x * cos + pltpu.roll(x, LANES // 2, 1) * sin


def _nt_dot(a, b):
    return lax.dot_general(a, b, (((1,), (1,)), ((), ())), preferred_element_type=F32)


def _ada_kernel(c_ref, w_ref, b_ref, o_ref):
    c = c_ref[...]
    c_act = (c * _sigmoid(c)).astype(BF16)
    o_ref[...] = jnp.dot(c_act, w_ref[...], preferred_element_type=F32) + b_ref[...]


def _ada(c, w_ada_bf, b_ada):
    b = c.shape[0]
    return pl.pallas_call(
        _ada_kernel,
        grid=(DEPTH,),
        in_specs=[pl.BlockSpec((b, D_MODEL), lambda l: (0, 0)),
                  pl.BlockSpec((None, D_MODEL, 3 * D_MODEL), lambda l: (l, 0, 0)),
                  pl.BlockSpec((None, 1, 3 * D_MODEL), lambda l: (l, 0, 0))],
        out_specs=pl.BlockSpec((None, b, 3 * D_MODEL), lambda l: (l, 0, 0)),
        out_shape=jax.ShapeDtypeStruct((DEPTH, b, 3 * D_MODEL), F32),
        compiler_params=_cparams(1),
        name="ada",
    )(c, w_ada_bf, b_ada.reshape(DEPTH, 1, 3 * D_MODEL))


def _modulated_norm(x32, g, mod):
    y = x32 * lax.rsqrt(jnp.mean(x32 * x32, axis=-1, keepdims=True) + NORM_EPS) * g
    return y * (1.0 + mod[1:2, :]) + mod[0:1, :]


def _prenorm_kernel(x_ref, g_ref, mod_ref, h_ref):
    h_ref[...] = _modulated_norm(x_ref[...], g_ref[...], mod_ref[...]).astype(BF16)


def _prenorm(x, g, mod):
    b, s, _ = x.shape
    tm = TM_IN
    return pl.pallas_call(
        _prenorm_kernel,
        grid=(b, s // tm),
        in_specs=[pl.BlockSpec((None, tm, D_MODEL), lambda bi, i: (bi, i, 0)),
                  pl.BlockSpec((1, D_MODEL), lambda bi, i: (0, 0)),
                  pl.BlockSpec((None, 3, D_MODEL), lambda bi, i: (bi, 0, 0))],
        out_specs=pl.BlockSpec((None, tm, D_MODEL), lambda bi, i: (bi, i, 0)),
        out_shape=jax.ShapeDtypeStruct((b, s, D_MODEL), BF16),
        compiler_params=_cparams(2),
        name="prenorm",
    )(x, g.reshape(1, D_MODEL), mod)


def _proj_kernel(h_ref, w_ref, o_ref):
    o_ref[...] = jnp.dot(h_ref[...], w_ref[...], preferred_element_type=F32).astype(BF16)


def _proj(h, w, name):
    b, s, _ = h.shape
    n = w.shape[1]
    tm = TM_IN
    return pl.pallas_call(
        _proj_kernel,
        grid=(b, s // tm),
        in_specs=[pl.BlockSpec((None, tm, D_MODEL), lambda bi, i: (bi, i, 0)),
                  pl.BlockSpec((D_MODEL, n), lambda bi, i: (0, 0))],
        out_specs=pl.BlockSpec((None, tm, n), lambda bi, i: (bi, i, 0)),
        out_shape=jax.ShapeDtypeStruct((b, s, n), BF16),
        compiler_params=_cparams(2),
        name=name,
    )(h, w)


def _proj_dil_kernel(h_ref, w_ref, cos_ref, sin_ref, o0_ref, o1_ref, o2_ref, z0_ref, z1_ref, z2_ref):
    tm = h_ref.shape[0]
    h = h_ref[...]
    cos, sin = cos_ref[...], sin_ref[...]
    slabs_per_part = DIL_WIDTH // LANES
    slabs_per_group = ZB_GROUP // LANES
    for gi, (o_ref, z_ref, (_, d)) in enumerate(zip((o0_ref, o1_ref, o2_ref), (z0_ref, z1_ref, z2_ref), DIL_PAIRS)):
        z = jnp.dot(h, w_ref[:, gi * ZB_GROUP:(gi + 1) * ZB_GROUP], preferred_element_type=F32)
        for c in range(slabs_per_group):
            x = z[:, c * LANES:(c + 1) * LANES]
            part = c // slabs_per_part
            if part != 2:
                x = _rope_lanes(x, cos, sin)
            if part == 0:
                x = x * LOG2_E
            z_ref[c] = x
        for r in range(d):
            rows = slice(None) if d == 1 else pl.ds(r, tm // d, stride=d)
            for c in range(slabs_per_group):
                o_ref[r, :, c * LANES:(c + 1) * LANES] = z_ref[c, rows, :].astype(BF16)


def _proj_dil(h, w, tabs):
    b, s, _ = h.shape
    tm = TM_IN
    cos, sin = tabs
    tab_spec = pl.BlockSpec((tm, LANES), lambda i, bi: (i, 0))
    out_shapes, out_specs = [], []
    for _, d in DIL_PAIRS:
        out_shapes.append(jax.ShapeDtypeStruct((b, d, s // d, ZB_GROUP), BF16))
        out_specs.append(pl.BlockSpec((None, d, tm // d, ZB_GROUP), lambda i, bi: (bi, 0, i, 0)))
    return pl.pallas_call(
        _proj_dil_kernel,
        grid=(s // tm, b),
        in_specs=[pl.BlockSpec((None, tm, D_MODEL), lambda i, bi: (bi, i, 0)),
                  pl.BlockSpec((D_MODEL, ZB_WIDTH), lambda i, bi: (0, 0)),
                  tab_spec, tab_spec],
        out_specs=out_specs,
        out_shape=out_shapes,
        scratch_shapes=[pltpu.VMEM((ZB_GROUP // LANES, tm, LANES), F32)] * 3,
        compiler_params=_cparams(2),
        name="proj_dil",
    )(h, w, cos, sin)


def _mla_prep_kernel(za_ref, gq_ref, gkv_ref, wqt_ref, wk_ref, wvt_ref,
                     cq_ref, sq_ref, ck_ref, sk_ref, q_ref, k_ref, vt_ref):
    def norm(x, g):
        x32 = x.astype(F32)
        return (x32 * lax.rsqrt(jnp.mean(x32 * x32, axis=-1, keepdims=True) + NORM_EPS) * g).astype(BF16)

    cqn = norm(za_ref[:, 0:MLA_Q_RANK], gq_ref[...])
    ckvn = norm(za_ref[:, MLA_Q_RANK:MLA_Q_RANK + MLA_KV_RANK], gkv_ref[...])
    kr = za_ref[:, MLA_Q_RANK + MLA_KV_RANK:ZA_WIDTH].astype(F32)
    kr = _rope_lanes(kr, ck_ref[...], sk_ref[...])
    qt = _nt_dot(wqt_ref[...], cqn)
    k = jnp.dot(ckvn, wk_ref[...], preferred_element_type=F32)
    cq, sq = cq_ref[...], sq_ref[...]
    vt = _nt_dot(wvt_ref[...], ckvn).astype(BF16)
    half = MLA_HEAD_PAD // 2
    for h in range(MLA_HEADS):
        cols = slice(h * MLA_HEAD_PAD, (h + 1) * MLA_HEAD_PAD)
        x = qt[cols]
        q_ref[h] = (x * cq + jnp.concatenate([x[half:], x[:half]], axis=0) * sq).astype(BF16)
        k_ref[h] = (k[:, cols] + kr).astype(BF16)
        vt_ref[h] = vt[h * MLA_V:(h + 1) * MLA_V, :]


def _mla_prep(za, g_q, g_kv, wqt, wk, wvt, qtabs_t, ktabs):
    b, s, _ = za.shape
    tm = MLA_KC
    hp = MLA_HEADS * MLA_HEAD_PAD
    tab_spec = pl.BlockSpec((tm, LANES), lambda i, bi: (i, 0))
    tab_t_spec = pl.BlockSpec((LANES, tm), lambda i, bi: (0, i))
    const = lambda shape: pl.BlockSpec(shape, lambda i, bi: (0,) * len(shape))
    return pl.pallas_call(
        _mla_prep_kernel,
        grid=(s // tm, b),
        in_specs=[pl.BlockSpec((None, tm, ZA_WIDTH), lambda i, bi: (bi, i, 0)),
                  const((1, MLA_Q_RANK)), const((1, MLA_KV_RANK)),
                  const((hp, MLA_Q_RANK)), const((MLA_KV_RANK, hp)), const((MLA_WIDTH, MLA_KV_RANK)),
                  tab_t_spec, tab_t_spec, tab_spec, tab_spec],
        out_specs=[pl.BlockSpec((None, MLA_HEADS, MLA_HEAD_PAD, tm), lambda i, bi: (bi, 0, 0, i)),
                   pl.BlockSpec((None, MLA_HEADS, tm, MLA_HEAD_PAD), lambda i, bi: (bi, 0, i, 0)),
                   pl.BlockSpec((None, None, MLA_HEADS, MLA_V, tm), lambda i, bi: (bi, i, 0, 0, 0))],
        out_shape=[jax.ShapeDtypeStruct((b, MLA_HEADS, MLA_HEAD_PAD, s), BF16),
                   jax.ShapeDtypeStruct((b, MLA_HEADS, s, MLA_HEAD_PAD), BF16),
                   jax.ShapeDtypeStruct((b, s // tm, MLA_HEADS, MLA_V, tm), BF16)],
        compiler_params=_cparams(2),
        name="mla_prep",
    )(za, g_q.reshape(1, -1), g_kv.reshape(1, -1), wqt, wk, wvt, *qtabs_t, *ktabs)


MLA_ONES_ROWS = 16


def _mla_attn_kernel(q_ref, k_ref, vt_ref, o_ref, s0_ref, s1_ref, ot_ref):
    tq = q_ref.shape[2]
    n_chunks, _, _, kc = vt_ref.shape
    ones = jnp.ones((MLA_ONES_ROWS, kc), BF16)
    acc_rows = MLA_V + MLA_ONES_ROWS
    s_refs = (s0_ref, s1_ref)

    def stage(h_score, h_value, m_value, parity):
        def body(j, carry):
            mx, acc = carry
            start = pl.multiple_of(j * kc, kc)
            if h_score is not None:
                st = jnp.dot(k_ref[h_score, pl.ds(start, kc), :], q_ref[h_score],
                             preferred_element_type=F32)
                s_refs[parity][pl.ds(start, kc), :] = st
                mx = jnp.maximum(mx, jnp.max(st.reshape(kc // 8, 8, tq), axis=0))
            if h_value is not None:
                p = jnp.exp2(s_refs[1 - parity][pl.ds(start, kc), :] - m_value).astype(BF16)
                vt = jnp.concatenate([vt_ref[j, h_value], ones], axis=0)
                acc = acc + jnp.dot(vt, p, preferred_element_type=F32)
            return mx, acc

        init = (jnp.full((8, tq), NEG_INF, F32), jnp.zeros((acc_rows, tq), F32))
        mx, acc = lax.fori_loop(0, n_chunks, body, init, unroll=MLA_UNROLL)
        if h_value is not None:
            ot_ref[h_value] = acc[:MLA_V] / acc[MLA_V:MLA_V + 1]
        return jnp.max(mx, axis=0, keepdims=True)

    def stage_pair(u, m):
        m = stage(2 * u + 1, 2 * u, m, 1)
        return stage(2 * u + 2, 2 * u + 1, m, 0)

    m = stage(0, None, None, 0)
    m = lax.fori_loop(0, MLA_HEADS // 2 - 1, stage_pair, m)
    m = stage(MLA_HEADS - 1, MLA_HEADS - 2, m, 1)
    stage(None, MLA_HEADS - 1, m, 0)
    for pr in range(MLA_HEADS // 2):
        pair = jnp.concatenate([ot_ref[2 * pr], ot_ref[2 * pr + 1]], axis=0)
        o_ref[:, 2 * pr * MLA_V:(2 * pr + 2) * MLA_V] = pair.T.astype(BF16)


def _mla_attn(qt, k, vt):
    b, _, s, _ = k.shape
    n_chunks, kc = vt.shape[1], vt.shape[4]
    tq = min(MLA_TQ, s)
    return pl.pallas_call(
        _mla_attn_kernel,
        grid=(b, s // tq),
        in_specs=[pl.BlockSpec((None, MLA_HEADS, MLA_HEAD_PAD, tq), lambda bi, i: (bi, 0, 0, i)),
                  pl.BlockSpec((None, MLA_HEADS, s, MLA_HEAD_PAD), lambda bi, i: (bi, 0, 0, 0)),
                  pl.BlockSpec((None, n_chunks, MLA_HEADS, MLA_V, kc), lambda bi, i: (bi, 0, 0, 0, 0))],
        out_specs=pl.BlockSpec((None, tq, MLA_WIDTH), lambda bi, i: (bi, i, 0)),
        out_shape=jax.ShapeDtypeStruct((b, s, MLA_WIDTH), BF16),
        scratch_shapes=[pltpu.VMEM((s, tq), F32), pltpu.VMEM((s, tq), F32), pltpu.VMEM((MLA_HEADS, MLA_V, tq), F32)],
        compiler_params=_cparams(2),
        name="mla_attn",
    )(qt, k, vt)


def _dil_attn_kernel(qkv_ref, o_ref, lse_ref, s_ref, *, tq, win):
    d, length, _ = qkv_ref.shape
    tiles = length // tq
    n_total = d * tiles
    heads = range(DIL_HPG)
    key_rel = lax.broadcasted_iota(jnp.int32, (tq, win), 1) - lax.broadcasted_iota(jnp.int32, (tq, win), 0)
    lane = lax.broadcasted_iota(jnp.int32, (tq, LANES), 1)
    low_half = lane < DIL_HD
    qk_head = (lane // (DIL_HD // 2)) % 2
    head_lanes = (qk_head == 0, qk_head == 1)

    def coords(n):
        r = n // tiles
        q0 = pl.multiple_of((n - r * tiles) * tq, tq)
        start = pl.multiple_of(jnp.clip(q0 - DIL_BAND, 0, length - win), DIL_BAND)
        return r, q0, start

    def score_step(n):
        r, q0, start = coords(n)
        valid = jnp.abs(key_rel + (start - q0)) <= DIL_BAND
        ms = []
        for h in heads:
            cols = slice((h // 2) * LANES, (h // 2 + 1) * LANES)
            q = qkv_ref[r, pl.ds(q0, tq), cols]
            k = qkv_ref[r, pl.ds(start, win), DIL_WIDTH + cols.start:DIL_WIDTH + cols.stop]
            qm = jnp.where(head_lanes[h % 2], q, jnp.zeros_like(q))
            sc = jnp.where(valid, _nt_dot(qm, k), NEG_INF)
            s_ref[n % 2, h] = sc
            ms.append(jnp.max(sc, axis=-1, keepdims=True))
        return tuple(ms)

    def value_step(n, ms):
        r, q0, start = coords(n)
        res = []
        for h in heads:
            cols = slice(2 * DIL_WIDTH + (h // 2) * LANES, 2 * DIL_WIDTH + (h // 2 + 1) * LANES)
            p = jnp.exp2(s_ref[n % 2, h] - ms[h])
            l = jnp.sum(p, axis=-1, keepdims=True)
            o = jnp.dot(p.astype(BF16), qkv_ref[r, pl.ds(start, win), cols], preferred_element_type=F32) / l
            res.append((o, ms[h] + jnp.log2(l)))
        for pr in range(DIL_HPG // 2):
            cols = slice(pr * LANES, (pr + 1) * LANES)
            o_ref[r, pl.ds(q0, tq), cols] = jnp.where(low_half, res[2 * pr][0], res[2 * pr + 1][0]).astype(BF16)
            lse_ref[r, pl.ds(q0, tq), cols] = jnp.where(low_half, res[2 * pr][1], res[2 * pr + 1][1])

    def body(n, ms):
        ms_next = score_step(jnp.minimum(n + 1, n_total - 1))
        value_step(n, ms)
        return ms_next

    lax.fori_loop(0, n_total, body, score_step(0), unroll=4)


def _dil_attn(zb, d):
    b, _, length, _ = zb.shape
    tq = min(2 * DIL_BAND, length)
    win = min(4 * DIL_BAND, length)
    whole = lambda width: pl.BlockSpec((None, d, length, width), lambda bi: (bi, 0, 0, 0))
    return pl.pallas_call(
        functools.partial(_dil_attn_kernel, tq=tq, win=win),
        grid=(b,),
        in_specs=[whole(ZB_GROUP)],
        out_specs=[whole(DIL_WIDTH), whole(DIL_WIDTH)],
        out_shape=[jax.ShapeDtypeStruct((b, d, length, DIL_WIDTH), BF16),
                   jax.ShapeDtypeStruct((b, d, length, DIL_WIDTH), F32)],
        scratch_shapes=[pltpu.VMEM((2, DIL_HPG, tq, win), F32)],
        compiler_params=_cparams(1),
        name=f"dil_attn_d{d}",
    )(zb)


NA_QROWS = 4
NA_UROWS = NA_QROWS + NA_KH
NA_QTOK = NA_QROWS * GRID_W
NA_UTOK = NA_UROWS * GRID_W


def _proj_na_kernel(h_ref, wqk_ref, wvt_ref, q_ref, k_ref, vt_ref):
    h = h_ref[...]
    z = jnp.dot(h, wqk_ref[...], preferred_element_type=F32)
    q_ref[...] = (z[:, :NA_WIDTH] * LOG2_E).astype(BF16)
    k_ref[...] = z[:, NA_WIDTH:].astype(BF16)
    vt = _nt_dot(wvt_ref[...], h)
    for c in range(vt_ref.shape[0]):
        vt_ref[c] = vt[:, c * NA_QTOK:(c + 1) * NA_QTOK].astype(BF16)


def _proj_na(h, wqk, wvt):
    b, s, _ = h.shape
    tm = TM_IN
    tok = pl.BlockSpec((None, tm, NA_WIDTH), lambda bi, i: (bi, i, 0))
    return pl.pallas_call(
        _proj_na_kernel,
        grid=(b, s // tm),
        in_specs=[pl.BlockSpec((None, tm, D_MODEL), lambda bi, i: (bi, i, 0)),
                  pl.BlockSpec((D_MODEL, 2 * NA_WIDTH), lambda bi, i: (0, 0)),
                  pl.BlockSpec((NA_WIDTH, D_MODEL), lambda bi, i: (0, 0))],
        out_specs=[tok, tok, pl.BlockSpec((None, tm // NA_QTOK, NA_WIDTH, NA_QTOK), lambda bi, i: (bi, i, 0, 0))],
        out_shape=[jax.ShapeDtypeStruct((b, s, NA_WIDTH), BF16), jax.ShapeDtypeStruct((b, s, NA_WIDTH), BF16),
                   jax.ShapeDtypeStruct((b, s // NA_QTOK, NA_WIDTH, NA_QTOK), BF16)],
        compiler_params=_cparams(2),
        name="proj_na",
    )(h, wqk, wvt)


NA_VARIANTS = ((0, lambda i: 0), (-NA_KH // 2, lambda i: i), (-NA_KH, lambda i: NA_KH // 2))


def _na_bias_kernel(rpb_ref, t_ref):
    h = pl.program_id(0)
    shape = (GRID_W, LANES)
    w = lax.broadcasted_iota(jnp.int32, shape, 0)
    lane = lax.broadcasted_iota(jnp.int32, shape, 1)
    c = lane & (GRID_W - 1)
    first = lane < GRID_W
    cs = jnp.clip(c - NA_KW // 2, 0, GRID_W - NA_KW)
    inside = (w >= cs) & (w < cs + NA_KW)
    off = w - c + NA_KW - 1
    neg = jnp.full(shape, NEG_INF, F32)
    tiles = {(None, None): neg}
    for v, (delta, lo) in enumerate(NA_VARIANTS):
        for jk in range(NA_UROWS):
            for ip in range(NA_QROWS // 2):
                ro = []
                for i in (2 * ip, 2 * ip + 1):
                    ok = lo(i) <= jk < lo(i) + NA_KH
                    ro.append(jk - i + NA_KH - 1 + delta if ok else None)
                ro = tuple(ro)
                if ro not in tiles:
                    def body(kk, acc, ro=ro):
                        a = NEG_INF if ro[0] is None else rpb_ref[h, ro[0], kk]
                        b = NEG_INF if ro[1] is None else rpb_ref[h, ro[1], kk]
                        return jnp.where(off == kk, jnp.where(first, a, b), acc)

                    acc = lax.fori_loop(0, 2 * NA_KW - 1, body, neg)
                    tiles[ro] = jnp.where(inside, acc * LOG2_E, NEG_INF)
                t_ref[v, jk * GRID_W:(jk + 1) * GRID_W, ip * LANES:(ip + 1) * LANES] = tiles[ro]


def _na_bias(rpb_l):
    nv = len(NA_VARIANTS)
    return pl.pallas_call(
        _na_bias_kernel,
        grid=(NA_HEADS,),
        in_specs=[pl.BlockSpec(memory_space=pltpu.SMEM)],
        out_specs=pl.BlockSpec((nv, None, NA_UTOK, NA_QTOK), lambda h: (0, h, 0, 0)),
        out_shape=jax.ShapeDtypeStruct((nv, NA_HEADS, NA_UTOK, NA_QTOK), F32),
        compiler_params=_cparams(1),
        name="na_bias",
    )(rpb_l)


def _na_attn_kernel(q_ref, k_ref, vt_ref, t_ref, o_ref, s_ref, *, nblk):
    c0 = jnp.clip(pl.program_id(1) - 1, 0, nblk - NA_UROWS // NA_QROWS)
    k0 = pl.multiple_of(c0 * NA_QTOK, NA_QTOK)
    lane = lax.broadcasted_iota(jnp.int32, (NA_QTOK, LANES), 1)
    head_lanes = (lane < NA_HD, lane >= NA_HD)
    ones = jnp.ones((MLA_ONES_ROWS, NA_UTOK), BF16)

    def scores(h):
        cols = slice((h // 2) * LANES, (h // 2 + 1) * LANES)
        q = q_ref[:, cols]
        qm = jnp.where(head_lanes[h % 2], q, jnp.zeros_like(q))
        st = _nt_dot(k_ref[pl.ds(k0, NA_UTOK), cols], qm) + t_ref[h]
        s_ref[h % 2] = st
        return jnp.max(jnp.max(st.reshape(NA_UTOK // 8, 8, NA_QTOK), axis=0), axis=0, keepdims=True)

    def values(h, m):
        pt = jnp.exp2(s_ref[h % 2] - m).astype(BF16)
        vt = jnp.concatenate([vt_ref[c0 + c, h * NA_HD:(h + 1) * NA_HD, :] for c in range(NA_UROWS // NA_QROWS)],
                             axis=1)
        acc = jnp.dot(jnp.concatenate([vt, ones], axis=0), pt, preferred_element_type=F32)
        return acc[:NA_HD] / acc[NA_HD:NA_HD + 1]

    m_next = scores(0)
    o_prev = None
    for h in range(NA_HEADS):
        m = m_next
        if h + 1 < NA_HEADS:
            m_next = scores(h + 1)
        o_t = values(h, m)
        if h % 2 == 1:
            o_ref[:, (h - 1) * NA_HD:(h + 1) * NA_HD] = jnp.concatenate([o_prev, o_t], axis=0).T.astype(BF16)
        o_prev = o_t


def _na_attn(q, k, vt, table):
    b, s, _ = q.shape
    nblk = s // NA_QTOK
    last = nblk - 1
    variant = lambda bi, a: (jnp.where(a == 0, 0, jnp.where(a == last, 2, 1)), 0, 0, 0)
    return pl.pallas_call(
        functools.partial(_na_attn_kernel, nblk=nblk),
        grid=(b, nblk),
        in_specs=[pl.BlockSpec((None, NA_QTOK, NA_WIDTH), lambda bi, a: (bi, a, 0)),
                  pl.BlockSpec((None, s, NA_WIDTH), lambda bi, a: (bi, 0, 0)),
                  pl.BlockSpec((None, nblk, NA_WIDTH, NA_QTOK), lambda bi, a: (bi, 0, 0, 0)),
                  pl.BlockSpec((None, NA_HEADS, NA_UTOK, NA_QTOK), variant)],
        out_specs=pl.BlockSpec((None, NA_QTOK, NA_WIDTH), lambda bi, a: (bi, a, 0)),
        out_shape=jax.ShapeDtypeStruct((b, s, NA_WIDTH), BF16),
        scratch_shapes=[pltpu.VMEM((2, NA_UTOK, NA_QTOK), F32)],
        compiler_params=_cparams(2),
        name="na_attn",
    )(q, k, vt, table)


def _out_kernel(*refs, emit_h):
    (x_ref, oa_ref, gates_ref, oc_ref, merge_ref,
     ob0_ref, ob1_ref, ob2_ref, ls0_ref, ls1_ref, ls2_ref,
     wpa_ref, wpb_ref, wpc_ref, wout_ref, gpost_ref, mod_ref) = refs[:17]
    rest = refs[17:]
    if emit_h:
        gnext_ref, modn_ref, y_ref, h_ref, so_ref, sl_ref = rest
    else:
        y_ref, so_ref, sl_ref = rest
    tm = x_ref.shape[0]

    for gi, (ob_ref, ls_ref, (_, d)) in enumerate(zip((ob0_ref, ob1_ref, ob2_ref), (ls0_ref, ls1_ref, ls2_ref),
                                                     DIL_PAIRS)):
        for r in range(d):
            rows = slice(None) if d == 1 else pl.ds(r, tm // d, stride=d)
            for c in range(DIL_WIDTH // LANES):
                cols = slice(c * LANES, (c + 1) * LANES)
                so_ref[gi, c, rows, :] = ob_ref[r, :, cols].astype(F32)
                sl_ref[gi, c, rows, :] = ls_ref[r, :, cols]
    slabs = range(DIL_WIDTH // LANES)
    lse = [jnp.concatenate([sl_ref[gi, c] for c in slabs], axis=1) for gi in range(3)]
    o_g = [jnp.concatenate([so_ref[gi, c] for c in slabs], axis=1) for gi in range(3)]
    mx = jnp.maximum(jnp.maximum(lse[0], lse[1]), lse[2])
    e = [jnp.exp2(x - mx) for x in lse]
    o_b = (e[0] * o_g[0] + e[1] * o_g[1] + e[2] * o_g[2]) / (e[0] + e[1] + e[2])

    def gated(o, lo, hi):
        g = gates_ref[:, lo:hi].astype(F32)
        return (o * (g * (1.0 + jnp.tanh(g)))).astype(BF16)

    def merged(idx, act, w_ref):
        t = jnp.tanh(merge_ref[:, idx * D_MODEL:(idx + 1) * D_MODEL].astype(F32))
        return (1.0 + t) * jnp.dot(act, w_ref[...], preferred_element_type=F32)

    a = gated(oa_ref[...].astype(F32), 0, MLA_WIDTH)
    bb = gated(o_b, MLA_WIDTH, MLA_WIDTH + DIL_WIDTH)
    c = gated(oc_ref[...].astype(F32), MLA_WIDTH + DIL_WIDTH, GATE_WIDTH)
    mixed = merged(0, a, wpa_ref) + merged(1, bb, wpb_ref) + merged(2, c, wpc_ref)
    out = jnp.dot(mixed.astype(BF16), wout_ref[...], preferred_element_type=F32)
    normed = out * lax.rsqrt(jnp.mean(out * out, axis=-1, keepdims=True) + NORM_EPS) * gpost_ref[...]
    y = x_ref[...] + mod_ref[2:3, :] * normed
    y_ref[...] = y
    if emit_h:
        h_ref[...] = _modulated_norm(y, gnext_ref[...], modn_ref[...]).astype(BF16)


def _out_layer(x, oa, gates, oc, merge, obs, lses, wpa, wpb, wpc, wout, g_post, mod, g_next, mod_next):
    b, s, _ = x.shape
    tm = TM_OUT
    emit_h = g_next is not None
    tok = lambda n: pl.BlockSpec((None, tm, n), lambda bi, i: (bi, i, 0))
    const = lambda shape: pl.BlockSpec(shape, lambda bi, i: (0,) * len(shape))
    modspec = pl.BlockSpec((None, 3, D_MODEL), lambda bi, i: (bi, 0, 0))
    cls = [pl.BlockSpec((None, d, tm // d, DIL_WIDTH), lambda bi, i: (bi, 0, i, 0)) for _, d in DIL_PAIRS]
    in_specs = [tok(D_MODEL), tok(MLA_WIDTH), tok(GATE_WIDTH), tok(NA_WIDTH), tok(3 * D_MODEL),
                *cls, *cls,
                const((MLA_WIDTH, D_MODEL)), const((DIL_WIDTH, D_MODEL)), const((NA_WIDTH, D_MODEL)),
                const((D_MODEL, D_MODEL)), const((1, D_MODEL)), modspec]
    args = [x, oa, gates, oc, merge, *obs, *lses, wpa, wpb, wpc, wout, g_post.reshape(1, D_MODEL), mod]
    out_specs = [tok(D_MODEL)]
    out_shape = [jax.ShapeDtypeStruct((b, s, D_MODEL), F32)]
    if emit_h:
        in_specs += [const((1, D_MODEL)), modspec]
        args += [g_next.reshape(1, D_MODEL), mod_next]
        out_specs.append(tok(D_MODEL))
        out_shape.append(jax.ShapeDtypeStruct((b, s, D_MODEL), BF16))
    res = pl.pallas_call(
        functools.partial(_out_kernel, emit_h=emit_h),
        grid=(b, s // tm),
        in_specs=in_specs,
        out_specs=out_specs,
        out_shape=out_shape,
        scratch_shapes=[pltpu.VMEM((3, DIL_WIDTH // LANES, tm, LANES), F32)] * 2,
        compiler_params=_cparams(2),
        name="out_layer",
    )(*args)
    return (res[0], res[1]) if emit_h else (res[0], None)


def _rope_tables(s, head_dim, lane_dim, scale):
    half = head_dim // 2
    inv = ROPE_THETA ** (-jnp.arange(half, dtype=F32) * 2.0 / head_dim)
    ang = jnp.arange(s, dtype=F32)[:, None] * inv[None, :]
    cos, sin = jnp.cos(ang), jnp.sin(ang)
    lane_dim = np.asarray(lane_dim)
    in_rope = lane_dim >= 0
    idx = np.where(in_rope, lane_dim % half, 0)
    sign = np.where(lane_dim < half, -1.0, 1.0).astype(np.float32)
    cos_t = jnp.where(in_rope[None, :], cos[:, idx], 1.0) * scale
    sin_t = jnp.where(in_rope[None, :], sin[:, idx] * sign[None, :], 0.0) * scale
    return cos_t.astype(F32), sin_t.astype(F32)


_LANE = np.arange(LANES)
DIL_LANE_DIM = _LANE % (DIL_HD // 2) + (DIL_HD // 2) * (_LANE // (LANES // 2))
MLA_LANE_DIM = np.where(_LANE < MLA_ROPE // 2, _LANE,
                        np.where((_LANE >= LANES // 2) & (_LANE < LANES // 2 + MLA_ROPE // 2),
                                 _LANE - LANES // 2 + MLA_ROPE // 2, -1))


def _pair_rotary_order(w):
    lead = w.shape[:-1]
    w = w.reshape(*lead, DIL_HPG // 2, 2, 2, DIL_HD // 2)
    return jnp.swapaxes(w, -3, -2).reshape(*lead, DIL_WIDTH)


def _mla_head_slab(nope, rope):
    lead = (nope if nope is not None else rope).shape[:-1]
    dtype = (nope if nope is not None else rope).dtype
    z = lambda n: jnp.zeros(lead + (n,), dtype)
    r = MLA_ROPE // 2
    n0 = LANES // 2 - r
    parts = [z(r) if rope is None else rope[..., :r],
             z(n0) if nope is None else nope[..., :n0],
             z(r) if rope is None else rope[..., r:],
             z(MLA_NOPE - n0) if nope is None else nope[..., n0:],
             z(LANES - MLA_NOPE - MLA_ROPE)]
    return jnp.concatenate(parts, axis=-1)


def _layout_weights(w_in, w_uq, w_ukv):
    w_in, w_uq, w_ukv = w_in.astype(BF16), w_uq.astype(BF16), w_ukv.astype(BF16)
    c = [0] + [int(v) for v in _CUTS]
    cq, ckv, kr, gate_a, qkv_b, gate_b, qkv_c, gate_c, merge = [w_in[:, :, c[i]:c[i + 1]] for i in range(9)]
    w_a = jnp.concatenate([cq, ckv, _mla_head_slab(None, kr)], axis=-1)
    w_g = 0.5 * jnp.concatenate([gate_a, gate_b, gate_c], axis=-1)
    merge = 0.5 * merge
    qkv_b = qkv_b.reshape(DEPTH, D_MODEL, 3, len(DIL_PAIRS), DIL_WIDTH)
    qkv_b = qkv_b * jnp.array([DIL_HD ** -0.5, 1.0, 1.0], w_in.dtype)[None, None, :, None, None]
    qkv_b = jnp.concatenate([_pair_rotary_order(qkv_b[:, :, :2]), qkv_b[:, :, 2:]], axis=2)
    w_b = qkv_b.transpose(0, 1, 3, 2, 4).reshape(DEPTH, D_MODEL, ZB_WIDTH)
    qkv_c = qkv_c.reshape(DEPTH, D_MODEL, 3, NA_WIDTH)
    qkv_c = qkv_c * jnp.array([NA_HD ** -0.5, 1.0, 1.0], w_in.dtype)[None, None, :, None]
    w_cqk = qkv_c[:, :, :2].reshape(DEPTH, D_MODEL, 2 * NA_WIDTH)
    w_cvt = qkv_c[:, :, 2].transpose(0, 2, 1)
    uq = w_uq.reshape(DEPTH, MLA_Q_RANK, MLA_HEADS, MLA_DQK)
    w_q = _mla_head_slab(uq[..., :MLA_NOPE], uq[..., MLA_NOPE:]).reshape(DEPTH, MLA_Q_RANK, MLA_HEADS * MLA_HEAD_PAD)
    ukv = w_ukv.reshape(DEPTH, MLA_KV_RANK, MLA_HEADS, MLA_NOPE + MLA_V)
    w_k = _mla_head_slab(ukv[..., :MLA_NOPE], None).reshape(DEPTH, MLA_KV_RANK, MLA_HEADS * MLA_HEAD_PAD)
    w_vt = ukv[..., MLA_NOPE:].reshape(DEPTH, MLA_KV_RANK, MLA_WIDTH).transpose(0, 2, 1)
    bf = lambda w: w.astype(BF16)
    return dict(a=bf(w_a), g=bf(w_g), b=bf(w_b), cqk=bf(w_cqk), cvt=bf(w_cvt), m=bf(merge),
                qt=bf(w_q).transpose(0, 2, 1), k=bf(w_k), vt=bf(w_vt))


def _trunk(x, c, p):
    b, s, _ = x.shape
    mods = _ada(c, p["w_ada"], p["b_ada"]).reshape(DEPTH, b, 3, D_MODEL)
    dil_tabs = _rope_tables(s, DIL_HD, DIL_LANE_DIM, 1.0)
    q_tabs_t = tuple(t.T for t in _rope_tables(s, MLA_ROPE, MLA_LANE_DIM, MLA_DQK ** -0.5 * LOG2_E))
    k_tabs = _rope_tables(s, MLA_ROPE, MLA_LANE_DIM, 1.0)
    w = p["w"]
    h = _prenorm(x, p["g_pre"][0], mods[0])
    for l in range(DEPTH):
        za = _proj(h, w["a"][l], "proj_mla")
        gates = _proj(h, w["g"][l], "proj_gates")
        qc, kc, vtc = _proj_na(h, w["cqk"][l], w["cvt"][l])
        merge = _proj(h, w["m"][l], "proj_merge")
        zbs = _proj_dil(h, w["b"][l], dil_tabs)
        q, k, vt = _mla_prep(za, p["g_q"][l], p["g_kv"][l], w["qt"][l], w["k"][l], w["vt"][l], q_tabs_t, k_tabs)
        oa = _mla_attn(q, k, vt)
        dil = [_dil_attn(zb, d) for zb, (_, d) in zip(zbs, DIL_PAIRS)]
        oc = _na_attn(qc, kc, vtc, p["na_bias"][l])
        last = l == DEPTH - 1
        x, h = _out_layer(x, oa, gates, oc, merge, [o for o, _ in dil], [ls for _, ls in dil],
                          p["w_pa"][l], p["w_pb"][l], p["w_pc"][l], p["w_out"][l], p["g_post"][l], mods[l],
                          None if last else p["g_pre"][l + 1], None if last else mods[l + 1])
    return x


def _prepare(w_ada, b_ada, g_pre, g_post, w_in, g_q, w_uq, g_kv, w_ukv, rpb, w_pa, w_pb, w_pc, w_out):
    return dict(w_ada=w_ada.astype(BF16), b_ada=b_ada, g_pre=g_pre, g_post=g_post, g_q=g_q, g_kv=g_kv,
                w=_layout_weights(w_in, w_uq, w_ukv),
                na_bias=[_na_bias(rpb[l]) for l in range(DEPTH)],
                w_pa=(0.5 * w_pa).astype(BF16), w_pb=(0.5 * w_pb).astype(BF16), w_pc=(0.5 * w_pc).astype(BF16),
                w_out=w_out.astype(BF16))


def kernel(x_prompt, x_sample, c_prompt, c_sample, w_ada, b_ada, g_pre, g_post, w_in, g_q, w_uq, g_kv, w_ukv, rpb,
           w_pa, w_pb, w_pc, w_out):
    p = _prepare(w_ada, b_ada, g_pre, g_post, w_in, g_q, w_uq, g_kv, w_ukv, rpb, w_pa, w_pb, w_pc, w_out)
    return (_trunk(x_prompt, c_prompt, p), _trunk(x_sample, c_sample, p))
```

```python
import functools

import jax
import jax.numpy as jnp
import numpy as np
from jax import lax
from jax.experimental import pallas as pl
from jax.experimental.pallas import tpu as pltpu

F32 = jnp.float32
BF16 = jnp.bfloat16

D_MODEL = 1024
DEPTH = 4
GRID_W = 64
ROPE_THETA = 10000.0
NORM_EPS = 1e-6
NEG_INF = -1e30
LOG2_E = float(np.log2(np.e))

MLA_HEADS = 8
MLA_NOPE = 64
MLA_ROPE = 32
MLA_V = 64
MLA_DQK = MLA_NOPE + MLA_ROPE
MLA_Q_RANK = 384
MLA_KV_RANK = 256
MLA_WIDTH = MLA_HEADS * MLA_V
MLA_HEAD_PAD = 128

DIL_PAIRS = ((128, 1), (512, 4), (2048, 16))
DIL_HPG = 4
DIL_HD = 64
DIL_HEADS = 12
DIL_WIDTH = DIL_HPG * DIL_HD
DIL_BAND = 64

NA_HEADS = 8
NA_HD = 64
NA_KH = 8
NA_KW = 16
NA_WIDTH = NA_HEADS * NA_HD

LANES = 128
VMEM_LIMIT = 56 * 1024 * 1024

_CUTS = np.cumsum((MLA_Q_RANK, MLA_KV_RANK, MLA_ROPE, MLA_WIDTH, 3 * DIL_HEADS * DIL_HD, DIL_WIDTH,
                   3 * NA_HEADS * NA_HD, NA_WIDTH, 3 * D_MODEL))
ZA_WIDTH = MLA_Q_RANK + MLA_KV_RANK + LANES
GATE_WIDTH = MLA_WIDTH + DIL_WIDTH + NA_WIDTH
ZB_WIDTH = 3 * DIL_HEADS * DIL_HD
ZB_GROUP = 3 * DIL_WIDTH
ZC_WIDTH = 3 * NA_WIDTH

TM_IN = 512
TM_OUT = 256
MLA_TQ = 512
MLA_KC = 512
MLA_UNROLL = 8


def _cparams(n_grid):
    return pltpu.CompilerParams(dimension_semantics=("arbitrary",) * n_grid, vmem_limit_bytes=VMEM_LIMIT)


def _sigmoid(x):
    return 0.5 * (1.0 + jnp.tanh(0.5 * x))


def _rope_lanes(x, cos, sin):
    return x * cos + pltpu.roll(x, LANES // 2, 1) * sin


def _nt_dot(a, b):
    return lax.dot_general(a, b, (((1,), (1,)), ((), ())), preferred_element_type=F32)


def _ada_kernel(c_ref, w_ref, b_ref, o_ref):
    c = c_ref[...]
    c_act = (c * _sigmoid(c)).astype(BF16)
    o_ref[...] = jnp.dot(c_act, w_ref[...], preferred_element_type=F32) + b_ref[...]


def _ada(c, w_ada_bf, b_ada):
    b = c.shape[0]
    return pl.pallas_call(
        _ada_kernel,
        grid=(DEPTH,),
        in_specs=[pl.BlockSpec((b, D_MODEL), lambda l: (0, 0)),
                  pl.BlockSpec((None, D_MODEL, 3 * D_MODEL), lambda l: (l, 0, 0)),
                  pl.BlockSpec((None, 1, 3 * D_MODEL), lambda l: (l, 0, 0))],
        out_specs=pl.BlockSpec((None, b, 3 * D_MODEL), lambda l: (l, 0, 0)),
        out_shape=jax.ShapeDtypeStruct((DEPTH, b, 3 * D_MODEL), F32),
        compiler_params=_cparams(1),
        name="ada",
    )(c, w_ada_bf, b_ada.reshape(DEPTH, 1, 3 * D_MODEL))


def _modulated_norm(x32, g, mod):
    y = x32 * lax.rsqrt(jnp.mean(x32 * x32, axis=-1, keepdims=True) + NORM_EPS) * g
    return y * (1.0 + mod[1:2, :]) + mod[0:1, :]


def _prenorm_kernel(x_ref, g_ref, mod_ref, h_ref):
    h_ref[...] = _modulated_norm(x_ref[...], g_ref[...], mod_ref[...]).astype(BF16)


def _prenorm(x, g, mod):
    b, s, _ = x.shape
    tm = TM_IN
    return pl.pallas_call(
        _prenorm_kernel,
        grid=(b, s // tm),
        in_specs=[pl.BlockSpec((None, tm, D_MODEL), lambda bi, i: (bi, i, 0)),
                  pl.BlockSpec((1, D_MODEL), lambda bi, i: (0, 0)),
                  pl.BlockSpec((None, 3, D_MODEL), lambda bi, i: (bi, 0, 0))],
        out_specs=pl.BlockSpec((None, tm, D_MODEL), lambda bi, i: (bi, i, 0)),
        out_shape=jax.ShapeDtypeStruct((b, s, D_MODEL), BF16),
        compiler_params=_cparams(2),
        name="prenorm",
    )(x, g.reshape(1, D_MODEL), mod)


def _proj_kernel(h_ref, w_ref, o_ref):
    o_ref[...] = jnp.dot(h_ref[...], w_ref[...], preferred_element_type=F32).astype(BF16)


def _proj(h, w, name):
    b, s, _ = h.shape
    n = w.shape[1]
    tm = TM_IN
    return pl.pallas_call(
        _proj_kernel,
        grid=(b, s // tm),
        in_specs=[pl.BlockSpec((None, tm, D_MODEL), lambda bi, i: (bi, i, 0)),
                  pl.BlockSpec((D_MODEL, n), lambda bi, i: (0, 0))],
        out_specs=pl.BlockSpec((None, tm, n), lambda bi, i: (bi, i, 0)),
        out_shape=jax.ShapeDtypeStruct((b, s, n), BF16),
        compiler_params=_cparams(2),
        name=name,
    )(h, w)


def _proj_dil_kernel(h_ref, w_ref, cos_ref, sin_ref, o0_ref, o1_ref, o2_ref, z0_ref, z1_ref, z2_ref):
    tm = h_ref.shape[0]
    h = h_ref[...]
    cos, sin = cos_ref[...], sin_ref[...]
    slabs_per_part = DIL_WIDTH // LANES
    slabs_per_group = ZB_GROUP // LANES
    for gi, (o_ref, z_ref, (_, d)) in enumerate(zip((o0_ref, o1_ref, o2_ref), (z0_ref, z1_ref, z2_ref), DIL_PAIRS)):
        z = jnp.dot(h, w_ref[:, gi * ZB_GROUP:(gi + 1) * ZB_GROUP], preferred_element_type=F32)
        for c in range(slabs_per_group):
            x = z[:, c * LANES:(c + 1) * LANES]
            part = c // slabs_per_part
            if part != 2:
                x = _rope_lanes(x, cos, sin)
            if part == 0:
                x = x * LOG2_E
            z_ref[c] = x
        for r in range(d):
            rows = slice(None) if d == 1 else pl.ds(r, tm // d, stride=d)
            for c in range(slabs_per_group):
                o_ref[r, :, c * LANES:(c + 1) * LANES] = z_ref[c, rows, :].astype(BF16)


def _proj_dil(h, w, tabs):
    b, s, _ = h.shape
    tm = TM_IN
    cos, sin = tabs
    tab_spec = pl.BlockSpec((tm, LANES), lambda i, bi: (i, 0))
    out_shapes, out_specs = [], []
    for _, d in DIL_PAIRS:
        out_shapes.append(jax.ShapeDtypeStruct((b, d, s // d, ZB_GROUP), BF16))
        out_specs.append(pl.BlockSpec((None, d, tm // d, ZB_GROUP), lambda i, bi: (bi, 0, i, 0)))
    return pl.pallas_call(
        _proj_dil_kernel,
        grid=(s // tm, b),
        in_specs=[pl.BlockSpec((None, tm, D_MODEL), lambda i, bi: (bi, i, 0)),
                  pl.BlockSpec((D_MODEL, ZB_WIDTH), lambda i, bi: (0, 0)),
                  tab_spec, tab_spec],
        out_specs=out_specs,
        out_shape=out_shapes,
        scratch_shapes=[pltpu.VMEM((ZB_GROUP // LANES, tm, LANES), F32)] * 3,
        compiler_params=_cparams(2),
        name="proj_dil",
    )(h, w, cos, sin)


def _mla_prep_kernel(za_ref, gq_ref, gkv_ref, wqt_ref, wk_ref, wvt_ref,
                     cq_ref, sq_ref, ck_ref, sk_ref, q_ref, k_ref, vt_ref):
    def norm(x, g):
        x32 = x.astype(F32)
        return (x32 * lax.rsqrt(jnp.mean(x32 * x32, axis=-1, keepdims=True) + NORM_EPS) * g).astype(BF16)

    cqn = norm(za_ref[:, 0:MLA_Q_RANK], gq_ref[...])
    ckvn = norm(za_ref[:, MLA_Q_RANK:MLA_Q_RANK + MLA_KV_RANK], gkv_ref[...])
    kr = za_ref[:, MLA_Q_RANK + MLA_KV_RANK:ZA_WIDTH].astype(F32)
    kr = _rope_lanes(kr, ck_ref[...], sk_ref[...])
    qt = _nt_dot(wqt_ref[...], cqn)
    k = jnp.dot(ckvn, wk_ref[...], preferred_element_type=F32)
    cq, sq = cq_ref[...], sq_ref[...]
    vt = _nt_dot(wvt_ref[...], ckvn).astype(BF16)
    half = MLA_HEAD_PAD // 2
    for h in range(MLA_HEADS):
        cols = slice(h * MLA_HEAD_PAD, (h + 1) * MLA_HEAD_PAD)
        x = qt[cols]
        q_ref[h] = (x * cq + jnp.concatenate([x[half:], x[:half]], axis=0) * sq).astype(BF16)
        k_ref[h] = (k[:, cols] + kr).astype(BF16)
        vt_ref[h] = vt[h * MLA_V:(h + 1) * MLA_V, :]


def _mla_prep(za, g_q, g_kv, wqt, wk, wvt, qtabs_t, ktabs):
    b, s, _ = za.shape
    tm = MLA_KC
    hp = MLA_HEADS * MLA_HEAD_PAD
    tab_spec = pl.BlockSpec((tm, LANES), lambda i, bi: (i, 0))
    tab_t_spec = pl.BlockSpec((LANES, tm), lambda i, bi: (0, i))
    const = lambda shape: pl.BlockSpec(shape, lambda i, bi: (0,) * len(shape))
    return pl.pallas_call(
        _mla_prep_kernel,
        grid=(s // tm, b),
        in_specs=[pl.BlockSpec((None, tm, ZA_WIDTH), lambda i, bi: (bi, i, 0)),
                  const((1, MLA_Q_RANK)), const((1, MLA_KV_RANK)),
                  const((hp, MLA_Q_RANK)), const((MLA_KV_RANK, hp)), const((MLA_WIDTH, MLA_KV_RANK)),
                  tab_t_spec, tab_t_spec, tab_spec, tab_spec],
        out_specs=[pl.BlockSpec((None, MLA_HEADS, MLA_HEAD_PAD, tm), lambda i, bi: (bi, 0, 0, i)),
                   pl.BlockSpec((None, MLA_HEADS, tm, MLA_HEAD_PAD), lambda i, bi: (bi, 0, i, 0)),
                   pl.BlockSpec((None, None, MLA_HEADS, MLA_V, tm), lambda i, bi: (bi, i, 0, 0, 0))],
        out_shape=[jax.ShapeDtypeStruct((b, MLA_HEADS, MLA_HEAD_PAD, s), BF16),
                   jax.ShapeDtypeStruct((b, MLA_HEADS, s, MLA_HEAD_PAD), BF16),
                   jax.ShapeDtypeStruct((b, s // tm, MLA_HEADS, MLA_V, tm), BF16)],
        compiler_params=_cparams(2),
        name="mla_prep",
    )(za, g_q.reshape(1, -1), g_kv.reshape(1, -1), wqt, wk, wvt, *qtabs_t, *ktabs)


MLA_ONES_ROWS = 16


def _mla_attn_kernel(q_ref, k_ref, vt_ref, o_ref, s0_ref, s1_ref, ot_ref):
    tq = q_ref.shape[2]
    n_chunks, _, _, kc = vt_ref.shape
    ones = jnp.ones((MLA_ONES_ROWS, kc), BF16)
    acc_rows = MLA_V + MLA_ONES_ROWS
    s_refs = (s0_ref, s1_ref)

    def stage(h_score, h_value, m_value, parity):
        def body(j, carry):
            mx, acc = carry
            start = pl.multiple_of(j * kc, kc)
            if h_score is not None:
                st = jnp.dot(k_ref[h_score, pl.ds(start, kc), :], q_ref[h_score],
                             preferred_element_type=F32)
                s_refs[parity][pl.ds(start, kc), :] = st
                mx = jnp.maximum(mx, jnp.max(st.reshape(kc // 8, 8, tq), axis=0))
            if h_value is not None:
                p = jnp.exp2(s_refs[1 - parity][pl.ds(start, kc), :] - m_value).astype(BF16)
                vt = jnp.concatenate([vt_ref[j, h_value], ones], axis=0)
                acc = acc + jnp.dot(vt, p, preferred_element_type=F32)
            return mx, acc

        init = (jnp.full((8, tq), NEG_INF, F32), jnp.zeros((acc_rows, tq), F32))
        mx, acc = lax.fori_loop(0, n_chunks, body, init, unroll=MLA_UNROLL)
        if h_value is not None:
            ot_ref[h_value] = acc[:MLA_V] / acc[MLA_V:MLA_V + 1]
        return jnp.max(mx, axis=0, keepdims=True)

    def stage_pair(u, m):
        m = stage(2 * u + 1, 2 * u, m, 1)
        return stage(2 * u + 2, 2 * u + 1, m, 0)

    m = stage(0, None, None, 0)
    m = lax.fori_loop(0, MLA_HEADS // 2 - 1, stage_pair, m)
    m = stage(MLA_HEADS - 1, MLA_HEADS - 2, m, 1)
    stage(None, MLA_HEADS - 1, m, 0)
    for pr in range(MLA_HEADS // 2):
        pair = jnp.concatenate([ot_ref[2 * pr], ot_ref[2 * pr + 1]], axis=0)
        o_ref[:, 2 * pr * MLA_V:(2 * pr + 2) * MLA_V] = pair.T.astype(BF16)


def _mla_attn(qt, k, vt):
    b, _, s, _ = k.shape
    n_chunks, kc = vt.shape[1], vt.shape[4]
    tq = min(MLA_TQ, s)
    return pl.pallas_call(
        _mla_attn_kernel,
        grid=(b, s // tq),
        in_specs=[pl.BlockSpec((None, MLA_HEADS, MLA_HEAD_PAD, tq), lambda bi, i: (bi, 0, 0, i)),
                  pl.BlockSpec((None, MLA_HEADS, s, MLA_HEAD_PAD), lambda bi, i: (bi, 0, 0, 0)),
                  pl.BlockSpec((None, n_chunks, MLA_HEADS, MLA_V, kc), lambda bi, i: (bi, 0, 0, 0, 0))],
        out_specs=pl.BlockSpec((None, tq, MLA_WIDTH), lambda bi, i: (bi, i, 0)),
        out_shape=jax.ShapeDtypeStruct((b, s, MLA_WIDTH), BF16),
        scratch_shapes=[pltpu.VMEM((s, tq), F32), pltpu.VMEM((s, tq), F32), pltpu.VMEM((MLA_HEADS, MLA_V, tq), F32)],
        compiler_params=_cparams(2),
        name="mla_attn",
    )(qt, k, vt)


def _dil_attn_kernel(qkv_ref, o_ref, lse_ref, s_ref, *, tq, win):
    d, length, _ = qkv_ref.shape
    tiles = length // tq
    n_total = d * tiles
    heads = range(DIL_HPG)
    key_rel = lax.broadcasted_iota(jnp.int32, (tq, win), 1) - lax.broadcasted_iota(jnp.int32, (tq, win), 0)
    lane = lax.broadcasted_iota(jnp.int32, (tq, LANES), 1)
    low_half = lane < DIL_HD
    qk_head = (lane // (DIL_HD // 2)) % 2
    head_lanes = (qk_head == 0, qk_head == 1)

    def coords(n):
        r = n // tiles
        q0 = pl.multiple_of((n - r * tiles) * tq, tq)
        start = pl.multiple_of(jnp.clip(q0 - DIL_BAND, 0, length - win), DIL_BAND)
        return r, q0, start

    def score_step(n):
        r, q0, start = coords(n)
        valid = jnp.abs(key_rel + (start - q0)) <= DIL_BAND
        ms = []
        for h in heads:
            cols = slice((h // 2) * LANES, (h // 2 + 1) * LANES)
            q = qkv_ref[r, pl.ds(q0, tq), cols]
            k = qkv_ref[r, pl.ds(start, win), DIL_WIDTH + cols.start:DIL_WIDTH + cols.stop]
            qm = jnp.where(head_lanes[h % 2], q, jnp.zeros_like(q))
            sc = jnp.where(valid, _nt_dot(qm, k), NEG_INF)
            s_ref[n % 2, h] = sc
            ms.append(jnp.max(sc, axis=-1, keepdims=True))
        return tuple(ms)

    def value_step(n, ms):
        r, q0, start = coords(n)
        res = []
        for h in heads:
            cols = slice(2 * DIL_WIDTH + (h // 2) * LANES, 2 * DIL_WIDTH + (h // 2 + 1) * LANES)
            p = jnp.exp2(s_ref[n % 2, h] - ms[h])
            l = jnp.sum(p, axis=-1, keepdims=True)
            o = jnp.dot(p.astype(BF16), qkv_ref[r, pl.ds(start, win), cols], preferred_element_type=F32) / l
            res.append((o, ms[h] + jnp.log2(l)))
        for pr in range(DIL_HPG // 2):
            cols = slice(pr * LANES, (pr + 1) * LANES)
            o_ref[r, pl.ds(q0, tq), cols] = jnp.where(low_half, res[2 * pr][0], res[2 * pr + 1][0]).astype(BF16)
            lse_ref[r, pl.ds(q0, tq), cols] = jnp.where(low_half, res[2 * pr][1], res[2 * pr + 1][1])

    def body(n, ms):
        ms_next = score_step(jnp.minimum(n + 1, n_total - 1))
        value_step(n, ms)
        return ms_next

    lax.fori_loop(0, n_total, body, score_step(0), unroll=4)


def _dil_attn(zb, d):
    b, _, length, _ = zb.shape
    tq = min(2 * DIL_BAND, length)
    win = min(4 * DIL_BAND, length)
    whole = lambda width: pl.BlockSpec((None, d, length, width), lambda bi: (bi, 0, 0, 0))
    return pl.pallas_call(
        functools.partial(_dil_attn_kernel, tq=tq, win=win),
        grid=(b,),
        in_specs=[whole(ZB_GROUP)],
        out_specs=[whole(DIL_WIDTH), whole(DIL_WIDTH)],
        out_shape=[jax.ShapeDtypeStruct((b, d, length, DIL_WIDTH), BF16),
                   jax.ShapeDtypeStruct((b, d, length, DIL_WIDTH), F32)],
        scratch_shapes=[pltpu.VMEM((2, DIL_HPG, tq, win), F32)],
        compiler_params=_cparams(1),
        name=f"dil_attn_d{d}",
    )(zb)


NA_QROWS = 4
NA_UROWS = NA_QROWS + NA_KH
NA_QTOK = NA_QROWS * GRID_W
NA_UTOK = NA_UROWS * GRID_W


def _proj_na_kernel(h_ref, wk_ref, wqvt_ref, qt_ref, k_ref, vt_ref):
    h = h_ref[...]
    k_ref[...] = jnp.dot(h, wk_ref[...], preferred_element_type=F32).astype(BF16)
    qvt = _nt_dot(wqvt_ref[...], h)
    qt_ref[...] = (qvt[:NA_WIDTH] * LOG2_E).astype(BF16)
    for c in range(vt_ref.shape[0]):
        vt_ref[c] = qvt[NA_WIDTH:, c * NA_QTOK:(c + 1) * NA_QTOK].astype(BF16)


def _proj_na(h, wk, wqvt):
    b, s, _ = h.shape
    tm = TM_IN
    return pl.pallas_call(
        _proj_na_kernel,
        grid=(b, s // tm),
        in_specs=[pl.BlockSpec((None, tm, D_MODEL), lambda bi, i: (bi, i, 0)),
                  pl.BlockSpec((D_MODEL, NA_WIDTH), lambda bi, i: (0, 0)),
                  pl.BlockSpec((2 * NA_WIDTH, D_MODEL), lambda bi, i: (0, 0))],
        out_specs=[pl.BlockSpec((None, NA_WIDTH, tm), lambda bi, i: (bi, 0, i)),
                   pl.BlockSpec((None, tm, NA_WIDTH), lambda bi, i: (bi, i, 0)),
                   pl.BlockSpec((None, tm // NA_QTOK, NA_WIDTH, NA_QTOK), lambda bi, i: (bi, i, 0, 0))],
        out_shape=[jax.ShapeDtypeStruct((b, NA_WIDTH, s), BF16),
                   jax.ShapeDtypeStruct((b, s, NA_WIDTH), BF16),
                   jax.ShapeDtypeStruct((b, s // NA_QTOK, NA_WIDTH, NA_QTOK), BF16)],
        compiler_params=_cparams(2),
        name="proj_na",
    )(h, wk, wqvt)


NA_VARIANTS = ((0, lambda i: 0), (-NA_KH // 2, lambda i: i), (-NA_KH, lambda i: NA_KH // 2))


def _na_bias_kernel(rpb_ref, t_ref):
    h = pl.program_id(0)
    shape = (GRID_W, LANES)
    w = lax.broadcasted_iota(jnp.int32, shape, 0)
    lane = lax.broadcasted_iota(jnp.int32, shape, 1)
    c = lane & (GRID_W - 1)
    first = lane < GRID_W
    cs = jnp.clip(c - NA_KW // 2, 0, GRID_W - NA_KW)
    inside = (w >= cs) & (w < cs + NA_KW)
    off = w - c + NA_KW - 1
    neg = jnp.full(shape, NEG_INF, F32)
    tiles = {(None, None): neg}
    for v, (delta, lo) in enumerate(NA_VARIANTS):
        for jk in range(NA_UROWS):
            for ip in range(NA_QROWS // 2):
                ro = []
                for i in (2 * ip, 2 * ip + 1):
                    ok = lo(i) <= jk < lo(i) + NA_KH
                    ro.append(jk - i + NA_KH - 1 + delta if ok else None)
                ro = tuple(ro)
                if ro not in tiles:
                    def body(kk, acc, ro=ro):
                        a = NEG_INF if ro[0] is None else rpb_ref[h, ro[0], kk]
                        b = NEG_INF if ro[1] is None else rpb_ref[h, ro[1], kk]
                        return jnp.where(off == kk, jnp.where(first, a, b), acc)

                    acc = lax.fori_loop(0, 2 * NA_KW - 1, body, neg)
                    tiles[ro] = jnp.where(inside, acc * LOG2_E, NEG_INF)
                t_ref[v, jk * GRID_W:(jk + 1) * GRID_W, ip * LANES:(ip + 1) * LANES] = tiles[ro]


def _na_bias(rpb_l):
    nv = len(NA_VARIANTS)
    return pl.pallas_call(
        _na_bias_kernel,
        grid=(NA_HEADS,),
        in_specs=[pl.BlockSpec(memory_space=pltpu.SMEM)],
        out_specs=pl.BlockSpec((nv, None, NA_UTOK, NA_QTOK), lambda h: (0, h, 0, 0)),
        out_shape=jax.ShapeDtypeStruct((nv, NA_HEADS, NA_UTOK, NA_QTOK), F32),
        compiler_params=_cparams(1),
        name="na_bias",
    )(rpb_l)


def _na_attn_kernel(q_ref, k_ref, vt_ref, t_ref, o_ref, s_ref, *, nblk):
    c0 = jnp.clip(pl.program_id(1) - 1, 0, nblk - NA_UROWS // NA_QROWS)
    k0 = pl.multiple_of(c0 * NA_QTOK, NA_QTOK)
    row = lax.broadcasted_iota(jnp.int32, (LANES, NA_QTOK), 0)
    head_rows = (row < NA_HD, row >= NA_HD)
    ones = jnp.ones((MLA_ONES_ROWS, NA_UTOK), BF16)

    def scores(h):
        cols = slice((h // 2) * LANES, (h // 2 + 1) * LANES)
        qt = q_ref[cols, :]
        qm = jnp.where(head_rows[h % 2], qt, jnp.zeros_like(qt))
        st = jnp.dot(k_ref[pl.ds(k0, NA_UTOK), cols], qm, preferred_element_type=F32) + t_ref[h]
        s_ref[h % 2] = st
        return jnp.max(jnp.max(st.reshape(NA_UTOK // 8, 8, NA_QTOK), axis=0), axis=0, keepdims=True)

    def values(h, m):
        pt = jnp.exp2(s_ref[h % 2] - m).astype(BF16)
        vt = jnp.concatenate([vt_ref[c0 + c, h * NA_HD:(h + 1) * NA_HD, :] for c in range(NA_UROWS // NA_QROWS)],
                             axis=1)
        acc = jnp.dot(jnp.concatenate([vt, ones], axis=0), pt, preferred_element_type=F32)
        return acc[:NA_HD] / acc[NA_HD:NA_HD + 1]

    m_next = scores(0)
    o_prev = None
    for h in range(NA_HEADS):
        m = m_next
        if h + 1 < NA_HEADS:
            m_next = scores(h + 1)
        o_t = values(h, m)
        if h % 2 == 1:
            o_ref[:, (h - 1) * NA_HD:(h + 1) * NA_HD] = jnp.concatenate([o_prev, o_t], axis=0).T.astype(BF16)
        o_prev = o_t


def _na_attn(qt, k, vt, table):
    b, s, _ = k.shape
    nblk = s // NA_QTOK
    last = nblk - 1
    variant = lambda bi, a: (jnp.where(a == 0, 0, jnp.where(a == last, 2, 1)), 0, 0, 0)
    return pl.pallas_call(
        functools.partial(_na_attn_kernel, nblk=nblk),
        grid=(b, nblk),
        in_specs=[pl.BlockSpec((None, NA_WIDTH, NA_QTOK), lambda bi, a: (bi, 0, a)),
                  pl.BlockSpec((None, s, NA_WIDTH), lambda bi, a: (bi, 0, 0)),
                  pl.BlockSpec((None, nblk, NA_WIDTH, NA_QTOK), lambda bi, a: (bi, 0, 0, 0)),
                  pl.BlockSpec((None, NA_HEADS, NA_UTOK, NA_QTOK), variant)],
        out_specs=pl.BlockSpec((None, NA_QTOK, NA_WIDTH), lambda bi, a: (bi, a, 0)),
        out_shape=jax.ShapeDtypeStruct((b, s, NA_WIDTH), BF16),
        scratch_shapes=[pltpu.VMEM((2, NA_UTOK, NA_QTOK), F32)],
        compiler_params=_cparams(2),
        name="na_attn",
    )(qt, k, vt, table)


def _out_kernel(*refs, emit_h):
    (x_ref, oa_ref, gates_ref, oc_ref, merge_ref,
     ob0_ref, ob1_ref, ob2_ref, ls0_ref, ls1_ref, ls2_ref,
     wpa_ref, wpb_ref, wpc_ref, wout_ref, gpost_ref, mod_ref) = refs[:17]
    rest = refs[17:]
    if emit_h:
        gnext_ref, modn_ref, y_ref, h_ref, so_ref, sl_ref = rest
    else:
        y_ref, so_ref, sl_ref = rest
    tm = x_ref.shape[0]

    for gi, (ob_ref, ls_ref, (_, d)) in enumerate(zip((ob0_ref, ob1_ref, ob2_ref), (ls0_ref, ls1_ref, ls2_ref),
                                                     DIL_PAIRS)):
        for r in range(d):
            rows = slice(None) if d == 1 else pl.ds(r, tm // d, stride=d)
            for c in range(DIL_WIDTH // LANES):
                cols = slice(c * LANES, (c + 1) * LANES)
                so_ref[gi, c, rows, :] = ob_ref[r, :, cols].astype(F32)
                sl_ref[gi, c, rows, :] = ls_ref[r, :, cols]
    slabs = range(DIL_WIDTH // LANES)
    lse = [jnp.concatenate([sl_ref[gi, c] for c in slabs], axis=1) for gi in range(3)]
    o_g = [jnp.concatenate([so_ref[gi, c] for c in slabs], axis=1) for gi in range(3)]
    mx = jnp.maximum(jnp.maximum(lse[0], lse[1]), lse[2])
    e = [jnp.exp2(x - mx) for x in lse]
    o_b = (e[0] * o_g[0] + e[1] * o_g[1] + e[2] * o_g[2]) / (e[0] + e[1] + e[2])

    def gated(o, lo, hi):
        g = gates_ref[:, lo:hi].astype(F32)
        return (o * (g * (1.0 + jnp.tanh(g)))).astype(BF16)

    def merged(idx, act, w_ref):
        t = jnp.tanh(merge_ref[:, idx * D_MODEL:(idx + 1) * D_MODEL].astype(F32))
        return (1.0 + t) * jnp.dot(act, w_ref[...], preferred_element_type=F32)

    a = gated(oa_ref[...].astype(F32), 0, MLA_WIDTH)
    bb = gated(o_b, MLA_WIDTH, MLA_WIDTH + DIL_WIDTH)
    c = gated(oc_ref[...].astype(F32), MLA_WIDTH + DIL_WIDTH, GATE_WIDTH)
    mixed = merged(0, a, wpa_ref) + merged(1, bb, wpb_ref) + merged(2, c, wpc_ref)
    out = jnp.dot(mixed.astype(BF16), wout_ref[...], preferred_element_type=F32)
    normed = out * lax.rsqrt(jnp.mean(out * out, axis=-1, keepdims=True) + NORM_EPS) * gpost_ref[...]
    y = x_ref[...] + mod_ref[2:3, :] * normed
    y_ref[...] = y
    if emit_h:
        h_ref[...] = _modulated_norm(y, gnext_ref[...], modn_ref[...]).astype(BF16)


def _out_layer(x, oa, gates, oc, merge, obs, lses, wpa, wpb, wpc, wout, g_post, mod, g_next, mod_next):
    b, s, _ = x.shape
    tm = TM_OUT
    emit_h = g_next is not None
    tok = lambda n: pl.BlockSpec((None, tm, n), lambda bi, i: (bi, i, 0))
    const = lambda shape: pl.BlockSpec(shape, lambda bi, i: (0,) * len(shape))
    modspec = pl.BlockSpec((None, 3, D_MODEL), lambda bi, i: (bi, 0, 0))
    cls = [pl.BlockSpec((None, d, tm // d, DIL_WIDTH), lambda bi, i: (bi, 0, i, 0)) for _, d in DIL_PAIRS]
    in_specs = [tok(D_MODEL), tok(MLA_WIDTH), tok(GATE_WIDTH), tok(NA_WIDTH), tok(3 * D_MODEL),
                *cls, *cls,
                const((MLA_WIDTH, D_MODEL)), const((DIL_WIDTH, D_MODEL)), const((NA_WIDTH, D_MODEL)),
                const((D_MODEL, D_MODEL)), const((1, D_MODEL)), modspec]
    args = [x, oa, gates, oc, merge, *obs, *lses, wpa, wpb, wpc, wout, g_post.reshape(1, D_MODEL), mod]
    out_specs = [tok(D_MODEL)]
    out_shape = [jax.ShapeDtypeStruct((b, s, D_MODEL), F32)]
    if emit_h:
        in_specs += [const((1, D_MODEL)), modspec]
        args += [g_next.reshape(1, D_MODEL), mod_next]
        out_specs.append(tok(D_MODEL))
        out_shape.append(jax.ShapeDtypeStruct((b, s, D_MODEL), BF16))
    res = pl.pallas_call(
        functools.partial(_out_kernel, emit_h=emit_h),
        grid=(b, s // tm),
        in_specs=in_specs,
        out_specs=out_specs,
        out_shape=out_shape,
        scratch_shapes=[pltpu.VMEM((3, DIL_WIDTH // LANES, tm, LANES), F32)] * 2,
        compiler_params=_cparams(2),
        name="out_layer",
    )(*args)
    return (res[0], res[1]) if emit_h else (res[0], None)


def _rope_tables(s, head_dim, lane_dim, scale):
    half = head_dim // 2
    inv = ROPE_THETA ** (-jnp.arange(half, dtype=F32) * 2.0 / head_dim)
    ang = jnp.arange(s, dtype=F32)[:, None] * inv[None, :]
    cos, sin = jnp.cos(ang), jnp.sin(ang)
    lane_dim = np.asarray(lane_dim)
    in_rope = lane_dim >= 0
    idx = np.where(in_rope, lane_dim % half, 0)
    sign = np.where(lane_dim < half, -1.0, 1.0).astype(np.float32)
    cos_t = jnp.where(in_rope[None, :], cos[:, idx], 1.0) * scale
    sin_t = jnp.where(in_rope[None, :], sin[:, idx] * sign[None, :], 0.0) * scale
    return cos_t.astype(F32), sin_t.astype(F32)


_LANE = np.arange(LANES)
DIL_LANE_DIM = _LANE % (DIL_HD // 2) + (DIL_HD // 2) * (_LANE // (LANES // 2))
MLA_LANE_DIM = np.where(_LANE < MLA_ROPE // 2, _LANE,
                        np.where((_LANE >= LANES // 2) & (_LANE < LANES // 2 + MLA_ROPE // 2),
                                 _LANE - LANES // 2 + MLA_ROPE // 2, -1))


def _pair_rotary_order(w):
    lead = w.shape[:-1]
    w = w.reshape(*lead, DIL_HPG // 2, 2, 2, DIL_HD // 2)
    return jnp.swapaxes(w, -3, -2).reshape(*lead, DIL_WIDTH)


def _mla_head_slab(nope, rope):
    lead = (nope if nope is not None else rope).shape[:-1]
    dtype = (nope if nope is not None else rope).dtype
    z = lambda n: jnp.zeros(lead + (n,), dtype)
    r = MLA_ROPE // 2
    n0 = LANES // 2 - r
    parts = [z(r) if rope is None else rope[..., :r],
             z(n0) if nope is None else nope[..., :n0],
             z(r) if rope is None else rope[..., r:],
             z(MLA_NOPE - n0) if nope is None else nope[..., n0:],
             z(LANES - MLA_NOPE - MLA_ROPE)]
    return jnp.concatenate(parts, axis=-1)


def _layout_weights(w_in, w_uq, w_ukv):
    w_in, w_uq, w_ukv = w_in.astype(BF16), w_uq.astype(BF16), w_ukv.astype(BF16)
    c = [0] + [int(v) for v in _CUTS]
    cq, ckv, kr, gate_a, qkv_b, gate_b, qkv_c, gate_c, merge = [w_in[:, :, c[i]:c[i + 1]] for i in range(9)]
    w_a = jnp.concatenate([cq, ckv, _mla_head_slab(None, kr)], axis=-1)
    w_g = 0.5 * jnp.concatenate([gate_a, gate_b, gate_c], axis=-1)
    merge = 0.5 * merge
    qkv_b = qkv_b.reshape(DEPTH, D_MODEL, 3, len(DIL_PAIRS), DIL_WIDTH)
    qkv_b = qkv_b * jnp.array([DIL_HD ** -0.5, 1.0, 1.0], w_in.dtype)[None, None, :, None, None]
    qkv_b = jnp.concatenate([_pair_rotary_order(qkv_b[:, :, :2]), qkv_b[:, :, 2:]], axis=2)
    w_b = qkv_b.transpose(0, 1, 3, 2, 4).reshape(DEPTH, D_MODEL, ZB_WIDTH)
    qkv_c = qkv_c.reshape(DEPTH, D_MODEL, 3, NA_WIDTH)
    qkv_c = qkv_c * jnp.array([NA_HD ** -0.5, 1.0, 1.0], w_in.dtype)[None, None, :, None]
    w_ck = qkv_c[:, :, 1]
    w_cqvt = jnp.concatenate([qkv_c[:, :, 0], qkv_c[:, :, 2]], axis=-1).transpose(0, 2, 1)
    uq = w_uq.reshape(DEPTH, MLA_Q_RANK, MLA_HEADS, MLA_DQK)
    w_q = _mla_head_slab(uq[..., :MLA_NOPE], uq[..., MLA_NOPE:]).reshape(DEPTH, MLA_Q_RANK, MLA_HEADS * MLA_HEAD_PAD)
    ukv = w_ukv.reshape(DEPTH, MLA_KV_RANK, MLA_HEADS, MLA_NOPE + MLA_V)
    w_k = _mla_head_slab(ukv[..., :MLA_NOPE], None).reshape(DEPTH, MLA_KV_RANK, MLA_HEADS * MLA_HEAD_PAD)
    w_vt = ukv[..., MLA_NOPE:].reshape(DEPTH, MLA_KV_RANK, MLA_WIDTH).transpose(0, 2, 1)
    bf = lambda w: w.astype(BF16)
    return dict(a=bf(w_a), g=bf(w_g), b=bf(w_b), ck=bf(w_ck), cqvt=bf(w_cqvt), m=bf(merge),
                qt=bf(w_q).transpose(0, 2, 1), k=bf(w_k), vt=bf(w_vt))


def _trunk(x, c, p):
    b, s, _ = x.shape
    mods = _ada(c, p["w_ada"], p["b_ada"]).reshape(DEPTH, b, 3, D_MODEL)
    dil_tabs = _rope_tables(s, DIL_HD, DIL_LANE_DIM, 1.0)
    q_tabs_t = tuple(t.T for t in _rope_tables(s, MLA_ROPE, MLA_LANE_DIM, MLA_DQK ** -0.5 * LOG2_E))
    k_tabs = _rope_tables(s, MLA_ROPE, MLA_LANE_DIM, 1.0)
    w = p["w"]
    h = _prenorm(x, p["g_pre"][0], mods[0])
    for l in range(DEPTH):
        za = _proj(h, w["a"][l], "proj_mla")
        gates = _proj(h, w["g"][l], "proj_gates")
        qc, kc, vtc = _proj_na(h, w["ck"][l], w["cqvt"][l])
        merge = _proj(h, w["m"][l], "proj_merge")
        zbs = _proj_dil(h, w["b"][l], dil_tabs)
        q, k, vt = _mla_prep(za, p["g_q"][l], p["g_kv"][l], w["qt"][l], w["k"][l], w["vt"][l], q_tabs_t, k_tabs)
        oa = _mla_attn(q, k, vt)
        dil = [_dil_attn(zb, d) for zb, (_, d) in zip(zbs, DIL_PAIRS)]
        oc = _na_attn(qc, kc, vtc, p["na_bias"][l])
        last = l == DEPTH - 1
        x, h = _out_layer(x, oa, gates, oc, merge, [o for o, _ in dil], [ls for _, ls in dil],
                          p["w_pa"][l], p["w_pb"][l], p["w_pc"][l], p["w_out"][l], p["g_post"][l], mods[l],
                          None if last else p["g_pre"][l + 1], None if last else mods[l + 1])
    return x


def _prepare(w_ada, b_ada, g_pre, g_post, w_in, g_q, w_uq, g_kv, w_ukv, rpb, w_pa, w_pb, w_pc, w_out):
    return dict(w_ada=w_ada.astype(BF16), b_ada=b_ada, g_pre=g_pre, g_post=g_post, g_q=g_q, g_kv=g_kv,
                w=_layout_weights(w_in, w_uq, w_ukv),
                na_bias=[_na_bias(rpb[l]) for l in range(DEPTH)],
                w_pa=(0.5 * w_pa).astype(BF16), w_pb=(0.5 * w_pb).astype(BF16), w_pc=(0.5 * w_pc).astype(BF16),
                w_out=w_out.astype(BF16))


def kernel(x_prompt, x_sample, c_prompt, c_sample, w_ada, b_ada, g_pre, g_post, w_in, g_q, w_uq, g_kv, w_ukv, rpb,
           w_pa, w_pb, w_pc, w_out):
    p = _prepare(w_ada, b_ada, g_pre, g_post, w_in, g_q, w_uq, g_kv, w_ukv, rpb, w_pa, w_pb, w_pc, w_out)
    return (_trunk(x_prompt, c_prompt, p), _trunk(x_sample, c_sample, p))
```

```python
import functools

import jax
import jax.numpy as jnp
import numpy as np
from jax import lax
from jax.experimental import pallas as pl
from jax.experimental.pallas import tpu as pltpu

F32 = jnp.float32
BF16 = jnp.bfloat16

D_MODEL = 1024
DEPTH = 4
GRID_W = 64
ROPE_THETA = 10000.0
NORM_EPS = 1e-6
NEG_INF = -1e30
LOG2_E = float(np.log2(np.e))

MLA_HEADS = 8
MLA_NOPE = 64
MLA_ROPE = 32
MLA_V = 64
MLA_DQK = MLA_NOPE + MLA_ROPE
MLA_Q_RANK = 384
MLA_KV_RANK = 256
MLA_WIDTH = MLA_HEADS * MLA_V
MLA_HEAD_PAD = 128

DIL_PAIRS = ((128, 1), (512, 4), (2048, 16))
DIL_HPG = 4
DIL_HD = 64
DIL_HEADS = 12
DIL_WIDTH = DIL_HPG * DIL_HD
DIL_BAND = 64

NA_HEADS = 8
NA_HD = 64
NA_KH = 8
NA_KW = 16
NA_WIDTH = NA_HEADS * NA_HD

LANES = 128
VMEM_LIMIT = 56 * 1024 * 1024

_CUTS = np.cumsum((MLA_Q_RANK, MLA_KV_RANK, MLA_ROPE, MLA_WIDTH, 3 * DIL_HEADS * DIL_HD, DIL_WIDTH,
                   3 * NA_HEADS * NA_HD, NA_WIDTH, 3 * D_MODEL))
ZA_WIDTH = MLA_Q_RANK + MLA_KV_RANK + LANES
GATE_WIDTH = MLA_WIDTH + DIL_WIDTH + NA_WIDTH
ZB_WIDTH = 3 * DIL_HEADS * DIL_HD
ZB_GROUP = 3 * DIL_WIDTH
ZC_WIDTH = 3 * NA_WIDTH

TM_IN = 512
TM_OUT = 256
MLA_TQ = 512
MLA_KC = 512
MLA_UNROLL = 8


def _cparams(n_grid):
    return pltpu.CompilerParams(dimension_semantics=("arbitrary",) * n_grid, vmem_limit_bytes=VMEM_LIMIT)


def _sigmoid(x):
    return 0.5 * (1.0 + jnp.tanh(0.5 * x))


def _rope_lanes(x, cos, sin):
    return x * cos + pltpu.roll(x, LANES // 2, 1) * sin


def _nt_dot(a, b):
    return lax.dot_general(a, b, (((1,), (1,)), ((), ())), preferred_element_type=F32)


def _ada_kernel(c_ref, w_ref, b_ref, o_ref):
    c = c_ref[...]
    c_act = (c * _sigmoid(c)).astype(BF16)
    o_ref[...] = jnp.dot(c_act, w_ref[...], preferred_element_type=F32) + b_ref[...]


def _ada(c, w_ada_bf, b_ada):
    b = c.shape[0]
    return pl.pallas_call(
        _ada_kernel,
        grid=(DEPTH,),
        in_specs=[pl.BlockSpec((b, D_MODEL), lambda l: (0, 0)),
                  pl.BlockSpec((None, D_MODEL, 3 * D_MODEL), lambda l: (l, 0, 0)),
                  pl.BlockSpec((None, 1, 3 * D_MODEL), lambda l: (l, 0, 0))],
        out_specs=pl.BlockSpec((None, b, 3 * D_MODEL), lambda l: (l, 0, 0)),
        out_shape=jax.ShapeDtypeStruct((DEPTH, b, 3 * D_MODEL), F32),
        compiler_params=_cparams(1),
        name="ada",
    )(c, w_ada_bf, b_ada.reshape(DEPTH, 1, 3 * D_MODEL))


def _modulated_norm(x32, g, mod):
    gain = g * (1.0 + mod[1:2, :])
    return (x32 * lax.rsqrt(jnp.mean(x32 * x32, axis=-1, keepdims=True) + NORM_EPS)) * gain + mod[0:1, :]


def _prenorm_kernel(x_ref, g_ref, mod_ref, h_ref):
    h_ref[...] = _modulated_norm(x_ref[...], g_ref[...], mod_ref[...]).astype(BF16)


def _prenorm(x, g, mod):
    b, s, _ = x.shape
    tm = TM_IN
    return pl.pallas_call(
        _prenorm_kernel,
        grid=(b, s // tm),
        in_specs=[pl.BlockSpec((None, tm, D_MODEL), lambda bi, i: (bi, i, 0)),
                  pl.BlockSpec((1, D_MODEL), lambda bi, i: (0, 0)),
                  pl.BlockSpec((None, 3, D_MODEL), lambda bi, i: (bi, 0, 0))],
        out_specs=pl.BlockSpec((None, tm, D_MODEL), lambda bi, i: (bi, i, 0)),
        out_shape=jax.ShapeDtypeStruct((b, s, D_MODEL), BF16),
        compiler_params=_cparams(2),
        name="prenorm",
    )(x, g.reshape(1, D_MODEL), mod)


def _proj_kernel(h_ref, w_ref, o_ref):
    o_ref[...] = jnp.dot(h_ref[...], w_ref[...], preferred_element_type=F32).astype(BF16)


def _proj(h, w, name):
    b, s, _ = h.shape
    n = w.shape[1]
    tm = TM_IN
    return pl.pallas_call(
        _proj_kernel,
        grid=(b, s // tm),
        in_specs=[pl.BlockSpec((None, tm, D_MODEL), lambda bi, i: (bi, i, 0)),
                  pl.BlockSpec((D_MODEL, n), lambda bi, i: (0, 0))],
        out_specs=pl.BlockSpec((None, tm, n), lambda bi, i: (bi, i, 0)),
        out_shape=jax.ShapeDtypeStruct((b, s, n), BF16),
        compiler_params=_cparams(2),
        name=name,
    )(h, w)


def _proj_dil_kernel(h_ref, w_ref, cos_ref, sin_ref, o0_ref, o1_ref, o2_ref, z0_ref, z1_ref, z2_ref):
    tm = h_ref.shape[0]
    h = h_ref[...]
    cos, sin = cos_ref[...], sin_ref[...]
    slabs_per_part = DIL_WIDTH // LANES
    slabs_per_group = ZB_GROUP // LANES
    for gi, (o_ref, z_ref, (_, d)) in enumerate(zip((o0_ref, o1_ref, o2_ref), (z0_ref, z1_ref, z2_ref), DIL_PAIRS)):
        z = jnp.dot(h, w_ref[:, gi * ZB_GROUP:(gi + 1) * ZB_GROUP], preferred_element_type=F32)
        for c in range(slabs_per_group):
            x = z[:, c * LANES:(c + 1) * LANES]
            part = c // slabs_per_part
            if part != 2:
                x = _rope_lanes(x, cos, sin)
            if part == 0:
                x = x * LOG2_E
            z_ref[c] = x
        for r in range(d):
            rows = slice(None) if d == 1 else pl.ds(r, tm // d, stride=d)
            for c in range(slabs_per_group):
                o_ref[r, :, c * LANES:(c + 1) * LANES] = z_ref[c, rows, :].astype(BF16)


def _proj_dil(h, w, tabs):
    b, s, _ = h.shape
    tm = TM_IN
    cos, sin = tabs
    tab_spec = pl.BlockSpec((tm, LANES), lambda i, bi: (i, 0))
    out_shapes, out_specs = [], []
    for _, d in DIL_PAIRS:
        out_shapes.append(jax.ShapeDtypeStruct((b, d, s // d, ZB_GROUP), BF16))
        out_specs.append(pl.BlockSpec((None, d, tm // d, ZB_GROUP), lambda i, bi: (bi, 0, i, 0)))
    return pl.pallas_call(
        _proj_dil_kernel,
        grid=(s // tm, b),
        in_specs=[pl.BlockSpec((None, tm, D_MODEL), lambda i, bi: (bi, i, 0)),
                  pl.BlockSpec((D_MODEL, ZB_WIDTH), lambda i, bi: (0, 0)),
                  tab_spec, tab_spec],
        out_specs=out_specs,
        out_shape=out_shapes,
        scratch_shapes=[pltpu.VMEM((ZB_GROUP // LANES, tm, LANES), F32)] * 3,
        compiler_params=_cparams(2),
        name="proj_dil",
    )(h, w, cos, sin)


def _mla_prep_kernel(h_ref, wa_ref, gq_ref, gkv_ref, wqt_ref, wk_ref, wvt_ref,
                     cq_ref, sq_ref, ck_ref, sk_ref, q_ref, k_ref, vt_ref):
    def norm(x32, g):
        return (x32 * lax.rsqrt(jnp.mean(x32 * x32, axis=-1, keepdims=True) + NORM_EPS) * g).astype(BF16)

    za = jnp.dot(h_ref[...], wa_ref[...], preferred_element_type=F32)
    cqn = norm(za[:, 0:MLA_Q_RANK], gq_ref[...])
    ckvn = norm(za[:, MLA_Q_RANK:MLA_Q_RANK + MLA_KV_RANK], gkv_ref[...])
    kr = za[:, MLA_Q_RANK + MLA_KV_RANK:ZA_WIDTH]
    kr = _rope_lanes(kr, ck_ref[...], sk_ref[...])
    qt = _nt_dot(wqt_ref[...], cqn)
    k = jnp.dot(ckvn, wk_ref[...], preferred_element_type=F32)
    cq, sq = cq_ref[...], sq_ref[...]
    vt = _nt_dot(wvt_ref[...], ckvn).astype(BF16)
    half = MLA_HEAD_PAD // 2
    for h in range(MLA_HEADS):
        cols = slice(h * MLA_HEAD_PAD, (h + 1) * MLA_HEAD_PAD)
        x = qt[cols]
        q_ref[h] = (x * cq + jnp.concatenate([x[half:], x[:half]], axis=0) * sq).astype(BF16)
        k_ref[h] = (k[:, cols] + kr).astype(BF16)
        vt_ref[h] = vt[h * MLA_V:(h + 1) * MLA_V, :]


def _mla_prep(h, wa, g_q, g_kv, wqt, wk, wvt, qtabs_t, ktabs):
    b, s, _ = h.shape
    tm = MLA_KC
    hp = MLA_HEADS * MLA_HEAD_PAD
    tab_spec = pl.BlockSpec((tm, LANES), lambda i, bi: (i, 0))
    tab_t_spec = pl.BlockSpec((LANES, tm), lambda i, bi: (0, i))
    const = lambda shape: pl.BlockSpec(shape, lambda i, bi: (0,) * len(shape))
    return pl.pallas_call(
        _mla_prep_kernel,
        grid=(s // tm, b),
        in_specs=[pl.BlockSpec((None, tm, D_MODEL), lambda i, bi: (bi, i, 0)), const((D_MODEL, ZA_WIDTH)),
                  const((1, MLA_Q_RANK)), const((1, MLA_KV_RANK)),
                  const((hp, MLA_Q_RANK)), const((MLA_KV_RANK, hp)), const((MLA_WIDTH, MLA_KV_RANK)),
                  tab_t_spec, tab_t_spec, tab_spec, tab_spec],
        out_specs=[pl.BlockSpec((None, MLA_HEADS, MLA_HEAD_PAD, tm), lambda i, bi: (bi, 0, 0, i)),
                   pl.BlockSpec((None, MLA_HEADS, tm, MLA_HEAD_PAD), lambda i, bi: (bi, 0, i, 0)),
                   pl.BlockSpec((None, None, MLA_HEADS, MLA_V, tm), lambda i, bi: (bi, i, 0, 0, 0))],
        out_shape=[jax.ShapeDtypeStruct((b, MLA_HEADS, MLA_HEAD_PAD, s), BF16),
                   jax.ShapeDtypeStruct((b, MLA_HEADS, s, MLA_HEAD_PAD), BF16),
                   jax.ShapeDtypeStruct((b, s // tm, MLA_HEADS, MLA_V, tm), BF16)],
        compiler_params=_cparams(2),
        name="mla_prep",
    )(h, wa, g_q.reshape(1, -1), g_kv.reshape(1, -1), wqt, wk, wvt, *qtabs_t, *ktabs)


MLA_ONES_ROWS = 16


def _mla_attn_kernel(q_ref, k_ref, vt_ref, o_ref, s0_ref, s1_ref, ot_ref):
    tq = q_ref.shape[2]
    n_chunks, _, _, kc = vt_ref.shape
    ones = jnp.ones((MLA_ONES_ROWS, kc), BF16)
    acc_rows = MLA_V + MLA_ONES_ROWS
    s_refs = (s0_ref, s1_ref)

    def stage(h_score, h_value, m_value, parity):
        def body(j, carry):
            mx, acc = carry
            start = pl.multiple_of(j * kc, kc)
            if h_score is not None:
                st = jnp.dot(k_ref[h_score, pl.ds(start, kc), :], q_ref[h_score],
                             preferred_element_type=F32)
                s_refs[parity][pl.ds(start, kc), :] = st
                mx = jnp.maximum(mx, jnp.max(st.reshape(kc // 8, 8, tq), axis=0))
            if h_value is not None:
                p = jnp.exp2(s_refs[1 - parity][pl.ds(start, kc), :] - m_value).astype(BF16)
                vt = jnp.concatenate([vt_ref[j, h_value], ones], axis=0)
                acc = acc + jnp.dot(vt, p, preferred_element_type=F32)
            return mx, acc

        init = (jnp.full((8, tq), NEG_INF, F32), jnp.zeros((acc_rows, tq), F32))
        mx, acc = lax.fori_loop(0, n_chunks, body, init, unroll=MLA_UNROLL)
        if h_value is not None:
            ot_ref[h_value] = acc[:MLA_V] / acc[MLA_V:MLA_V + 1]
        return jnp.max(mx, axis=0, keepdims=True)

    def stage_pair(u, m):
        m = stage(2 * u + 1, 2 * u, m, 1)
        return stage(2 * u + 2, 2 * u + 1, m, 0)

    m = stage(0, None, None, 0)
    m = lax.fori_loop(0, MLA_HEADS // 2 - 1, stage_pair, m)
    m = stage(MLA_HEADS - 1, MLA_HEADS - 2, m, 1)
    stage(None, MLA_HEADS - 1, m, 0)
    for pr in range(MLA_HEADS // 2):
        pair = jnp.concatenate([ot_ref[2 * pr], ot_ref[2 * pr + 1]], axis=0)
        o_ref[:, 2 * pr * MLA_V:(2 * pr + 2) * MLA_V] = pair.T.astype(BF16)


def _mla_attn(qt, k, vt):
    b, _, s, _ = k.shape
    n_chunks, kc = vt.shape[1], vt.shape[4]
    tq = min(MLA_TQ, s)
    return pl.pallas_call(
        _mla_attn_kernel,
        grid=(b, s // tq),
        in_specs=[pl.BlockSpec((None, MLA_HEADS, MLA_HEAD_PAD, tq), lambda bi, i: (bi, 0, 0, i)),
                  pl.BlockSpec((None, MLA_HEADS, s, MLA_HEAD_PAD), lambda bi, i: (bi, 0, 0, 0)),
                  pl.BlockSpec((None, n_chunks, MLA_HEADS, MLA_V, kc), lambda bi, i: (bi, 0, 0, 0, 0))],
        out_specs=pl.BlockSpec((None, tq, MLA_WIDTH), lambda bi, i: (bi, i, 0)),
        out_shape=jax.ShapeDtypeStruct((b, s, MLA_WIDTH), BF16),
        scratch_shapes=[pltpu.VMEM((s, tq), F32), pltpu.VMEM((s, tq), F32), pltpu.VMEM((MLA_HEADS, MLA_V, tq), F32)],
        compiler_params=_cparams(2),
        name="mla_attn",
    )(qt, k, vt)


def _dil_attn_kernel(qkv_ref, o_ref, lse_ref, s_ref, *, tq, win):
    d, length, _ = qkv_ref.shape
    tiles = length // tq
    n_total = d * tiles
    heads = range(DIL_HPG)
    key_rel = lax.broadcasted_iota(jnp.int32, (tq, win), 1) - lax.broadcasted_iota(jnp.int32, (tq, win), 0)
    lane = lax.broadcasted_iota(jnp.int32, (tq, LANES), 1)
    low_half = lane < DIL_HD
    qk_head = (lane // (DIL_HD // 2)) % 2
    head_lanes = (qk_head == 0, qk_head == 1)

    def coords(n):
        r = n // tiles
        q0 = pl.multiple_of((n - r * tiles) * tq, tq)
        start = pl.multiple_of(jnp.clip(q0 - DIL_BAND, 0, length - win), DIL_BAND)
        return r, q0, start

    def score_step(n):
        r, q0, start = coords(n)
        valid = jnp.abs(key_rel + (start - q0)) <= DIL_BAND
        ms = []
        for h in heads:
            cols = slice((h // 2) * LANES, (h // 2 + 1) * LANES)
            q = qkv_ref[r, pl.ds(q0, tq), cols]
            k = qkv_ref[r, pl.ds(start, win), DIL_WIDTH + cols.start:DIL_WIDTH + cols.stop]
            qm = jnp.where(head_lanes[h % 2], q, jnp.zeros_like(q))
            sc = jnp.where(valid, _nt_dot(qm, k), NEG_INF)
            s_ref[n % 2, h] = sc
            ms.append(jnp.max(sc, axis=-1, keepdims=True))
        return tuple(ms)

    def value_step(n, ms):
        r, q0, start = coords(n)
        res = []
        for h in heads:
            cols = slice(2 * DIL_WIDTH + (h // 2) * LANES, 2 * DIL_WIDTH + (h // 2 + 1) * LANES)
            p = jnp.exp2(s_ref[n % 2, h] - ms[h])
            l = jnp.sum(p, axis=-1, keepdims=True)
            o = jnp.dot(p.astype(BF16), qkv_ref[r, pl.ds(start, win), cols], preferred_element_type=F32) / l
            res.append((o, ms[h] + jnp.log2(l)))
        for pr in range(DIL_HPG // 2):
            cols = slice(pr * LANES, (pr + 1) * LANES)
            o_ref[r, pl.ds(q0, tq), cols] = jnp.where(low_half, res[2 * pr][0], res[2 * pr + 1][0]).astype(BF16)
            lse_ref[r, pl.ds(q0, tq), cols] = jnp.where(low_half, res[2 * pr][1], res[2 * pr + 1][1])

    def body(n, ms):
        ms_next = score_step(jnp.minimum(n + 1, n_total - 1))
        value_step(n, ms)
        return ms_next

    lax.fori_loop(0, n_total, body, score_step(0), unroll=4)


def _dil_attn(zb, d):
    b, _, length, _ = zb.shape
    tq = min(2 * DIL_BAND, length)
    win = min(4 * DIL_BAND, length)
    whole = lambda width: pl.BlockSpec((None, d, length, width), lambda bi: (bi, 0, 0, 0))
    return pl.pallas_call(
        functools.partial(_dil_attn_kernel, tq=tq, win=win),
        grid=(b,),
        in_specs=[whole(ZB_GROUP)],
        out_specs=[whole(DIL_WIDTH), whole(DIL_WIDTH)],
        out_shape=[jax.ShapeDtypeStruct((b, d, length, DIL_WIDTH), BF16),
                   jax.ShapeDtypeStruct((b, d, length, DIL_WIDTH), F32)],
        scratch_shapes=[pltpu.VMEM((2, DIL_HPG, tq, win), F32)],
        compiler_params=_cparams(1),
        name=f"dil_attn_d{d}",
    )(zb)


NA_QROWS = 4
NA_UROWS = NA_QROWS + NA_KH
NA_QTOK = NA_QROWS * GRID_W
NA_UTOK = NA_UROWS * GRID_W


def _proj_na_kernel(h_ref, wk_ref, wqvt_ref, qt_ref, k_ref, vt_ref):
    h = h_ref[...]
    k_ref[...] = jnp.dot(h, wk_ref[...], preferred_element_type=F32).astype(BF16)
    qvt = _nt_dot(wqvt_ref[...], h)
    qt_ref[...] = (qvt[:NA_WIDTH] * LOG2_E).astype(BF16)
    for c in range(vt_ref.shape[0]):
        vt_ref[c] = qvt[NA_WIDTH:, c * NA_QTOK:(c + 1) * NA_QTOK].astype(BF16)


def _proj_na(h, wk, wqvt):
    b, s, _ = h.shape
    tm = TM_IN
    return pl.pallas_call(
        _proj_na_kernel,
        grid=(b, s // tm),
        in_specs=[pl.BlockSpec((None, tm, D_MODEL), lambda bi, i: (bi, i, 0)),
                  pl.BlockSpec((D_MODEL, NA_WIDTH), lambda bi, i: (0, 0)),
                  pl.BlockSpec((2 * NA_WIDTH, D_MODEL), lambda bi, i: (0, 0))],
        out_specs=[pl.BlockSpec((None, NA_WIDTH, tm), lambda bi, i: (bi, 0, i)),
                   pl.BlockSpec((None, tm, NA_WIDTH), lambda bi, i: (bi, i, 0)),
                   pl.BlockSpec((None, tm // NA_QTOK, NA_WIDTH, NA_QTOK), lambda bi, i: (bi, i, 0, 0))],
        out_shape=[jax.ShapeDtypeStruct((b, NA_WIDTH, s), BF16),
                   jax.ShapeDtypeStruct((b, s, NA_WIDTH), BF16),
                   jax.ShapeDtypeStruct((b, s // NA_QTOK, NA_WIDTH, NA_QTOK), BF16)],
        compiler_params=_cparams(2),
        name="proj_na",
    )(h, wk, wqvt)


NA_VARIANTS = ((0, lambda i: 0), (-NA_KH // 2, lambda i: i), (-NA_KH, lambda i: NA_KH // 2))


def _na_bias_kernel(rpb_ref, t_ref):
    h = pl.program_id(0)
    shape = (GRID_W, LANES)
    w = lax.broadcasted_iota(jnp.int32, shape, 0)
    lane = lax.broadcasted_iota(jnp.int32, shape, 1)
    c = lane & (GRID_W - 1)
    first = lane < GRID_W
    cs = jnp.clip(c - NA_KW // 2, 0, GRID_W - NA_KW)
    inside = (w >= cs) & (w < cs + NA_KW)
    off = w - c + NA_KW - 1
    neg = jnp.full(shape, NEG_INF, F32)
    tiles = {(None, None): neg}
    for v, (delta, lo) in enumerate(NA_VARIANTS):
        for jk in range(NA_UROWS):
            for ip in range(NA_QROWS // 2):
                ro = []
                for i in (2 * ip, 2 * ip + 1):
                    ok = lo(i) <= jk < lo(i) + NA_KH
                    ro.append(jk - i + NA_KH - 1 + delta if ok else None)
                ro = tuple(ro)
                if ro not in tiles:
                    def body(kk, acc, ro=ro):
                        a = NEG_INF if ro[0] is None else rpb_ref[h, ro[0], kk]
                        b = NEG_INF if ro[1] is None else rpb_ref[h, ro[1], kk]
                        return jnp.where(off == kk, jnp.where(first, a, b), acc)

                    acc = lax.fori_loop(0, 2 * NA_KW - 1, body, neg)
                    tiles[ro] = jnp.where(inside, acc * LOG2_E, NEG_INF)
                t_ref[v, jk * GRID_W:(jk + 1) * GRID_W, ip * LANES:(ip + 1) * LANES] = tiles[ro]


def _na_bias(rpb_l):
    nv = len(NA_VARIANTS)
    return pl.pallas_call(
        _na_bias_kernel,
        grid=(NA_HEADS,),
        in_specs=[pl.BlockSpec(memory_space=pltpu.SMEM)],
        out_specs=pl.BlockSpec((nv, None, NA_UTOK, NA_QTOK), lambda h: (0, h, 0, 0)),
        out_shape=jax.ShapeDtypeStruct((nv, NA_HEADS, NA_UTOK, NA_QTOK), F32),
        compiler_params=_cparams(1),
        name="na_bias",
    )(rpb_l)


def _na_attn_kernel(q_ref, k_ref, vt_ref, t_ref, o_ref, s_ref, *, nblk):
    c0 = jnp.clip(pl.program_id(1) - 1, 0, nblk - NA_UROWS // NA_QROWS)
    k0 = pl.multiple_of(c0 * NA_QTOK, NA_QTOK)
    row = lax.broadcasted_iota(jnp.int32, (LANES, NA_QTOK), 0)
    head_rows = (row < NA_HD, row >= NA_HD)
    ones = jnp.ones((MLA_ONES_ROWS, NA_UTOK), BF16)

    def scores(h):
        cols = slice((h // 2) * LANES, (h // 2 + 1) * LANES)
        qt = q_ref[cols, :]
        qm = jnp.where(head_rows[h % 2], qt, jnp.zeros_like(qt))
        st = jnp.dot(k_ref[pl.ds(k0, NA_UTOK), cols], qm, preferred_element_type=F32) + t_ref[h]
        s_ref[h % 2] = st
        return jnp.max(jnp.max(st.reshape(NA_UTOK // 8, 8, NA_QTOK), axis=0), axis=0, keepdims=True)

    def values(h, m):
        pt = jnp.exp2(s_ref[h % 2] - m).astype(BF16)
        vt = jnp.concatenate([vt_ref[c0 + c, h * NA_HD:(h + 1) * NA_HD, :] for c in range(NA_UROWS // NA_QROWS)],
                             axis=1)
        acc = jnp.dot(jnp.concatenate([vt, ones], axis=0), pt, preferred_element_type=F32)
        return acc[:NA_HD] / acc[NA_HD:NA_HD + 1]

    m_next = scores(0)
    o_prev = None
    for h in range(NA_HEADS):
        m = m_next
        if h + 1 < NA_HEADS:
            m_next = scores(h + 1)
        o_t = values(h, m)
        if h % 2 == 1:
            o_ref[:, (h - 1) * NA_HD:(h + 1) * NA_HD] = jnp.concatenate([o_prev, o_t], axis=0).T.astype(BF16)
        o_prev = o_t


def _na_attn(qt, k, vt, table):
    b, s, _ = k.shape
    nblk = s // NA_QTOK
    last = nblk - 1
    variant = lambda bi, a: (jnp.where(a == 0, 0, jnp.where(a == last, 2, 1)), 0, 0, 0)
    return pl.pallas_call(
        functools.partial(_na_attn_kernel, nblk=nblk),
        grid=(b, nblk),
        in_specs=[pl.BlockSpec((None, NA_WIDTH, NA_QTOK), lambda bi, a: (bi, 0, a)),
                  pl.BlockSpec((None, s, NA_WIDTH), lambda bi, a: (bi, 0, 0)),
                  pl.BlockSpec((None, nblk, NA_WIDTH, NA_QTOK), lambda bi, a: (bi, 0, 0, 0)),
                  pl.BlockSpec((None, NA_HEADS, NA_UTOK, NA_QTOK), variant)],
        out_specs=pl.BlockSpec((None, NA_QTOK, NA_WIDTH), lambda bi, a: (bi, a, 0)),
        out_shape=jax.ShapeDtypeStruct((b, s, NA_WIDTH), BF16),
        scratch_shapes=[pltpu.VMEM((2, NA_UTOK, NA_QTOK), F32)],
        compiler_params=_cparams(2),
        name="na_attn",
    )(qt, k, vt, table)


def _out_kernel(*refs, emit_h):
    (x_ref, oa_ref, gates_ref, oc_ref, merge_ref,
     ob0_ref, ob1_ref, ob2_ref, ls0_ref, ls1_ref, ls2_ref,
     wpa_ref, wpb_ref, wpc_ref, wout_ref, gpost_ref, mod_ref) = refs[:17]
    rest = refs[17:]
    if emit_h:
        gnext_ref, modn_ref, y_ref, h_ref, so_ref, sl_ref = rest
    else:
        y_ref, so_ref, sl_ref = rest
    tm = x_ref.shape[0]

    for gi, (ob_ref, ls_ref, (_, d)) in enumerate(zip((ob0_ref, ob1_ref, ob2_ref), (ls0_ref, ls1_ref, ls2_ref),
                                                     DIL_PAIRS)):
        for r in range(d):
            rows = slice(None) if d == 1 else pl.ds(r, tm // d, stride=d)
            for c in range(DIL_WIDTH // LANES):
                cols = slice(c * LANES, (c + 1) * LANES)
                so_ref[gi, c, rows, :] = ob_ref[r, :, cols].astype(F32)
                sl_ref[gi, c, rows, :] = ls_ref[r, :, cols]
    slabs = range(DIL_WIDTH // LANES)
    lse = [jnp.concatenate([sl_ref[gi, c] for c in slabs], axis=1) for gi in range(3)]
    o_g = [jnp.concatenate([so_ref[gi, c] for c in slabs], axis=1) for gi in range(3)]
    mx = jnp.maximum(jnp.maximum(lse[0], lse[1]), lse[2])
    e = [jnp.exp2(x - mx) for x in lse]
    o_b = (e[0] * o_g[0] + e[1] * o_g[1] + e[2] * o_g[2]) / (e[0] + e[1] + e[2])

    def gated(o, lo, hi):
        g = gates_ref[:, lo:hi].astype(F32)
        return (o * (g * (1.0 + jnp.tanh(g)))).astype(BF16)

    def merged(idx, act, w_ref):
        t = jnp.tanh(merge_ref[:, idx * D_MODEL:(idx + 1) * D_MODEL].astype(F32))
        return (1.0 + t) * jnp.dot(act, w_ref[...], preferred_element_type=F32)

    a = gated(oa_ref[...].astype(F32), 0, MLA_WIDTH)
    bb = gated(o_b, MLA_WIDTH, MLA_WIDTH + DIL_WIDTH)
    c = gated(oc_ref[...].astype(F32), MLA_WIDTH + DIL_WIDTH, GATE_WIDTH)
    mixed = merged(0, a, wpa_ref) + merged(1, bb, wpb_ref) + merged(2, c, wpc_ref)
    out = jnp.dot(mixed.astype(BF16), wout_ref[...], preferred_element_type=F32)
    gain = gpost_ref[...] * mod_ref[2:3, :]
    y = x_ref[...] + (out * lax.rsqrt(jnp.mean(out * out, axis=-1, keepdims=True) + NORM_EPS)) * gain
    y_ref[...] = y
    if emit_h:
        h_ref[...] = _modulated_norm(y, gnext_ref[...], modn_ref[...]).astype(BF16)


def _out_layer(x, oa, gates, oc, merge, obs, lses, wpa, wpb, wpc, wout, g_post, mod, g_next, mod_next):
    b, s, _ = x.shape
    tm = TM_OUT
    emit_h = g_next is not None
    tok = lambda n: pl.BlockSpec((None, tm, n), lambda bi, i: (bi, i, 0))
    const = lambda shape: pl.BlockSpec(shape, lambda bi, i: (0,) * len(shape))
    modspec = pl.BlockSpec((None, 3, D_MODEL), lambda bi, i: (bi, 0, 0))
    cls = [pl.BlockSpec((None, d, tm // d, DIL_WIDTH), lambda bi, i: (bi, 0, i, 0)) for _, d in DIL_PAIRS]
    in_specs = [tok(D_MODEL), tok(MLA_WIDTH), tok(GATE_WIDTH), tok(NA_WIDTH), tok(3 * D_MODEL),
                *cls, *cls,
                const((MLA_WIDTH, D_MODEL)), const((DIL_WIDTH, D_MODEL)), const((NA_WIDTH, D_MODEL)),
                const((D_MODEL, D_MODEL)), const((1, D_MODEL)), modspec]
    args = [x, oa, gates, oc, merge, *obs, *lses, wpa, wpb, wpc, wout, g_post.reshape(1, D_MODEL), mod]
    out_specs = [tok(D_MODEL)]
    out_shape = [jax.ShapeDtypeStruct((b, s, D_MODEL), F32)]
    if emit_h:
        in_specs += [const((1, D_MODEL)), modspec]
        args += [g_next.reshape(1, D_MODEL), mod_next]
        out_specs.append(tok(D_MODEL))
        out_shape.append(jax.ShapeDtypeStruct((b, s, D_MODEL), BF16))
    res = pl.pallas_call(
        functools.partial(_out_kernel, emit_h=emit_h),
        grid=(b, s // tm),
        in_specs=in_specs,
        out_specs=out_specs,
        out_shape=out_shape,
        scratch_shapes=[pltpu.VMEM((3, DIL_WIDTH // LANES, tm, LANES), F32)] * 2,
        compiler_params=_cparams(2),
        name="out_layer",
    )(*args)
    return (res[0], res[1]) if emit_h else (res[0], None)


def _rope_tables(s, head_dim, lane_dim, scale):
    half = head_dim // 2
    inv = ROPE_THETA ** (-jnp.arange(half, dtype=F32) * 2.0 / head_dim)
    ang = jnp.arange(s, dtype=F32)[:, None] * inv[None, :]
    cos, sin = jnp.cos(ang), jnp.sin(ang)
    lane_dim = np.asarray(lane_dim)
    in_rope = lane_dim >= 0
    idx = np.where(in_rope, lane_dim % half, 0)
    sign = np.where(lane_dim < half, -1.0, 1.0).astype(np.float32)
    cos_t = jnp.where(in_rope[None, :], cos[:, idx], 1.0) * scale
    sin_t = jnp.where(in_rope[None, :], sin[:, idx] * sign[None, :], 0.0) * scale
    return cos_t.astype(F32), sin_t.astype(F32)


_LANE = np.arange(LANES)
DIL_LANE_DIM = _LANE % (DIL_HD // 2) + (DIL_HD // 2) * (_LANE // (LANES // 2))
MLA_LANE_DIM = np.where(_LANE < MLA_ROPE // 2, _LANE,
                        np.where((_LANE >= LANES // 2) & (_LANE < LANES // 2 + MLA_ROPE // 2),
                                 _LANE - LANES // 2 + MLA_ROPE // 2, -1))


def _pair_rotary_order(w):
    lead = w.shape[:-1]
    w = w.reshape(*lead, DIL_HPG // 2, 2, 2, DIL_HD // 2)
    return jnp.swapaxes(w, -3, -2).reshape(*lead, DIL_WIDTH)


def _mla_head_slab(nope, rope):
    lead = (nope if nope is not None else rope).shape[:-1]
    dtype = (nope if nope is not None else rope).dtype
    z = lambda n: jnp.zeros(lead + (n,), dtype)
    r = MLA_ROPE // 2
    n0 = LANES // 2 - r
    parts = [z(r) if rope is None else rope[..., :r],
             z(n0) if nope is None else nope[..., :n0],
             z(r) if rope is None else rope[..., r:],
             z(MLA_NOPE - n0) if nope is None else nope[..., n0:],
             z(LANES - MLA_NOPE - MLA_ROPE)]
    return jnp.concatenate(parts, axis=-1)


def _layout_weights(w_in, w_uq, w_ukv):
    w_in, w_uq, w_ukv = w_in.astype(BF16), w_uq.astype(BF16), w_ukv.astype(BF16)
    c = [0] + [int(v) for v in _CUTS]
    cq, ckv, kr, gate_a, qkv_b, gate_b, qkv_c, gate_c, merge = [w_in[:, :, c[i]:c[i + 1]] for i in range(9)]
    w_a = jnp.concatenate([cq, ckv, _mla_head_slab(None, kr)], axis=-1)
    w_g = 0.5 * jnp.concatenate([gate_a, gate_b, gate_c], axis=-1)
    merge = 0.5 * merge
    qkv_b = qkv_b.reshape(DEPTH, D_MODEL, 3, len(DIL_PAIRS), DIL_WIDTH)
    qkv_b = qkv_b * jnp.array([DIL_HD ** -0.5, 1.0, 1.0], w_in.dtype)[None, None, :, None, None]
    qkv_b = jnp.concatenate([_pair_rotary_order(qkv_b[:, :, :2]), qkv_b[:, :, 2:]], axis=2)
    w_b = qkv_b.transpose(0, 1, 3, 2, 4).reshape(DEPTH, D_MODEL, ZB_WIDTH)
    qkv_c = qkv_c.reshape(DEPTH, D_MODEL, 3, NA_WIDTH)
    qkv_c = qkv_c * jnp.array([NA_HD ** -0.5, 1.0, 1.0], w_in.dtype)[None, None, :, None]
    w_ck = qkv_c[:, :, 1]
    w_cqvt = jnp.concatenate([qkv_c[:, :, 0], qkv_c[:, :, 2]], axis=-1).transpose(0, 2, 1)
    uq = w_uq.reshape(DEPTH, MLA_Q_RANK, MLA_HEADS, MLA_DQK)
    w_q = _mla_head_slab(uq[..., :MLA_NOPE], uq[..., MLA_NOPE:]).reshape(DEPTH, MLA_Q_RANK, MLA_HEADS * MLA_HEAD_PAD)
    ukv = w_ukv.reshape(DEPTH, MLA_KV_RANK, MLA_HEADS, MLA_NOPE + MLA_V)
    w_k = _mla_head_slab(ukv[..., :MLA_NOPE], None).reshape(DEPTH, MLA_KV_RANK, MLA_HEADS * MLA_HEAD_PAD)
    w_vt = ukv[..., MLA_NOPE:].reshape(DEPTH, MLA_KV_RANK, MLA_WIDTH).transpose(0, 2, 1)
    bf = lambda w: w.astype(BF16)
    return dict(a=bf(w_a), g=bf(w_g), b=bf(w_b), ck=bf(w_ck), cqvt=bf(w_cqvt), m=bf(merge),
                qt=bf(w_q).transpose(0, 2, 1), k=bf(w_k), vt=bf(w_vt))


def _trunk(x, c, p):
    b, s, _ = x.shape
    mods = _ada(c, p["w_ada"], p["b_ada"]).reshape(DEPTH, b, 3, D_MODEL)
    dil_tabs = _rope_tables(s, DIL_HD, DIL_LANE_DIM, 1.0)
    q_tabs_t = tuple(t.T for t in _rope_tables(s, MLA_ROPE, MLA_LANE_DIM, MLA_DQK ** -0.5 * LOG2_E))
    k_tabs = _rope_tables(s, MLA_ROPE, MLA_LANE_DIM, 1.0)
    w = p["w"]
    h = _prenorm(x, p["g_pre"][0], mods[0])
    for l in range(DEPTH):
        gates = _proj(h, w["g"][l], "proj_gates")
        qc, kc, vtc = _proj_na(h, w["ck"][l], w["cqvt"][l])
        merge = _proj(h, w["m"][l], "proj_merge")
        zbs = _proj_dil(h, w["b"][l], dil_tabs)
        q, k, vt = _mla_prep(h, w["a"][l], p["g_q"][l], p["g_kv"][l], w["qt"][l], w["k"][l], w["vt"][l],
                             q_tabs_t, k_tabs)
        oa = _mla_attn(q, k, vt)
        dil = [_dil_attn(zb, d) for zb, (_, d) in zip(zbs, DIL_PAIRS)]
        oc = _na_attn(qc, kc, vtc, p["na_bias"][l])
        last = l == DEPTH - 1
        x, h = _out_layer(x, oa, gates, oc, merge, [o for o, _ in dil], [ls for _, ls in dil],
                          p["w_pa"][l], p["w_pb"][l], p["w_pc"][l], p["w_out"][l], p["g_post"][l], mods[l],
                          None if last else p["g_pre"][l + 1], None if last else mods[l + 1])
    return x


def _prepare(w_ada, b_ada, g_pre, g_post, w_in, g_q, w_uq, g_kv, w_ukv, rpb, w_pa, w_pb, w_pc, w_out):
    return dict(w_ada=w_ada.astype(BF16), b_ada=b_ada, g_pre=g_pre, g_post=g_post, g_q=g_q, g_kv=g_kv,
                w=_layout_weights(w_in, w_uq, w_ukv),
                na_bias=[_na_bias(rpb[l]) for l in range(DEPTH)],
                w_pa=(0.5 * w_pa).astype(BF16), w_pb=(0.5 * w_pb).astype(BF16), w_pc=(0.5 * w_pc).astype(BF16),
                w_out=w_out.astype(BF16))


def kernel(x_prompt, x_sample, c_prompt, c_sample, w_ada, b_ada, g_pre, g_post, w_in, g_q, w_uq, g_kv, w_ukv, rpb,
           w_pa, w_pb, w_pc, w_out):
    p = _prepare(w_ada, b_ada, g_pre, g_post, w_in, g_q, w_uq, g_kv, w_ukv, rpb, w_pa, w_pb, w_pc, w_out)
    return (_trunk(x_prompt, c_prompt, p), _trunk(x_sample, c_sample, p))
```

```python
import functools

import jax
import jax.numpy as jnp
import numpy as np
from jax import lax
from jax.experimental import pallas as pl
from jax.experimental.pallas import tpu as pltpu

F32 = jnp.float32
BF16 = jnp.bfloat16

D_MODEL = 1024
DEPTH = 4
GRID_W = 64
ROPE_THETA = 10000.0
NORM_EPS = 1e-6
NEG_INF = -1e30
LOG2_E = float(np.log2(np.e))

MLA_HEADS = 8
MLA_NOPE = 64
MLA_ROPE = 32
MLA_V = 64
MLA_DQK = MLA_NOPE + MLA_ROPE
MLA_Q_RANK = 384
MLA_KV_RANK = 256
MLA_WIDTH = MLA_HEADS * MLA_V
MLA_HEAD_PAD = 128

DIL_PAIRS = ((128, 1), (512, 4), (2048, 16))
DIL_HPG = 4
DIL_HD = 64
DIL_HEADS = 12
DIL_WIDTH = DIL_HPG * DIL_HD
DIL_BAND = 64

NA_HEADS = 8
NA_HD = 64
NA_KH = 8
NA_KW = 16
NA_WIDTH = NA_HEADS * NA_HD

LANES = 128
VMEM_LIMIT = 56 * 1024 * 1024

_CUTS = np.cumsum((MLA_Q_RANK, MLA_KV_RANK, MLA_ROPE, MLA_WIDTH, 3 * DIL_HEADS * DIL_HD, DIL_WIDTH,
                   3 * NA_HEADS * NA_HD, NA_WIDTH, 3 * D_MODEL))
ZA_WIDTH = MLA_Q_RANK + MLA_KV_RANK + LANES
GATE_WIDTH = MLA_WIDTH + DIL_WIDTH + NA_WIDTH
ZB_WIDTH = 3 * DIL_HEADS * DIL_HD
ZB_GROUP = 3 * DIL_WIDTH
ZC_WIDTH = 3 * NA_WIDTH

TM_IN = 512
TM_OUT = 512
MLA_TQ = 512
MLA_KC = 512
MLA_UNROLL = 8


def _cparams(n_grid):
    return pltpu.CompilerParams(dimension_semantics=("arbitrary",) * n_grid, vmem_limit_bytes=VMEM_LIMIT)


def _sigmoid(x):
    return 0.5 * (1.0 + jnp.tanh(0.5 * x))


def _rope_lanes(x, cos, sin):
    return x * cos + pltpu.roll(x, LANES // 2, 1) * sin


def _nt_dot(a, b):
    return lax.dot_general(a, b, (((1,), (1,)), ((), ())), preferred_element_type=F32)


def _ada_kernel(c_ref, w_ref, b_ref, o_ref):
    c = c_ref[...]
    c_act = (c * _sigmoid(c)).astype(BF16)
    o_ref[...] = jnp.dot(c_act, w_ref[...], preferred_element_type=F32) + b_ref[...]


def _ada(c, w_ada_bf, b_ada):
    b = c.shape[0]
    return pl.pallas_call(
        _ada_kernel,
        grid=(DEPTH,),
        in_specs=[pl.BlockSpec((b, D_MODEL), lambda l: (0, 0)),
                  pl.BlockSpec((None, D_MODEL, 3 * D_MODEL), lambda l: (l, 0, 0)),
                  pl.BlockSpec((None, 1, 3 * D_MODEL), lambda l: (l, 0, 0))],
        out_specs=pl.BlockSpec((None, b, 3 * D_MODEL), lambda l: (l, 0, 0)),
        out_shape=jax.ShapeDtypeStruct((DEPTH, b, 3 * D_MODEL), F32),
        compiler_params=_cparams(1),
        name="ada",
    )(c, w_ada_bf, b_ada.reshape(DEPTH, 1, 3 * D_MODEL))


def _modulated_norm(x32, g, mod):
    gain = g * (1.0 + mod[1:2, :])
    return (x32 * lax.rsqrt(jnp.mean(x32 * x32, axis=-1, keepdims=True) + NORM_EPS)) * gain + mod[0:1, :]


def _prenorm_kernel(x_ref, g_ref, mod_ref, h_ref):
    h_ref[...] = _modulated_norm(x_ref[...], g_ref[...], mod_ref[...]).astype(BF16)


def _prenorm(x, g, mod):
    b, s, _ = x.shape
    tm = TM_IN
    return pl.pallas_call(
        _prenorm_kernel,
        grid=(b, s // tm),
        in_specs=[pl.BlockSpec((None, tm, D_MODEL), lambda bi, i: (bi, i, 0)),
                  pl.BlockSpec((1, D_MODEL), lambda bi, i: (0, 0)),
                  pl.BlockSpec((None, 3, D_MODEL), lambda bi, i: (bi, 0, 0))],
        out_specs=pl.BlockSpec((None, tm, D_MODEL), lambda bi, i: (bi, i, 0)),
        out_shape=jax.ShapeDtypeStruct((b, s, D_MODEL), BF16),
        compiler_params=_cparams(2),
        name="prenorm",
    )(x, g.reshape(1, D_MODEL), mod)


def _proj_dil_kernel(h_ref, w_ref, cos_ref, sin_ref, o0_ref, o1_ref, o2_ref, z0_ref, z1_ref, z2_ref):
    tm = h_ref.shape[0]
    h = h_ref[...]
    cos, sin = cos_ref[...], sin_ref[...]
    slabs_per_part = DIL_WIDTH // LANES
    slabs_per_group = ZB_GROUP // LANES
    for gi, (o_ref, z_ref, (_, d)) in enumerate(zip((o0_ref, o1_ref, o2_ref), (z0_ref, z1_ref, z2_ref), DIL_PAIRS)):
        z = jnp.dot(h, w_ref[:, gi * ZB_GROUP:(gi + 1) * ZB_GROUP], preferred_element_type=F32)
        for c in range(slabs_per_group):
            x = z[:, c * LANES:(c + 1) * LANES]
            part = c // slabs_per_part
            if part != 2:
                x = _rope_lanes(x, cos, sin)
            if part == 0:
                x = x * LOG2_E
            z_ref[c] = x
        for r in range(d):
            rows = slice(None) if d == 1 else pl.ds(r, tm // d, stride=d)
            for c in range(slabs_per_group):
                o_ref[r, :, c * LANES:(c + 1) * LANES] = z_ref[c, rows, :].astype(BF16)


def _proj_dil(h, w, tabs):
    b, s, _ = h.shape
    tm = TM_IN
    cos, sin = tabs
    tab_spec = pl.BlockSpec((tm, LANES), lambda i, bi: (i, 0))
    out_shapes, out_specs = [], []
    for _, d in DIL_PAIRS:
        out_shapes.append(jax.ShapeDtypeStruct((b, d, s // d, ZB_GROUP), BF16))
        out_specs.append(pl.BlockSpec((None, d, tm // d, ZB_GROUP), lambda i, bi: (bi, 0, i, 0)))
    return pl.pallas_call(
        _proj_dil_kernel,
        grid=(s // tm, b),
        in_specs=[pl.BlockSpec((None, tm, D_MODEL), lambda i, bi: (bi, i, 0)),
                  pl.BlockSpec((D_MODEL, ZB_WIDTH), lambda i, bi: (0, 0)),
                  tab_spec, tab_spec],
        out_specs=out_specs,
        out_shape=out_shapes,
        scratch_shapes=[pltpu.VMEM((ZB_GROUP // LANES, tm, LANES), F32)] * 3,
        compiler_params=_cparams(2),
        name="proj_dil",
    )(h, w, cos, sin)


def _mla_prep_kernel(h_ref, wa_ref, gq_ref, gkv_ref, wqt_ref, wk_ref, wvt_ref,
                     cq_ref, sq_ref, ck_ref, sk_ref, q_ref, k_ref, vt_ref):
    def norm(x32, g):
        return (x32 * lax.rsqrt(jnp.mean(x32 * x32, axis=-1, keepdims=True) + NORM_EPS) * g).astype(BF16)

    za = jnp.dot(h_ref[...], wa_ref[...], preferred_element_type=F32)
    cqn = norm(za[:, 0:MLA_Q_RANK], gq_ref[...])
    ckvn = norm(za[:, MLA_Q_RANK:MLA_Q_RANK + MLA_KV_RANK], gkv_ref[...])
    kr = za[:, MLA_Q_RANK + MLA_KV_RANK:ZA_WIDTH]
    kr = _rope_lanes(kr, ck_ref[...], sk_ref[...])
    qt = _nt_dot(wqt_ref[...], cqn)
    k = jnp.dot(ckvn, wk_ref[...], preferred_element_type=F32)
    cq, sq = cq_ref[...], sq_ref[...]
    vt = _nt_dot(wvt_ref[...], ckvn).astype(BF16)
    half = MLA_HEAD_PAD // 2
    for h in range(MLA_HEADS):
        cols = slice(h * MLA_HEAD_PAD, (h + 1) * MLA_HEAD_PAD)
        x = qt[cols]
        q_ref[h] = (x * cq + jnp.concatenate([x[half:], x[:half]], axis=0) * sq).astype(BF16)
        k_ref[h] = (k[:, cols] + kr).astype(BF16)
        vt_ref[h] = vt[h * MLA_V:(h + 1) * MLA_V, :]


def _mla_prep(h, wa, g_q, g_kv, wqt, wk, wvt, qtabs_t, ktabs):
    b, s, _ = h.shape
    tm = MLA_KC
    hp = MLA_HEADS * MLA_HEAD_PAD
    tab_spec = pl.BlockSpec((tm, LANES), lambda i, bi: (i, 0))
    tab_t_spec = pl.BlockSpec((LANES, tm), lambda i, bi: (0, i))
    const = lambda shape: pl.BlockSpec(shape, lambda i, bi: (0,) * len(shape))
    return pl.pallas_call(
        _mla_prep_kernel,
        grid=(s // tm, b),
        in_specs=[pl.BlockSpec((None, tm, D_MODEL), lambda i, bi: (bi, i, 0)), const((D_MODEL, ZA_WIDTH)),
                  const((1, MLA_Q_RANK)), const((1, MLA_KV_RANK)),
                  const((hp, MLA_Q_RANK)), const((MLA_KV_RANK, hp)), const((MLA_WIDTH, MLA_KV_RANK)),
                  tab_t_spec, tab_t_spec, tab_spec, tab_spec],
        out_specs=[pl.BlockSpec((None, MLA_HEADS, MLA_HEAD_PAD, tm), lambda i, bi: (bi, 0, 0, i)),
                   pl.BlockSpec((None, MLA_HEADS, tm, MLA_HEAD_PAD), lambda i, bi: (bi, 0, i, 0)),
                   pl.BlockSpec((None, None, MLA_HEADS, MLA_V, tm), lambda i, bi: (bi, i, 0, 0, 0))],
        out_shape=[jax.ShapeDtypeStruct((b, MLA_HEADS, MLA_HEAD_PAD, s), BF16),
                   jax.ShapeDtypeStruct((b, MLA_HEADS, s, MLA_HEAD_PAD), BF16),
                   jax.ShapeDtypeStruct((b, s // tm, MLA_HEADS, MLA_V, tm), BF16)],
        compiler_params=_cparams(2),
        name="mla_prep",
    )(h, wa, g_q.reshape(1, -1), g_kv.reshape(1, -1), wqt, wk, wvt, *qtabs_t, *ktabs)


MLA_ONES_ROWS = 16


def _mla_attn_kernel(q_ref, k_ref, vt_ref, o_ref, s0_ref, s1_ref, ot_ref):
    tq = q_ref.shape[2]
    n_chunks, _, _, kc = vt_ref.shape
    ones = jnp.ones((MLA_ONES_ROWS, kc), BF16)
    acc_rows = MLA_V + MLA_ONES_ROWS
    s_refs = (s0_ref, s1_ref)

    def stage(h_score, h_value, m_value, parity):
        def body(j, carry):
            mx, acc = carry
            start = pl.multiple_of(j * kc, kc)
            if h_score is not None:
                st = jnp.dot(k_ref[h_score, pl.ds(start, kc), :], q_ref[h_score],
                             preferred_element_type=F32)
                s_refs[parity][pl.ds(start, kc), :] = st
                mx = jnp.maximum(mx, jnp.max(st.reshape(kc // 8, 8, tq), axis=0))
            if h_value is not None:
                p = jnp.exp2(s_refs[1 - parity][pl.ds(start, kc), :] - m_value).astype(BF16)
                vt = jnp.concatenate([vt_ref[j, h_value], ones], axis=0)
                acc = acc + jnp.dot(vt, p, preferred_element_type=F32)
            return mx, acc

        init = (jnp.full((8, tq), NEG_INF, F32), jnp.zeros((acc_rows, tq), F32))
        mx, acc = lax.fori_loop(0, n_chunks, body, init, unroll=MLA_UNROLL)
        if h_value is not None:
            ot_ref[h_value] = acc[:MLA_V] / acc[MLA_V:MLA_V + 1]
        return jnp.max(mx, axis=0, keepdims=True)

    def stage_pair(u, m):
        m = stage(2 * u + 1, 2 * u, m, 1)
        return stage(2 * u + 2, 2 * u + 1, m, 0)

    m = stage(0, None, None, 0)
    m = lax.fori_loop(0, MLA_HEADS // 2 - 1, stage_pair, m)
    m = stage(MLA_HEADS - 1, MLA_HEADS - 2, m, 1)
    stage(None, MLA_HEADS - 1, m, 0)
    for pr in range(MLA_HEADS // 2):
        pair = jnp.concatenate([ot_ref[2 * pr], ot_ref[2 * pr + 1]], axis=0)
        o_ref[:, 2 * pr * MLA_V:(2 * pr + 2) * MLA_V] = pair.T.astype(BF16)


def _mla_attn(qt, k, vt):
    b, _, s, _ = k.shape
    n_chunks, kc = vt.shape[1], vt.shape[4]
    tq = min(MLA_TQ, s)
    return pl.pallas_call(
        _mla_attn_kernel,
        grid=(b, s // tq),
        in_specs=[pl.BlockSpec((None, MLA_HEADS, MLA_HEAD_PAD, tq), lambda bi, i: (bi, 0, 0, i)),
                  pl.BlockSpec((None, MLA_HEADS, s, MLA_HEAD_PAD), lambda bi, i: (bi, 0, 0, 0)),
                  pl.BlockSpec((None, n_chunks, MLA_HEADS, MLA_V, kc), lambda bi, i: (bi, 0, 0, 0, 0))],
        out_specs=pl.BlockSpec((None, tq, MLA_WIDTH), lambda bi, i: (bi, i, 0)),
        out_shape=jax.ShapeDtypeStruct((b, s, MLA_WIDTH), BF16),
        scratch_shapes=[pltpu.VMEM((s, tq), F32), pltpu.VMEM((s, tq), F32), pltpu.VMEM((MLA_HEADS, MLA_V, tq), F32)],
        compiler_params=_cparams(2),
        name="mla_attn",
    )(qt, k, vt)


def _dil_attn_kernel(qkv_ref, o_ref, lse_ref, s_ref, *, tq, win):
    d, length, _ = qkv_ref.shape
    tiles = length // tq
    n_total = d * tiles
    heads = range(DIL_HPG)
    key_rel = lax.broadcasted_iota(jnp.int32, (tq, win), 1) - lax.broadcasted_iota(jnp.int32, (tq, win), 0)
    lane = lax.broadcasted_iota(jnp.int32, (tq, LANES), 1)
    low_half = lane < DIL_HD
    qk_head = (lane // (DIL_HD // 2)) % 2
    head_lanes = (qk_head == 0, qk_head == 1)

    def coords(n):
        r = n // tiles
        q0 = pl.multiple_of((n - r * tiles) * tq, tq)
        start = pl.multiple_of(jnp.clip(q0 - DIL_BAND, 0, length - win), DIL_BAND)
        return r, q0, start

    def score_step(n):
        r, q0, start = coords(n)
        valid = jnp.abs(key_rel + (start - q0)) <= DIL_BAND
        ms = []
        for h in heads:
            cols = slice((h // 2) * LANES, (h // 2 + 1) * LANES)
            q = qkv_ref[r, pl.ds(q0, tq), cols]
            k = qkv_ref[r, pl.ds(start, win), DIL_WIDTH + cols.start:DIL_WIDTH + cols.stop]
            qm = jnp.where(head_lanes[h % 2], q, jnp.zeros_like(q))
            sc = jnp.where(valid, _nt_dot(qm, k), NEG_INF)
            s_ref[n % 2, h] = sc
            ms.append(jnp.max(sc, axis=-1, keepdims=True))
        return tuple(ms)

    def value_step(n, ms):
        r, q0, start = coords(n)
        res = []
        for h in heads:
            cols = slice(2 * DIL_WIDTH + (h // 2) * LANES, 2 * DIL_WIDTH + (h // 2 + 1) * LANES)
            p = jnp.exp2(s_ref[n % 2, h] - ms[h])
            l = jnp.sum(p, axis=-1, keepdims=True)
            o = jnp.dot(p.astype(BF16), qkv_ref[r, pl.ds(start, win), cols], preferred_element_type=F32) / l
            res.append((o, ms[h] + jnp.log2(l)))
        for pr in range(DIL_HPG // 2):
            cols = slice(pr * LANES, (pr + 1) * LANES)
            o_ref[r, pl.ds(q0, tq), cols] = jnp.where(low_half, res[2 * pr][0], res[2 * pr + 1][0]).astype(BF16)
            lse_ref[r, pl.ds(q0, tq), cols] = jnp.where(low_half, res[2 * pr][1], res[2 * pr + 1][1])

    def body(n, ms):
        ms_next = score_step(jnp.minimum(n + 1, n_total - 1))
        value_step(n, ms)
        return ms_next

    lax.fori_loop(0, n_total, body, score_step(0), unroll=4)


def _dil_attn(zb, d):
    b, _, length, _ = zb.shape
    tq = min(2 * DIL_BAND, length)
    win = min(4 * DIL_BAND, length)
    whole = lambda width: pl.BlockSpec((None, d, length, width), lambda bi: (bi, 0, 0, 0))
    return pl.pallas_call(
        functools.partial(_dil_attn_kernel, tq=tq, win=win),
        grid=(b,),
        in_specs=[whole(ZB_GROUP)],
        out_specs=[whole(DIL_WIDTH), whole(DIL_WIDTH)],
        out_shape=[jax.ShapeDtypeStruct((b, d, length, DIL_WIDTH), BF16),
                   jax.ShapeDtypeStruct((b, d, length, DIL_WIDTH), F32)],
        scratch_shapes=[pltpu.VMEM((2, DIL_HPG, tq, win), F32)],
        compiler_params=_cparams(1),
        name=f"dil_attn_d{d}",
    )(zb)


NA_QROWS = 4
NA_UROWS = NA_QROWS + NA_KH
NA_QTOK = NA_QROWS * GRID_W
NA_UTOK = NA_UROWS * GRID_W


def _proj_na_kernel(h_ref, wk_ref, wqvt_ref, qt_ref, k_ref, vt_ref):
    h = h_ref[...]
    k_ref[...] = jnp.dot(h, wk_ref[...], preferred_element_type=F32).astype(BF16)
    qvt = _nt_dot(wqvt_ref[...], h)
    qt_ref[...] = (qvt[:NA_WIDTH] * LOG2_E).astype(BF16)
    for c in range(vt_ref.shape[0]):
        vt_ref[c] = qvt[NA_WIDTH:, c * NA_QTOK:(c + 1) * NA_QTOK].astype(BF16)


def _proj_na(h, wk, wqvt):
    b, s, _ = h.shape
    tm = TM_IN
    return pl.pallas_call(
        _proj_na_kernel,
        grid=(b, s // tm),
        in_specs=[pl.BlockSpec((None, tm, D_MODEL), lambda bi, i: (bi, i, 0)),
                  pl.BlockSpec((D_MODEL, NA_WIDTH), lambda bi, i: (0, 0)),
                  pl.BlockSpec((2 * NA_WIDTH, D_MODEL), lambda bi, i: (0, 0))],
        out_specs=[pl.BlockSpec((None, NA_WIDTH, tm), lambda bi, i: (bi, 0, i)),
                   pl.BlockSpec((None, tm, NA_WIDTH), lambda bi, i: (bi, i, 0)),
                   pl.BlockSpec((None, tm // NA_QTOK, NA_WIDTH, NA_QTOK), lambda bi, i: (bi, i, 0, 0))],
        out_shape=[jax.ShapeDtypeStruct((b, NA_WIDTH, s), BF16),
                   jax.ShapeDtypeStruct((b, s, NA_WIDTH), BF16),
                   jax.ShapeDtypeStruct((b, s // NA_QTOK, NA_WIDTH, NA_QTOK), BF16)],
        compiler_params=_cparams(2),
        name="proj_na",
    )(h, wk, wqvt)


NA_VARIANTS = ((0, lambda i: 0), (-NA_KH // 2, lambda i: i), (-NA_KH, lambda i: NA_KH // 2))


def _na_bias_kernel(rpb_ref, t_ref):
    h = pl.program_id(0)
    shape = (GRID_W, LANES)
    w = lax.broadcasted_iota(jnp.int32, shape, 0)
    lane = lax.broadcasted_iota(jnp.int32, shape, 1)
    c = lane & (GRID_W - 1)
    first = lane < GRID_W
    cs = jnp.clip(c - NA_KW // 2, 0, GRID_W - NA_KW)
    inside = (w >= cs) & (w < cs + NA_KW)
    off = w - c + NA_KW - 1
    neg = jnp.full(shape, NEG_INF, F32)
    tiles = {(None, None): neg}
    for v, (delta, lo) in enumerate(NA_VARIANTS):
        for jk in range(NA_UROWS):
            for ip in range(NA_QROWS // 2):
                ro = []
                for i in (2 * ip, 2 * ip + 1):
                    ok = lo(i) <= jk < lo(i) + NA_KH
                    ro.append(jk - i + NA_KH - 1 + delta if ok else None)
                ro = tuple(ro)
                if ro not in tiles:
                    def body(kk, acc, ro=ro):
                        a = NEG_INF if ro[0] is None else rpb_ref[h, ro[0], kk]
                        b = NEG_INF if ro[1] is None else rpb_ref[h, ro[1], kk]
                        return jnp.where(off == kk, jnp.where(first, a, b), acc)

                    acc = lax.fori_loop(0, 2 * NA_KW - 1, body, neg)
                    tiles[ro] = jnp.where(inside, acc * LOG2_E, NEG_INF)
                t_ref[v, jk * GRID_W:(jk + 1) * GRID_W, ip * LANES:(ip + 1) * LANES] = tiles[ro]


def _na_bias(rpb_l):
    nv = len(NA_VARIANTS)
    return pl.pallas_call(
        _na_bias_kernel,
        grid=(NA_HEADS,),
        in_specs=[pl.BlockSpec(memory_space=pltpu.SMEM)],
        out_specs=pl.BlockSpec((nv, None, NA_UTOK, NA_QTOK), lambda h: (0, h, 0, 0)),
        out_shape=jax.ShapeDtypeStruct((nv, NA_HEADS, NA_UTOK, NA_QTOK), F32),
        compiler_params=_cparams(1),
        name="na_bias",
    )(rpb_l)


def _na_attn_kernel(q_ref, k_ref, vt_ref, t_ref, o_ref, s_ref, *, nblk):
    c0 = jnp.clip(pl.program_id(1) - 1, 0, nblk - NA_UROWS // NA_QROWS)
    k0 = pl.multiple_of(c0 * NA_QTOK, NA_QTOK)
    row = lax.broadcasted_iota(jnp.int32, (LANES, NA_QTOK), 0)
    head_rows = (row < NA_HD, row >= NA_HD)
    ones = jnp.ones((MLA_ONES_ROWS, NA_UTOK), BF16)

    def scores(h):
        cols = slice((h // 2) * LANES, (h // 2 + 1) * LANES)
        qt = q_ref[cols, :]
        qm = jnp.where(head_rows[h % 2], qt, jnp.zeros_like(qt))
        st = jnp.dot(k_ref[pl.ds(k0, NA_UTOK), cols], qm, preferred_element_type=F32) + t_ref[h]
        s_ref[h % 2] = st
        return jnp.max(jnp.max(st.reshape(NA_UTOK // 8, 8, NA_QTOK), axis=0), axis=0, keepdims=True)

    def values(h, m):
        pt = jnp.exp2(s_ref[h % 2] - m).astype(BF16)
        vt = jnp.concatenate([vt_ref[c0 + c, h * NA_HD:(h + 1) * NA_HD, :] for c in range(NA_UROWS // NA_QROWS)],
                             axis=1)
        acc = jnp.dot(jnp.concatenate([vt, ones], axis=0), pt, preferred_element_type=F32)
        return acc[:NA_HD] / acc[NA_HD:NA_HD + 1]

    m_next = scores(0)
    o_prev = None
    for h in range(NA_HEADS):
        m = m_next
        if h + 1 < NA_HEADS:
            m_next = scores(h + 1)
        o_t = values(h, m)
        if h % 2 == 1:
            o_ref[:, (h - 1) * NA_HD:(h + 1) * NA_HD] = jnp.concatenate([o_prev, o_t], axis=0).T.astype(BF16)
        o_prev = o_t


def _na_attn(qt, k, vt, table):
    b, s, _ = k.shape
    nblk = s // NA_QTOK
    last = nblk - 1
    variant = lambda bi, a: (jnp.where(a == 0, 0, jnp.where(a == last, 2, 1)), 0, 0, 0)
    return pl.pallas_call(
        functools.partial(_na_attn_kernel, nblk=nblk),
        grid=(b, nblk),
        in_specs=[pl.BlockSpec((None, NA_WIDTH, NA_QTOK), lambda bi, a: (bi, 0, a)),
                  pl.BlockSpec((None, s, NA_WIDTH), lambda bi, a: (bi, 0, 0)),
                  pl.BlockSpec((None, nblk, NA_WIDTH, NA_QTOK), lambda bi, a: (bi, 0, 0, 0)),
                  pl.BlockSpec((None, NA_HEADS, NA_UTOK, NA_QTOK), variant)],
        out_specs=pl.BlockSpec((None, NA_QTOK, NA_WIDTH), lambda bi, a: (bi, a, 0)),
        out_shape=jax.ShapeDtypeStruct((b, s, NA_WIDTH), BF16),
        scratch_shapes=[pltpu.VMEM((2, NA_UTOK, NA_QTOK), F32)],
        compiler_params=_cparams(2),
        name="na_attn",
    )(qt, k, vt, table)


def _out_kernel(*refs, emit_h):
    (x_ref, oa_ref, hin_ref, oc_ref,
     ob0_ref, ob1_ref, ob2_ref, ls0_ref, ls1_ref, ls2_ref,
     wg_ref, wm_ref, wpa_ref, wpb_ref, wpc_ref, wout_ref, gpost_ref, mod_ref) = refs[:18]
    rest = refs[18:]
    if emit_h:
        gnext_ref, modn_ref, y_ref, h_ref, so_ref, sl_ref = rest
    else:
        y_ref, so_ref, sl_ref = rest
    tm = x_ref.shape[0]

    for gi, (ob_ref, ls_ref, (_, d)) in enumerate(zip((ob0_ref, ob1_ref, ob2_ref), (ls0_ref, ls1_ref, ls2_ref),
                                                     DIL_PAIRS)):
        for r in range(d):
            rows = slice(None) if d == 1 else pl.ds(r, tm // d, stride=d)
            for c in range(DIL_WIDTH // LANES):
                cols = slice(c * LANES, (c + 1) * LANES)
                so_ref[gi, c, rows, :] = ob_ref[r, :, cols].astype(F32)
                sl_ref[gi, c, rows, :] = ls_ref[r, :, cols]
    slabs = range(DIL_WIDTH // LANES)
    lse = [jnp.concatenate([sl_ref[gi, c] for c in slabs], axis=1) for gi in range(3)]
    o_g = [jnp.concatenate([so_ref[gi, c] for c in slabs], axis=1) for gi in range(3)]
    mx = jnp.maximum(jnp.maximum(lse[0], lse[1]), lse[2])
    e = [jnp.exp2(x - mx) for x in lse]
    o_b = (e[0] * o_g[0] + e[1] * o_g[1] + e[2] * o_g[2]) / (e[0] + e[1] + e[2])

    hin = hin_ref[...]

    def gated(o, lo, hi):
        g = jnp.dot(hin, wg_ref[:, lo:hi], preferred_element_type=F32)
        return (o * (g * (1.0 + jnp.tanh(g)))).astype(BF16)

    def merged(idx, act, w_ref):
        t = jnp.tanh(jnp.dot(hin, wm_ref[:, idx * D_MODEL:(idx + 1) * D_MODEL], preferred_element_type=F32))
        return (1.0 + t) * jnp.dot(act, w_ref[...], preferred_element_type=F32)

    a = gated(oa_ref[...].astype(F32), 0, MLA_WIDTH)
    bb = gated(o_b, MLA_WIDTH, MLA_WIDTH + DIL_WIDTH)
    c = gated(oc_ref[...].astype(F32), MLA_WIDTH + DIL_WIDTH, GATE_WIDTH)
    mixed = merged(0, a, wpa_ref) + merged(1, bb, wpb_ref) + merged(2, c, wpc_ref)
    out = jnp.dot(mixed.astype(BF16), wout_ref[...], preferred_element_type=F32)
    gain = gpost_ref[...] * mod_ref[2:3, :]
    y = x_ref[...] + (out * lax.rsqrt(jnp.mean(out * out, axis=-1, keepdims=True) + NORM_EPS)) * gain
    y_ref[...] = y
    if emit_h:
        h_ref[...] = _modulated_norm(y, gnext_ref[...], modn_ref[...]).astype(BF16)


def _out_layer(x, oa, h_in, oc, obs, lses, wg, wm, wpa, wpb, wpc, wout, g_post, mod, g_next, mod_next):
    b, s, _ = x.shape
    tm = TM_OUT
    emit_h = g_next is not None
    tok = lambda n: pl.BlockSpec((None, tm, n), lambda bi, i: (bi, i, 0))
    const = lambda shape: pl.BlockSpec(shape, lambda bi, i: (0,) * len(shape))
    modspec = pl.BlockSpec((None, 3, D_MODEL), lambda bi, i: (bi, 0, 0))
    cls = [pl.BlockSpec((None, d, tm // d, DIL_WIDTH), lambda bi, i: (bi, 0, i, 0)) for _, d in DIL_PAIRS]
    in_specs = [tok(D_MODEL), tok(MLA_WIDTH), tok(D_MODEL), tok(NA_WIDTH),
                *cls, *cls,
                const((D_MODEL, GATE_WIDTH)), const((D_MODEL, 3 * D_MODEL)), const((MLA_WIDTH, D_MODEL)), const((DIL_WIDTH, D_MODEL)), const((NA_WIDTH, D_MODEL)),
                const((D_MODEL, D_MODEL)), const((1, D_MODEL)), modspec]
    args = [x, oa, h_in, oc, *obs, *lses, wg, wm, wpa, wpb, wpc, wout, g_post.reshape(1, D_MODEL), mod]
    out_specs = [tok(D_MODEL)]
    out_shape = [jax.ShapeDtypeStruct((b, s, D_MODEL), F32)]
    if emit_h:
        in_specs += [const((1, D_MODEL)), modspec]
        args += [g_next.reshape(1, D_MODEL), mod_next]
        out_specs.append(tok(D_MODEL))
        out_shape.append(jax.ShapeDtypeStruct((b, s, D_MODEL), BF16))
    res = pl.pallas_call(
        functools.partial(_out_kernel, emit_h=emit_h),
        grid=(b, s // tm),
        in_specs=in_specs,
        out_specs=out_specs,
        out_shape=out_shape,
        scratch_shapes=[pltpu.VMEM((3, DIL_WIDTH // LANES, tm, LANES), F32)] * 2,
        compiler_params=_cparams(2),
        name="out_layer",
    )(*args)
    return (res[0], res[1]) if emit_h else (res[0], None)


def _rope_tables(s, head_dim, lane_dim, scale):
    half = head_dim // 2
    inv = ROPE_THETA ** (-jnp.arange(half, dtype=F32) * 2.0 / head_dim)
    ang = jnp.arange(s, dtype=F32)[:, None] * inv[None, :]
    cos, sin = jnp.cos(ang), jnp.sin(ang)
    lane_dim = np.asarray(lane_dim)
    in_rope = lane_dim >= 0
    idx = np.where(in_rope, lane_dim % half, 0)
    sign = np.where(lane_dim < half, -1.0, 1.0).astype(np.float32)
    cos_t = jnp.where(in_rope[None, :], cos[:, idx], 1.0) * scale
    sin_t = jnp.where(in_rope[None, :], sin[:, idx] * sign[None, :], 0.0) * scale
    return cos_t.astype(F32), sin_t.astype(F32)


_LANE = np.arange(LANES)
DIL_LANE_DIM = _LANE % (DIL_HD // 2) + (DIL_HD // 2) * (_LANE // (LANES // 2))
MLA_LANE_DIM = np.where(_LANE < MLA_ROPE // 2, _LANE,
                        np.where((_LANE >= LANES // 2) & (_LANE < LANES // 2 + MLA_ROPE // 2),
                                 _LANE - LANES // 2 + MLA_ROPE // 2, -1))


def _pair_rotary_order(w):
    lead = w.shape[:-1]
    w = w.reshape(*lead, DIL_HPG // 2, 2, 2, DIL_HD // 2)
    return jnp.swapaxes(w, -3, -2).reshape(*lead, DIL_WIDTH)


def _mla_head_slab(nope, rope):
    lead = (nope if nope is not None else rope).shape[:-1]
    dtype = (nope if nope is not None else rope).dtype
    z = lambda n: jnp.zeros(lead + (n,), dtype)
    r = MLA_ROPE // 2
    n0 = LANES // 2 - r
    parts = [z(r) if rope is None else rope[..., :r],
             z(n0) if nope is None else nope[..., :n0],
             z(r) if rope is None else rope[..., r:],
             z(MLA_NOPE - n0) if nope is None else nope[..., n0:],
             z(LANES - MLA_NOPE - MLA_ROPE)]
    return jnp.concatenate(parts, axis=-1)


def _layout_weights(w_in, w_uq, w_ukv):
    w_in, w_uq, w_ukv = w_in.astype(BF16), w_uq.astype(BF16), w_ukv.astype(BF16)
    c = [0] + [int(v) for v in _CUTS]
    cq, ckv, kr, gate_a, qkv_b, gate_b, qkv_c, gate_c, merge = [w_in[:, :, c[i]:c[i + 1]] for i in range(9)]
    w_a = jnp.concatenate([cq, ckv, _mla_head_slab(None, kr)], axis=-1)
    w_g = 0.5 * jnp.concatenate([gate_a, gate_b, gate_c], axis=-1)
    merge = 0.5 * merge
    qkv_b = qkv_b.reshape(DEPTH, D_MODEL, 3, len(DIL_PAIRS), DIL_WIDTH)
    qkv_b = qkv_b * jnp.array([DIL_HD ** -0.5, 1.0, 1.0], w_in.dtype)[None, None, :, None, None]
    qkv_b = jnp.concatenate([_pair_rotary_order(qkv_b[:, :, :2]), qkv_b[:, :, 2:]], axis=2)
    w_b = qkv_b.transpose(0, 1, 3, 2, 4).reshape(DEPTH, D_MODEL, ZB_WIDTH)
    qkv_c = qkv_c.reshape(DEPTH, D_MODEL, 3, NA_WIDTH)
    qkv_c = qkv_c * jnp.array([NA_HD ** -0.5, 1.0, 1.0], w_in.dtype)[None, None, :, None]
    w_ck = qkv_c[:, :, 1]
    w_cqvt = jnp.concatenate([qkv_c[:, :, 0], qkv_c[:, :, 2]], axis=-1).transpose(0, 2, 1)
    uq = w_uq.reshape(DEPTH, MLA_Q_RANK, MLA_HEADS, MLA_DQK)
    w_q = _mla_head_slab(uq[..., :MLA_NOPE], uq[..., MLA_NOPE:]).reshape(DEPTH, MLA_Q_RANK, MLA_HEADS * MLA_HEAD_PAD)
    ukv = w_ukv.reshape(DEPTH, MLA_KV_RANK, MLA_HEADS, MLA_NOPE + MLA_V)
    w_k = _mla_head_slab(ukv[..., :MLA_NOPE], None).reshape(DEPTH, MLA_KV_RANK, MLA_HEADS * MLA_HEAD_PAD)
    w_vt = ukv[..., MLA_NOPE:].reshape(DEPTH, MLA_KV_RANK, MLA_WIDTH).transpose(0, 2, 1)
    bf = lambda w: w.astype(BF16)
    return dict(a=bf(w_a), g=bf(w_g), b=bf(w_b), ck=bf(w_ck), cqvt=bf(w_cqvt), m=bf(merge),
                qt=bf(w_q).transpose(0, 2, 1), k=bf(w_k), vt=bf(w_vt))


def _trunk(x, c, p):
    b, s, _ = x.shape
    mods = _ada(c, p["w_ada"], p["b_ada"]).reshape(DEPTH, b, 3, D_MODEL)
    dil_tabs = _rope_tables(s, DIL_HD, DIL_LANE_DIM, 1.0)
    q_tabs_t = tuple(t.T for t in _rope_tables(s, MLA_ROPE, MLA_LANE_DIM, MLA_DQK ** -0.5 * LOG2_E))
    k_tabs = _rope_tables(s, MLA_ROPE, MLA_LANE_DIM, 1.0)
    w = p["w"]
    h = _prenorm(x, p["g_pre"][0], mods[0])
    for l in range(DEPTH):
        qc, kc, vtc = _proj_na(h, w["ck"][l], w["cqvt"][l])
        zbs = _proj_dil(h, w["b"][l], dil_tabs)
        q, k, vt = _mla_prep(h, w["a"][l], p["g_q"][l], p["g_kv"][l], w["qt"][l], w["k"][l], w["vt"][l],
                             q_tabs_t, k_tabs)
        oa = _mla_attn(q, k, vt)
        dil = [_dil_attn(zb, d) for zb, (_, d) in zip(zbs, DIL_PAIRS)]
        oc = _na_attn(qc, kc, vtc, p["na_bias"][l])
        last = l == DEPTH - 1
        x, h = _out_layer(x, oa, h, oc, [o for o, _ in dil], [ls for _, ls in dil], w["g"][l], w["m"][l],
                          p["w_pa"][l], p["w_pb"][l], p["w_pc"][l], p["w_out"][l], p["g_post"][l], mods[l],
                          None if last else p["g_pre"][l + 1], None if last else mods[l + 1])
    return x


def _prepare(w_ada, b_ada, g_pre, g_post, w_in, g_q, w_uq, g_kv, w_ukv, rpb, w_pa, w_pb, w_pc, w_out):
    return dict(w_ada=w_ada.astype(BF16), b_ada=b_ada, g_pre=g_pre, g_post=g_post, g_q=g_q, g_kv=g_kv,
                w=_layout_weights(w_in, w_uq, w_ukv),
                na_bias=[_na_bias(rpb[l]) for l in range(DEPTH)],
                w_pa=(0.5 * w_pa).astype(BF16), w_pb=(0.5 * w_pb).astype(BF16), w_pc=(0.5 * w_pc).astype(BF16),
                w_out=w_out.astype(BF16))


def kernel(x_prompt, x_sample, c_prompt, c_sample, w_ada, b_ada, g_pre, g_post, w_in, g_q, w_uq, g_kv, w_ukv, rpb,
           w_pa, w_pb, w_pc, w_out):
    p = _prepare(w_ada, b_ada, g_pre, g_post, w_in, g_q, w_uq, g_kv, w_ukv, rpb, w_pa, w_pb, w_pc, w_out)
    return (_trunk(x_prompt, c_prompt, p), _trunk(x_sample, c_sample, p))
```

```python
import functools

import jax
import jax.numpy as jnp
import numpy as np
from jax import lax
from jax.experimental import pallas as pl
from jax.experimental.pallas import tpu as pltpu

F32 = jnp.float32
BF16 = jnp.bfloat16

D_MODEL = 1024
DEPTH = 4
GRID_W = 64
ROPE_THETA = 10000.0
NORM_EPS = 1e-6
NEG_INF = -1e30
LOG2_E = float(np.log2(np.e))

MLA_HEADS = 8
MLA_NOPE = 64
MLA_ROPE = 32
MLA_V = 64
MLA_DQK = MLA_NOPE + MLA_ROPE
MLA_Q_RANK = 384
MLA_KV_RANK = 256
MLA_WIDTH = MLA_HEADS * MLA_V
MLA_HEAD_PAD = 128

DIL_PAIRS = ((128, 1), (512, 4), (2048, 16))
DIL_HPG = 4
DIL_HD = 64
DIL_HEADS = 12
DIL_WIDTH = DIL_HPG * DIL_HD
DIL_BAND = 64

NA_HEADS = 8
NA_HD = 64
NA_KH = 8
NA_KW = 16
NA_WIDTH = NA_HEADS * NA_HD

LANES = 128
VMEM_LIMIT = 56 * 1024 * 1024

_CUTS = np.cumsum((MLA_Q_RANK, MLA_KV_RANK, MLA_ROPE, MLA_WIDTH, 3 * DIL_HEADS * DIL_HD, DIL_WIDTH,
                   3 * NA_HEADS * NA_HD, NA_WIDTH, 3 * D_MODEL))
ZA_WIDTH = MLA_Q_RANK + MLA_KV_RANK + LANES
GATE_WIDTH = MLA_WIDTH + DIL_WIDTH + NA_WIDTH
ZB_WIDTH = 3 * DIL_HEADS * DIL_HD
ZB_GROUP = 3 * DIL_WIDTH
ZC_WIDTH = 3 * NA_WIDTH

TM_IN = 512
TM_OUT = 512
MLA_TQ = 512
MLA_KC = 512
MLA_UNROLL = 8


def _cparams(n_grid):
    return pltpu.CompilerParams(dimension_semantics=("arbitrary",) * n_grid, vmem_limit_bytes=VMEM_LIMIT)


def _sigmoid(x):
    return 0.5 * (1.0 + jnp.tanh(0.5 * x))


def _rope_lanes(x, cos, sin):
    return x * cos + pltpu.roll(x, LANES // 2, 1) * sin


def _nt_dot(a, b):
    return lax.dot_general(a, b, (((1,), (1,)), ((), ())), preferred_element_type=F32)


def _ada_kernel(c_ref, w_ref, b_ref, o_ref):
    c = c_ref[...]
    c_act = (c * _sigmoid(c)).astype(BF16)
    o_ref[...] = jnp.dot(c_act, w_ref[...], preferred_element_type=F32) + b_ref[...]


def _ada(c, w_ada_bf, b_ada):
    b = c.shape[0]
    return pl.pallas_call(
        _ada_kernel,
        grid=(DEPTH,),
        in_specs=[pl.BlockSpec((b, D_MODEL), lambda l: (0, 0)),
                  pl.BlockSpec((None, D_MODEL, 3 * D_MODEL), lambda l: (l, 0, 0)),
                  pl.BlockSpec((None, 1, 3 * D_MODEL), lambda l: (l, 0, 0))],
        out_specs=pl.BlockSpec((None, b, 3 * D_MODEL), lambda l: (l, 0, 0)),
        out_shape=jax.ShapeDtypeStruct((DEPTH, b, 3 * D_MODEL), F32),
        compiler_params=_cparams(1),
        name="ada",
    )(c, w_ada_bf, b_ada.reshape(DEPTH, 1, 3 * D_MODEL))


def _modulated_norm(x32, g, mod):
    gain = g * (1.0 + mod[1:2, :])
    return (x32 * lax.rsqrt(jnp.mean(x32 * x32, axis=-1, keepdims=True) + NORM_EPS)) * gain + mod[0:1, :]


def _prenorm_kernel(x_ref, g_ref, mod_ref, h_ref):
    h_ref[...] = _modulated_norm(x_ref[...], g_ref[...], mod_ref[...]).astype(BF16)


def _prenorm(x, g, mod):
    b, s, _ = x.shape
    tm = TM_IN
    return pl.pallas_call(
        _prenorm_kernel,
        grid=(b, s // tm),
        in_specs=[pl.BlockSpec((None, tm, D_MODEL), lambda bi, i: (bi, i, 0)),
                  pl.BlockSpec((1, D_MODEL), lambda bi, i: (0, 0)),
                  pl.BlockSpec((None, 3, D_MODEL), lambda bi, i: (bi, 0, 0))],
        out_specs=pl.BlockSpec((None, tm, D_MODEL), lambda bi, i: (bi, i, 0)),
        out_shape=jax.ShapeDtypeStruct((b, s, D_MODEL), BF16),
        compiler_params=_cparams(2),
        name="prenorm",
    )(x, g.reshape(1, D_MODEL), mod)


def _proj_dil_kernel(h_ref, w_ref, cos_ref, sin_ref, wk_ref, wqvt_ref, o0_ref, o1_ref, o2_ref,
                     qt_ref, k_ref, vt_ref, z0_ref, z1_ref, z2_ref):
    tm = h_ref.shape[0]
    h = h_ref[...]
    cos, sin = cos_ref[...], sin_ref[...]
    slabs_per_part = DIL_WIDTH // LANES
    slabs_per_group = ZB_GROUP // LANES
    for gi, (o_ref, z_ref, (_, d)) in enumerate(zip((o0_ref, o1_ref, o2_ref), (z0_ref, z1_ref, z2_ref), DIL_PAIRS)):
        z = jnp.dot(h, w_ref[:, gi * ZB_GROUP:(gi + 1) * ZB_GROUP], preferred_element_type=F32)
        for c in range(slabs_per_group):
            x = z[:, c * LANES:(c + 1) * LANES]
            part = c // slabs_per_part
            if part != 2:
                x = _rope_lanes(x, cos, sin)
            if part == 0:
                x = x * LOG2_E
            z_ref[c] = x
        for r in range(d):
            rows = slice(None) if d == 1 else pl.ds(r, tm // d, stride=d)
            for c in range(slabs_per_group):
                o_ref[r, :, c * LANES:(c + 1) * LANES] = z_ref[c, rows, :].astype(BF16)
    _na_projection(h, wk_ref, wqvt_ref, qt_ref, k_ref, vt_ref)


def _proj_dil(h, w, tabs, wk, wqvt):
    b, s, _ = h.shape
    tm = TM_IN
    cos, sin = tabs
    tab_spec = pl.BlockSpec((tm, LANES), lambda i, bi: (i, 0))
    out_shapes, out_specs = [], []
    for _, d in DIL_PAIRS:
        out_shapes.append(jax.ShapeDtypeStruct((b, d, s // d, ZB_GROUP), BF16))
        out_specs.append(pl.BlockSpec((None, d, tm // d, ZB_GROUP), lambda i, bi: (bi, 0, i, 0)))
    out_shapes += [jax.ShapeDtypeStruct((b, NA_WIDTH, s), BF16), jax.ShapeDtypeStruct((b, s, NA_WIDTH), BF16),
                   jax.ShapeDtypeStruct((b, s // NA_QTOK, NA_WIDTH, NA_QTOK), BF16)]
    out_specs += [pl.BlockSpec((None, NA_WIDTH, tm), lambda i, bi: (bi, 0, i)),
                  pl.BlockSpec((None, tm, NA_WIDTH), lambda i, bi: (bi, i, 0)),
                  pl.BlockSpec((None, tm // NA_QTOK, NA_WIDTH, NA_QTOK), lambda i, bi: (bi, i, 0, 0))]
    res = pl.pallas_call(
        _proj_dil_kernel,
        grid=(s // tm, b),
        in_specs=[pl.BlockSpec((None, tm, D_MODEL), lambda i, bi: (bi, i, 0)),
                  pl.BlockSpec((D_MODEL, ZB_WIDTH), lambda i, bi: (0, 0)),
                  tab_spec, tab_spec,
                  pl.BlockSpec((D_MODEL, NA_WIDTH), lambda i, bi: (0, 0)),
                  pl.BlockSpec((2 * NA_WIDTH, D_MODEL), lambda i, bi: (0, 0))],
        out_specs=out_specs,
        out_shape=out_shapes,
        scratch_shapes=[pltpu.VMEM((ZB_GROUP // LANES, tm, LANES), F32)] * 3,
        compiler_params=_cparams(2),
        name="proj_dil_na",
    )(h, w, cos, sin, wk, wqvt)
    return res[:3], res[3:]


def _mla_prep_kernel(h_ref, wa_ref, gq_ref, gkv_ref, wqt_ref, wk_ref, wvt_ref,
                     cq_ref, sq_ref, ck_ref, sk_ref, q_ref, k_ref, vt_ref):
    def norm(x32, g):
        return (x32 * lax.rsqrt(jnp.mean(x32 * x32, axis=-1, keepdims=True) + NORM_EPS) * g).astype(BF16)

    za = jnp.dot(h_ref[...], wa_ref[...], preferred_element_type=F32)
    cqn = norm(za[:, 0:MLA_Q_RANK], gq_ref[...])
    ckvn = norm(za[:, MLA_Q_RANK:MLA_Q_RANK + MLA_KV_RANK], gkv_ref[...])
    kr = za[:, MLA_Q_RANK + MLA_KV_RANK:ZA_WIDTH]
    kr = _rope_lanes(kr, ck_ref[...], sk_ref[...])
    qt = _nt_dot(wqt_ref[...], cqn)
    k = jnp.dot(ckvn, wk_ref[...], preferred_element_type=F32)
    cq, sq = cq_ref[...], sq_ref[...]
    vt = _nt_dot(wvt_ref[...], ckvn).astype(BF16)
    half = MLA_HEAD_PAD // 2
    for h in range(MLA_HEADS):
        cols = slice(h * MLA_HEAD_PAD, (h + 1) * MLA_HEAD_PAD)
        x = qt[cols]
        q_ref[h] = (x * cq + jnp.concatenate([x[half:], x[:half]], axis=0) * sq).astype(BF16)
        k_ref[h] = (k[:, cols] + kr).astype(BF16)
        vt_ref[h] = vt[h * MLA_V:(h + 1) * MLA_V, :]


def _mla_prep(h, wa, g_q, g_kv, wqt, wk, wvt, qtabs_t, ktabs):
    b, s, _ = h.shape
    tm = MLA_KC
    hp = MLA_HEADS * MLA_HEAD_PAD
    tab_spec = pl.BlockSpec((tm, LANES), lambda i, bi: (i, 0))
    tab_t_spec = pl.BlockSpec((LANES, tm), lambda i, bi: (0, i))
    const = lambda shape: pl.BlockSpec(shape, lambda i, bi: (0,) * len(shape))
    return pl.pallas_call(
        _mla_prep_kernel,
        grid=(s // tm, b),
        in_specs=[pl.BlockSpec((None, tm, D_MODEL), lambda i, bi: (bi, i, 0)), const((D_MODEL, ZA_WIDTH)),
                  const((1, MLA_Q_RANK)), const((1, MLA_KV_RANK)),
                  const((hp, MLA_Q_RANK)), const((MLA_KV_RANK, hp)), const((MLA_WIDTH, MLA_KV_RANK)),
                  tab_t_spec, tab_t_spec, tab_spec, tab_spec],
        out_specs=[pl.BlockSpec((None, MLA_HEADS, MLA_HEAD_PAD, tm), lambda i, bi: (bi, 0, 0, i)),
                   pl.BlockSpec((None, MLA_HEADS, tm, MLA_HEAD_PAD), lambda i, bi: (bi, 0, i, 0)),
                   pl.BlockSpec((None, None, MLA_HEADS, MLA_V, tm), lambda i, bi: (bi, i, 0, 0, 0))],
        out_shape=[jax.ShapeDtypeStruct((b, MLA_HEADS, MLA_HEAD_PAD, s), BF16),
                   jax.ShapeDtypeStruct((b, MLA_HEADS, s, MLA_HEAD_PAD), BF16),
                   jax.ShapeDtypeStruct((b, s // tm, MLA_HEADS, MLA_V, tm), BF16)],
        compiler_params=_cparams(2),
        name="mla_prep",
    )(h, wa, g_q.reshape(1, -1), g_kv.reshape(1, -1), wqt, wk, wvt, *qtabs_t, *ktabs)


MLA_ONES_ROWS = 16


def _mla_attn_kernel(q_ref, k_ref, vt_ref, o_ref, s0_ref, s1_ref, ot_ref):
    tq = q_ref.shape[2]
    n_chunks, _, _, kc = vt_ref.shape
    ones = jnp.ones((MLA_ONES_ROWS, kc), BF16)
    acc_rows = MLA_V + MLA_ONES_ROWS
    s_refs = (s0_ref, s1_ref)

    def stage(h_score, h_value, m_value, parity):
        def body(j, carry):
            mx, acc = carry
            start = pl.multiple_of(j * kc, kc)
            if h_score is not None:
                st = jnp.dot(k_ref[h_score, pl.ds(start, kc), :], q_ref[h_score],
                             preferred_element_type=F32)
                s_refs[parity][pl.ds(start, kc), :] = st
                mx = jnp.maximum(mx, jnp.max(st.reshape(kc // 8, 8, tq), axis=0))
            if h_value is not None:
                p = jnp.exp2(s_refs[1 - parity][pl.ds(start, kc), :] - m_value).astype(BF16)
                vt = jnp.concatenate([vt_ref[j, h_value], ones], axis=0)
                acc = acc + jnp.dot(vt, p, preferred_element_type=F32)
            return mx, acc

        init = (jnp.full((8, tq), NEG_INF, F32), jnp.zeros((acc_rows, tq), F32))
        mx, acc = lax.fori_loop(0, n_chunks, body, init, unroll=MLA_UNROLL)
        if h_value is not None:
            ot_ref[h_value] = acc[:MLA_V] / acc[MLA_V:MLA_V + 1]
        return jnp.max(mx, axis=0, keepdims=True)

    def stage_pair(u, m):
        m = stage(2 * u + 1, 2 * u, m, 1)
        return stage(2 * u + 2, 2 * u + 1, m, 0)

    m = stage(0, None, None, 0)
    m = lax.fori_loop(0, MLA_HEADS // 2 - 1, stage_pair, m)
    m = stage(MLA_HEADS - 1, MLA_HEADS - 2, m, 1)
    stage(None, MLA_HEADS - 1, m, 0)
    for pr in range(MLA_HEADS // 2):
        pair = jnp.concatenate([ot_ref[2 * pr], ot_ref[2 * pr + 1]], axis=0)
        o_ref[:, 2 * pr * MLA_V:(2 * pr + 2) * MLA_V] = pair.T.astype(BF16)


def _mla_attn(qt, k, vt):
    b, _, s, _ = k.shape
    n_chunks, kc = vt.shape[1], vt.shape[4]
    tq = min(MLA_TQ, s)
    return pl.pallas_call(
        _mla_attn_kernel,
        grid=(b, s // tq),
        in_specs=[pl.BlockSpec((None, MLA_HEADS, MLA_HEAD_PAD, tq), lambda bi, i: (bi, 0, 0, i)),
                  pl.BlockSpec((None, MLA_HEADS, s, MLA_HEAD_PAD), lambda bi, i: (bi, 0, 0, 0)),
                  pl.BlockSpec((None, n_chunks, MLA_HEADS, MLA_V, kc), lambda bi, i: (bi, 0, 0, 0, 0))],
        out_specs=pl.BlockSpec((None, tq, MLA_WIDTH), lambda bi, i: (bi, i, 0)),
        out_shape=jax.ShapeDtypeStruct((b, s, MLA_WIDTH), BF16),
        scratch_shapes=[pltpu.VMEM((s, tq), F32), pltpu.VMEM((s, tq), F32), pltpu.VMEM((MLA_HEADS, MLA_V, tq), F32)],
        compiler_params=_cparams(2),
        name="mla_attn",
    )(qt, k, vt)


def _dil_attn_kernel(qkv_ref, o_ref, lse_ref, s_ref, *, tq, win):
    d, length, _ = qkv_ref.shape
    tiles = length // tq
    n_total = d * tiles
    heads = range(DIL_HPG)
    key_rel = lax.broadcasted_iota(jnp.int32, (tq, win), 1) - lax.broadcasted_iota(jnp.int32, (tq, win), 0)
    lane = lax.broadcasted_iota(jnp.int32, (tq, LANES), 1)
    low_half = lane < DIL_HD
    qk_head = (lane // (DIL_HD // 2)) % 2
    head_lanes = (qk_head == 0, qk_head == 1)

    def coords(n):
        r = n // tiles
        q0 = pl.multiple_of((n - r * tiles) * tq, tq)
        start = pl.multiple_of(jnp.clip(q0 - DIL_BAND, 0, length - win), DIL_BAND)
        return r, q0, start

    def score_step(n):
        r, q0, start = coords(n)
        valid = jnp.abs(key_rel + (start - q0)) <= DIL_BAND
        ms = []
        for h in heads:
            cols = slice((h // 2) * LANES, (h // 2 + 1) * LANES)
            q = qkv_ref[r, pl.ds(q0, tq), cols]
            k = qkv_ref[r, pl.ds(start, win), DIL_WIDTH + cols.start:DIL_WIDTH + cols.stop]
            qm = jnp.where(head_lanes[h % 2], q, jnp.zeros_like(q))
            sc = jnp.where(valid, _nt_dot(qm, k), NEG_INF)
            s_ref[n % 2, h] = sc
            ms.append(jnp.max(sc, axis=-1, keepdims=True))
        return tuple(ms)

    def value_step(n, ms):
        r, q0, start = coords(n)
        res = []
        for h in heads:
            cols = slice(2 * DIL_WIDTH + (h // 2) * LANES, 2 * DIL_WIDTH + (h // 2 + 1) * LANES)
            p = jnp.exp2(s_ref[n % 2, h] - ms[h])
            l = jnp.sum(p, axis=-1, keepdims=True)
            o = jnp.dot(p.astype(BF16), qkv_ref[r, pl.ds(start, win), cols], preferred_element_type=F32) / l
            res.append((o, ms[h] + jnp.log2(l)))
        for pr in range(DIL_HPG // 2):
            cols = slice(pr * LANES, (pr + 1) * LANES)
            o_ref[r, pl.ds(q0, tq), cols] = jnp.where(low_half, res[2 * pr][0], res[2 * pr + 1][0]).astype(BF16)
            lse_ref[r, pl.ds(q0, tq), cols] = jnp.where(low_half, res[2 * pr][1], res[2 * pr + 1][1])

    def body(n, ms):
        ms_next = score_step(jnp.minimum(n + 1, n_total - 1))
        value_step(n, ms)
        return ms_next

    lax.fori_loop(0, n_total, body, score_step(0), unroll=4)


def _dil_attn(zb, d):
    b, _, length, _ = zb.shape
    tq = min(2 * DIL_BAND, length)
    win = min(4 * DIL_BAND, length)
    whole = lambda width: pl.BlockSpec((None, d, length, width), lambda bi: (bi, 0, 0, 0))
    return pl.pallas_call(
        functools.partial(_dil_attn_kernel, tq=tq, win=win),
        grid=(b,),
        in_specs=[whole(ZB_GROUP)],
        out_specs=[whole(DIL_WIDTH), whole(DIL_WIDTH)],
        out_shape=[jax.ShapeDtypeStruct((b, d, length, DIL_WIDTH), BF16),
                   jax.ShapeDtypeStruct((b, d, length, DIL_WIDTH), F32)],
        scratch_shapes=[pltpu.VMEM((2, DIL_HPG, tq, win), F32)],
        compiler_params=_cparams(1),
        name=f"dil_attn_d{d}",
    )(zb)


NA_QROWS = 4
NA_UROWS = NA_QROWS + NA_KH
NA_QTOK = NA_QROWS * GRID_W
NA_UTOK = NA_UROWS * GRID_W


def _na_projection(h, wk_ref, wqvt_ref, qt_ref, k_ref, vt_ref):
    k_ref[...] = jnp.dot(h, wk_ref[...], preferred_element_type=F32).astype(BF16)
    qvt = _nt_dot(wqvt_ref[...], h)
    qt_ref[...] = (qvt[:NA_WIDTH] * LOG2_E).astype(BF16)
    for c in range(vt_ref.shape[0]):
        vt_ref[c] = qvt[NA_WIDTH:, c * NA_QTOK:(c + 1) * NA_QTOK].astype(BF16)


NA_VARIANTS = ((0, lambda i: 0), (-NA_KH // 2, lambda i: i), (-NA_KH, lambda i: NA_KH // 2))


def _na_bias_kernel(rpb_ref, t_ref):
    h = pl.program_id(0)
    shape = (GRID_W, LANES)
    w = lax.broadcasted_iota(jnp.int32, shape, 0)
    lane = lax.broadcasted_iota(jnp.int32, shape, 1)
    c = lane & (GRID_W - 1)
    first = lane < GRID_W
    cs = jnp.clip(c - NA_KW // 2, 0, GRID_W - NA_KW)
    inside = (w >= cs) & (w < cs + NA_KW)
    off = w - c + NA_KW - 1
    neg = jnp.full(shape, NEG_INF, F32)
    tiles = {(None, None): neg}
    for v, (delta, lo) in enumerate(NA_VARIANTS):
        for jk in range(NA_UROWS):
            for ip in range(NA_QROWS // 2):
                ro = []
                for i in (2 * ip, 2 * ip + 1):
                    ok = lo(i) <= jk < lo(i) + NA_KH
                    ro.append(jk - i + NA_KH - 1 + delta if ok else None)
                ro = tuple(ro)
                if ro not in tiles:
                    def body(kk, acc, ro=ro):
                        a = NEG_INF if ro[0] is None else rpb_ref[h, ro[0], kk]
                        b = NEG_INF if ro[1] is None else rpb_ref[h, ro[1], kk]
                        return jnp.where(off == kk, jnp.where(first, a, b), acc)

                    acc = lax.fori_loop(0, 2 * NA_KW - 1, body, neg)
                    tiles[ro] = jnp.where(inside, acc * LOG2_E, NEG_INF)
                t_ref[v, jk * GRID_W:(jk + 1) * GRID_W, ip * LANES:(ip + 1) * LANES] = tiles[ro]


def _na_bias(rpb_l):
    nv = len(NA_VARIANTS)
    return pl.pallas_call(
        _na_bias_kernel,
        grid=(NA_HEADS,),
        in_specs=[pl.BlockSpec(memory_space=pltpu.SMEM)],
        out_specs=pl.BlockSpec((nv, None, NA_UTOK, NA_QTOK), lambda h: (0, h, 0, 0)),
        out_shape=jax.ShapeDtypeStruct((nv, NA_HEADS, NA_UTOK, NA_QTOK), F32),
        compiler_params=_cparams(1),
        name="na_bias",
    )(rpb_l)


def _na_attn_kernel(q_ref, k_ref, vt_ref, t_ref, o_ref, s_ref, *, nblk):
    c0 = jnp.clip(pl.program_id(1) - 1, 0, nblk - NA_UROWS // NA_QROWS)
    k0 = pl.multiple_of(c0 * NA_QTOK, NA_QTOK)
    row = lax.broadcasted_iota(jnp.int32, (LANES, NA_QTOK), 0)
    head_rows = (row < NA_HD, row >= NA_HD)
    ones = jnp.ones((MLA_ONES_ROWS, NA_UTOK), BF16)

    def scores(h):
        cols = slice((h // 2) * LANES, (h // 2 + 1) * LANES)
        qt = q_ref[cols, :]
        qm = jnp.where(head_rows[h % 2], qt, jnp.zeros_like(qt))
        st = jnp.dot(k_ref[pl.ds(k0, NA_UTOK), cols], qm, preferred_element_type=F32) + t_ref[h]
        s_ref[h % 2] = st
        return jnp.max(jnp.max(st.reshape(NA_UTOK // 8, 8, NA_QTOK), axis=0), axis=0, keepdims=True)

    def values(h, m):
        pt = jnp.exp2(s_ref[h % 2] - m).astype(BF16)
        vt = jnp.concatenate([vt_ref[c0 + c, h * NA_HD:(h + 1) * NA_HD, :] for c in range(NA_UROWS // NA_QROWS)],
                             axis=1)
        acc = jnp.dot(jnp.concatenate([vt, ones], axis=0), pt, preferred_element_type=F32)
        return acc[:NA_HD] / acc[NA_HD:NA_HD + 1]

    m_next = scores(0)
    o_prev = None
    for h in range(NA_HEADS):
        m = m_next
        if h + 1 < NA_HEADS:
            m_next = scores(h + 1)
        o_t = values(h, m)
        if h % 2 == 1:
            o_ref[:, (h - 1) * NA_HD:(h + 1) * NA_HD] = jnp.concatenate([o_prev, o_t], axis=0).T.astype(BF16)
        o_prev = o_t


def _na_attn(qt, k, vt, table):
    b, s, _ = k.shape
    nblk = s // NA_QTOK
    last = nblk - 1
    variant = lambda bi, a: (jnp.where(a == 0, 0, jnp.where(a == last, 2, 1)), 0, 0, 0)
    return pl.pallas_call(
        functools.partial(_na_attn_kernel, nblk=nblk),
        grid=(b, nblk),
        in_specs=[pl.BlockSpec((None, NA_WIDTH, NA_QTOK), lambda bi, a: (bi, 0, a)),
                  pl.BlockSpec((None, s, NA_WIDTH), lambda bi, a: (bi, 0, 0)),
                  pl.BlockSpec((None, nblk, NA_WIDTH, NA_QTOK), lambda bi, a: (bi, 0, 0, 0)),
                  pl.BlockSpec((None, NA_HEADS, NA_UTOK, NA_QTOK), variant)],
        out_specs=pl.BlockSpec((None, NA_QTOK, NA_WIDTH), lambda bi, a: (bi, a, 0)),
        out_shape=jax.ShapeDtypeStruct((b, s, NA_WIDTH), BF16),
        scratch_shapes=[pltpu.VMEM((2, NA_UTOK, NA_QTOK), F32)],
        compiler_params=_cparams(2),
        name="na_attn",
    )(qt, k, vt, table)


def _out_kernel(*refs, emit_h):
    (x_ref, oa_ref, hin_ref, oc_ref,
     ob0_ref, ob1_ref, ob2_ref, ls0_ref, ls1_ref, ls2_ref,
     wg_ref, wm_ref, wpa_ref, wpb_ref, wpc_ref, wout_ref, gpost_ref, mod_ref) = refs[:18]
    rest = refs[18:]
    if emit_h:
        gnext_ref, modn_ref, y_ref, h_ref, so_ref, sl_ref = rest
    else:
        y_ref, so_ref, sl_ref = rest
    tm = x_ref.shape[0]

    for gi, (ob_ref, ls_ref, (_, d)) in enumerate(zip((ob0_ref, ob1_ref, ob2_ref), (ls0_ref, ls1_ref, ls2_ref),
                                                     DIL_PAIRS)):
        for r in range(d):
            rows = slice(None) if d == 1 else pl.ds(r, tm // d, stride=d)
            for c in range(DIL_WIDTH // LANES):
                cols = slice(c * LANES, (c + 1) * LANES)
                so_ref[gi, c, rows, :] = ob_ref[r, :, cols].astype(F32)
                sl_ref[gi, c, rows, :] = ls_ref[r, :, cols]
    slabs = range(DIL_WIDTH // LANES)
    lse = [jnp.concatenate([sl_ref[gi, c] for c in slabs], axis=1) for gi in range(3)]
    o_g = [jnp.concatenate([so_ref[gi, c] for c in slabs], axis=1) for gi in range(3)]
    mx = jnp.maximum(jnp.maximum(lse[0], lse[1]), lse[2])
    e = [jnp.exp2(x - mx) for x in lse]
    o_b = (e[0] * o_g[0] + e[1] * o_g[1] + e[2] * o_g[2]) / (e[0] + e[1] + e[2])

    hin = hin_ref[...]

    def gated(o, lo, hi):
        g = jnp.dot(hin, wg_ref[:, lo:hi], preferred_element_type=F32)
        return (o * (g * (1.0 + jnp.tanh(g)))).astype(BF16)

    def merged(idx, act, w_ref):
        t = jnp.tanh(jnp.dot(hin, wm_ref[:, idx * D_MODEL:(idx + 1) * D_MODEL], preferred_element_type=F32))
        return (1.0 + t) * jnp.dot(act, w_ref[...], preferred_element_type=F32)

    a = gated(oa_ref[...].astype(F32), 0, MLA_WIDTH)
    bb = gated(o_b, MLA_WIDTH, MLA_WIDTH + DIL_WIDTH)
    c = gated(oc_ref[...].astype(F32), MLA_WIDTH + DIL_WIDTH, GATE_WIDTH)
    mixed = merged(0, a, wpa_ref) + merged(1, bb, wpb_ref) + merged(2, c, wpc_ref)
    out = jnp.dot(mixed.astype(BF16), wout_ref[...], preferred_element_type=F32)
    gain = gpost_ref[...] * mod_ref[2:3, :]
    y = x_ref[...] + (out * lax.rsqrt(jnp.mean(out * out, axis=-1, keepdims=True) + NORM_EPS)) * gain
    y_ref[...] = y
    if emit_h:
        h_ref[...] = _modulated_norm(y, gnext_ref[...], modn_ref[...]).astype(BF16)


def _out_layer(x, oa, h_in, oc, obs, lses, wg, wm, wpa, wpb, wpc, wout, g_post, mod, g_next, mod_next):
    b, s, _ = x.shape
    tm = TM_OUT
    emit_h = g_next is not None
    tok = lambda n: pl.BlockSpec((None, tm, n), lambda bi, i: (bi, i, 0))
    const = lambda shape: pl.BlockSpec(shape, lambda bi, i: (0,) * len(shape))
    modspec = pl.BlockSpec((None, 3, D_MODEL), lambda bi, i: (bi, 0, 0))
    cls = [pl.BlockSpec((None, d, tm // d, DIL_WIDTH), lambda bi, i: (bi, 0, i, 0)) for _, d in DIL_PAIRS]
    in_specs = [tok(D_MODEL), tok(MLA_WIDTH), tok(D_MODEL), tok(NA_WIDTH),
                *cls, *cls,
                const((D_MODEL, GATE_WIDTH)), const((D_MODEL, 3 * D_MODEL)), const((MLA_WIDTH, D_MODEL)), const((DIL_WIDTH, D_MODEL)), const((NA_WIDTH, D_MODEL)),
                const((D_MODEL, D_MODEL)), const((1, D_MODEL)), modspec]
    args = [x, oa, h_in, oc, *obs, *lses, wg, wm, wpa, wpb, wpc, wout, g_post.reshape(1, D_MODEL), mod]
    out_specs = [tok(D_MODEL)]
    out_shape = [jax.ShapeDtypeStruct((b, s, D_MODEL), F32)]
    if emit_h:
        in_specs += [const((1, D_MODEL)), modspec]
        args += [g_next.reshape(1, D_MODEL), mod_next]
        out_specs.append(tok(D_MODEL))
        out_shape.append(jax.ShapeDtypeStruct((b, s, D_MODEL), BF16))
    res = pl.pallas_call(
        functools.partial(_out_kernel, emit_h=emit_h),
        grid=(b, s // tm),
        in_specs=in_specs,
        out_specs=out_specs,
        out_shape=out_shape,
        scratch_shapes=[pltpu.VMEM((3, DIL_WIDTH // LANES, tm, LANES), F32)] * 2,
        compiler_params=_cparams(2),
        name="out_layer",
    )(*args)
    return (res[0], res[1]) if emit_h else (res[0], None)


def _rope_tables(s, head_dim, lane_dim, scale):
    half = head_dim // 2
    inv = ROPE_THETA ** (-jnp.arange(half, dtype=F32) * 2.0 / head_dim)
    ang = jnp.arange(s, dtype=F32)[:, None] * inv[None, :]
    cos, sin = jnp.cos(ang), jnp.sin(ang)
    lane_dim = np.asarray(lane_dim)
    in_rope = lane_dim >= 0
    idx = np.where(in_rope, lane_dim % half, 0)
    sign = np.where(lane_dim < half, -1.0, 1.0).astype(np.float32)
    cos_t = jnp.where(in_rope[None, :], cos[:, idx], 1.0) * scale
    sin_t = jnp.where(in_rope[None, :], sin[:, idx] * sign[None, :], 0.0) * scale
    return cos_t.astype(F32), sin_t.astype(F32)


_LANE = np.arange(LANES)
DIL_LANE_DIM = _LANE % (DIL_HD // 2) + (DIL_HD // 2) * (_LANE // (LANES // 2))
MLA_LANE_DIM = np.where(_LANE < MLA_ROPE // 2, _LANE,
                        np.where((_LANE >= LANES // 2) & (_LANE < LANES // 2 + MLA_ROPE // 2),
                                 _LANE - LANES // 2 + MLA_ROPE // 2, -1))


def _pair_rotary_order(w):
    lead = w.shape[:-1]
    w = w.reshape(*lead, DIL_HPG // 2, 2, 2, DIL_HD // 2)
    return jnp.swapaxes(w, -3, -2).reshape(*lead, DIL_WIDTH)


def _mla_head_slab(nope, rope):
    lead = (nope if nope is not None else rope).shape[:-1]
    dtype = (nope if nope is not None else rope).dtype
    z = lambda n: jnp.zeros(lead + (n,), dtype)
    r = MLA_ROPE // 2
    n0 = LANES // 2 - r
    parts = [z(r) if rope is None else rope[..., :r],
             z(n0) if nope is None else nope[..., :n0],
             z(r) if rope is None else rope[..., r:],
             z(MLA_NOPE - n0) if nope is None else nope[..., n0:],
             z(LANES - MLA_NOPE - MLA_ROPE)]
    return jnp.concatenate(parts, axis=-1)


def _layout_weights(w_in, w_uq, w_ukv):
    w_in, w_uq, w_ukv = w_in.astype(BF16), w_uq.astype(BF16), w_ukv.astype(BF16)
    c = [0] + [int(v) for v in _CUTS]
    cq, ckv, kr, gate_a, qkv_b, gate_b, qkv_c, gate_c, merge = [w_in[:, :, c[i]:c[i + 1]] for i in range(9)]
    w_a = jnp.concatenate([cq, ckv, _mla_head_slab(None, kr)], axis=-1)
    w_g = 0.5 * jnp.concatenate([gate_a, gate_b, gate_c], axis=-1)
    merge = 0.5 * merge
    qkv_b = qkv_b.reshape(DEPTH, D_MODEL, 3, len(DIL_PAIRS), DIL_WIDTH)
    qkv_b = qkv_b * jnp.array([DIL_HD ** -0.5, 1.0, 1.0], w_in.dtype)[None, None, :, None, None]
    qkv_b = jnp.concatenate([_pair_rotary_order(qkv_b[:, :, :2]), qkv_b[:, :, 2:]], axis=2)
    w_b = qkv_b.transpose(0, 1, 3, 2, 4).reshape(DEPTH, D_MODEL, ZB_WIDTH)
    qkv_c = qkv_c.reshape(DEPTH, D_MODEL, 3, NA_WIDTH)
    qkv_c = qkv_c * jnp.array([NA_HD ** -0.5, 1.0, 1.0], w_in.dtype)[None, None, :, None]
    w_ck = qkv_c[:, :, 1]
    w_cqvt = jnp.concatenate([qkv_c[:, :, 0], qkv_c[:, :, 2]], axis=-1).transpose(0, 2, 1)
    uq = w_uq.reshape(DEPTH, MLA_Q_RANK, MLA_HEADS, MLA_DQK)
    w_q = _mla_head_slab(uq[..., :MLA_NOPE], uq[..., MLA_NOPE:]).reshape(DEPTH, MLA_Q_RANK, MLA_HEADS * MLA_HEAD_PAD)
    ukv = w_ukv.reshape(DEPTH, MLA_KV_RANK, MLA_HEADS, MLA_NOPE + MLA_V)
    w_k = _mla_head_slab(ukv[..., :MLA_NOPE], None).reshape(DEPTH, MLA_KV_RANK, MLA_HEADS * MLA_HEAD_PAD)
    w_vt = ukv[..., MLA_NOPE:].reshape(DEPTH, MLA_KV_RANK, MLA_WIDTH).transpose(0, 2, 1)
    bf = lambda w: w.astype(BF16)
    return dict(a=bf(w_a), g=bf(w_g), b=bf(w_b), ck=bf(w_ck), cqvt=bf(w_cqvt), m=bf(merge),
                qt=bf(w_q).transpose(0, 2, 1), k=bf(w_k), vt=bf(w_vt))


def _trunk(x, c, p):
    b, s, _ = x.shape
    mods = _ada(c, p["w_ada"], p["b_ada"]).reshape(DEPTH, b, 3, D_MODEL)
    dil_tabs = _rope_tables(s, DIL_HD, DIL_LANE_DIM, 1.0)
    q_tabs_t = tuple(t.T for t in _rope_tables(s, MLA_ROPE, MLA_LANE_DIM, MLA_DQK ** -0.5 * LOG2_E))
    k_tabs = _rope_tables(s, MLA_ROPE, MLA_LANE_DIM, 1.0)
    w = p["w"]
    h = _prenorm(x, p["g_pre"][0], mods[0])
    for l in range(DEPTH):
        zbs, (qc, kc, vtc) = _proj_dil(h, w["b"][l], dil_tabs, w["ck"][l], w["cqvt"][l])
        q, k, vt = _mla_prep(h, w["a"][l], p["g_q"][l], p["g_kv"][l], w["qt"][l], w["k"][l], w["vt"][l],
                             q_tabs_t, k_tabs)
        oa = _mla_attn(q, k, vt)
        dil = [_dil_attn(zb, d) for zb, (_, d) in zip(zbs, DIL_PAIRS)]
        oc = _na_attn(qc, kc, vtc, p["na_bias"][l])
        last = l == DEPTH - 1
        x, h = _out_layer(x, oa, h, oc, [o for o, _ in dil], [ls for _, ls in dil], w["g"][l], w["m"][l],
                          p["w_pa"][l], p["w_pb"][l], p["w_pc"][l], p["w_out"][l], p["g_post"][l], mods[l],
                          None if last else p["g_pre"][l + 1], None if last else mods[l + 1])
    return x


def _prepare(w_ada, b_ada, g_pre, g_post, w_in, g_q, w_uq, g_kv, w_ukv, rpb, w_pa, w_pb, w_pc, w_out):
    return dict(w_ada=w_ada.astype(BF16), b_ada=b_ada, g_pre=g_pre, g_post=g_post, g_q=g_q, g_kv=g_kv,
                w=_layout_weights(w_in, w_uq, w_ukv),
                na_bias=[_na_bias(rpb[l]) for l in range(DEPTH)],
                w_pa=(0.5 * w_pa).astype(BF16), w_pb=(0.5 * w_pb).astype(BF16), w_pc=(0.5 * w_pc).astype(BF16),
                w_out=w_out.astype(BF16))


def kernel(x_prompt, x_sample, c_prompt, c_sample, w_ada, b_ada, g_pre, g_post, w_in, g_q, w_uq, g_kv, w_ukv, rpb,
           w_pa, w_pb, w_pc, w_out):
    p = _prepare(w_ada, b_ada, g_pre, g_post, w_in, g_q, w_uq, g_kv, w_ukv, rpb, w_pa, w_pb, w_pc, w_out)
    return (_trunk(x_prompt, c_prompt, p), _trunk(x_sample, c_sample, p))
```

```python
import functools

import jax
import jax.numpy as jnp
import numpy as np
from jax import lax
from jax.experimental import pallas as pl
from jax.experimental.pallas import tpu as pltpu

F32 = jnp.float32
BF16 = jnp.bfloat16

D_MODEL = 1024
DEPTH = 4
GRID_W = 64
ROPE_THETA = 10000.0
NORM_EPS = 1e-6
NEG_INF = -1e30
LOG2_E = float(np.log2(np.e))

MLA_HEADS = 8
MLA_NOPE = 64
MLA_ROPE = 32
MLA_V = 64
MLA_DQK = MLA_NOPE + MLA_ROPE
MLA_Q_RANK = 384
MLA_KV_RANK = 256
MLA_WIDTH = MLA_HEADS * MLA_V
MLA_HEAD_PAD = 128

DIL_PAIRS = ((128, 1), (512, 4), (2048, 16))
DIL_HPG = 4
DIL_HD = 64
DIL_HEADS = 12
DIL_WIDTH = DIL_HPG * DIL_HD
DIL_BAND = 64

NA_HEADS = 8
NA_HD = 64
NA_KH = 8
NA_KW = 16
NA_WIDTH = NA_HEADS * NA_HD

LANES = 128
VMEM_LIMIT = 56 * 1024 * 1024

_CUTS = np.cumsum((MLA_Q_RANK, MLA_KV_RANK, MLA_ROPE, MLA_WIDTH, 3 * DIL_HEADS * DIL_HD, DIL_WIDTH,
                   3 * NA_HEADS * NA_HD, NA_WIDTH, 3 * D_MODEL))
ZA_WIDTH = MLA_Q_RANK + MLA_KV_RANK + LANES
GATE_WIDTH = MLA_WIDTH + DIL_WIDTH + NA_WIDTH
ZB_WIDTH = 3 * DIL_HEADS * DIL_HD
ZB_GROUP = 3 * DIL_WIDTH
ZC_WIDTH = 3 * NA_WIDTH

TM_IN = 1024
TM_OUT = 512
MLA_TQ = 512
MLA_KC = 512
MLA_UNROLL = 8


def _cparams(n_grid):
    return pltpu.CompilerParams(dimension_semantics=("arbitrary",) * n_grid, vmem_limit_bytes=VMEM_LIMIT)


def _sigmoid(x):
    return 0.5 * (1.0 + jnp.tanh(0.5 * x))


def _rope_lanes(x, cos, sin):
    return x * cos + pltpu.roll(x, LANES // 2, 1) * sin


def _nt_dot(a, b):
    return lax.dot_general(a, b, (((1,), (1,)), ((), ())), preferred_element_type=F32)


def _ada_kernel(c_ref, w_ref, b_ref, o_ref):
    c = c_ref[...]
    c_act = (c * _sigmoid(c)).astype(BF16)
    o_ref[...] = jnp.dot(c_act, w_ref[...], preferred_element_type=F32) + b_ref[...]


def _ada(c, w_ada_bf, b_ada):
    b = c.shape[0]
    return pl.pallas_call(
        _ada_kernel,
        grid=(DEPTH,),
        in_specs=[pl.BlockSpec((b, D_MODEL), lambda l: (0, 0)),
                  pl.BlockSpec((None, D_MODEL, 3 * D_MODEL), lambda l: (l, 0, 0)),
                  pl.BlockSpec((None, 1, 3 * D_MODEL), lambda l: (l, 0, 0))],
        out_specs=pl.BlockSpec((None, b, 3 * D_MODEL), lambda l: (l, 0, 0)),
        out_shape=jax.ShapeDtypeStruct((DEPTH, b, 3 * D_MODEL), F32),
        compiler_params=_cparams(1),
        name="ada",
    )(c, w_ada_bf, b_ada.reshape(DEPTH, 1, 3 * D_MODEL))


def _modulated_norm(x32, g, mod):
    gain = g * (1.0 + mod[1:2, :])
    return (x32 * lax.rsqrt(jnp.mean(x32 * x32, axis=-1, keepdims=True) + NORM_EPS)) * gain + mod[0:1, :]


def _prenorm_kernel(x_ref, g_ref, mod_ref, h_ref):
    h_ref[...] = _modulated_norm(x_ref[...], g_ref[...], mod_ref[...]).astype(BF16)


def _prenorm(x, g, mod):
    b, s, _ = x.shape
    tm = TM_IN
    return pl.pallas_call(
        _prenorm_kernel,
        grid=(b, s // tm),
        in_specs=[pl.BlockSpec((None, tm, D_MODEL), lambda bi, i: (bi, i, 0)),
                  pl.BlockSpec((1, D_MODEL), lambda bi, i: (0, 0)),
                  pl.BlockSpec((None, 3, D_MODEL), lambda bi, i: (bi, 0, 0))],
        out_specs=pl.BlockSpec((None, tm, D_MODEL), lambda bi, i: (bi, i, 0)),
        out_shape=jax.ShapeDtypeStruct((b, s, D_MODEL), BF16),
        compiler_params=_cparams(2),
        name="prenorm",
    )(x, g.reshape(1, D_MODEL), mod)


def _proj_dil_kernel(h_ref, w_ref, cos_ref, sin_ref, wk_ref, wqvt_ref, o0_ref, o1_ref, o2_ref,
                     qt_ref, k_ref, vt_ref, z0_ref, z1_ref, z2_ref):
    tm = h_ref.shape[0]
    h = h_ref[...]
    cos, sin = cos_ref[...], sin_ref[...]
    slabs_per_part = DIL_WIDTH // LANES
    slabs_per_group = ZB_GROUP // LANES
    for gi, (o_ref, z_ref, (_, d)) in enumerate(zip((o0_ref, o1_ref, o2_ref), (z0_ref, z1_ref, z2_ref), DIL_PAIRS)):
        z = jnp.dot(h, w_ref[:, gi * ZB_GROUP:(gi + 1) * ZB_GROUP], preferred_element_type=F32)
        for c in range(slabs_per_group):
            x = z[:, c * LANES:(c + 1) * LANES]
            part = c // slabs_per_part
            if part != 2:
                x = _rope_lanes(x, cos, sin)
            if part == 0:
                x = x * LOG2_E
            z_ref[c] = x
        for r in range(d):
            rows = slice(None) if d == 1 else pl.ds(r, tm // d, stride=d)
            for c in range(slabs_per_group):
                o_ref[r, :, c * LANES:(c + 1) * LANES] = z_ref[c, rows, :].astype(BF16)
    _na_projection(h, wk_ref, wqvt_ref, qt_ref, k_ref, vt_ref)


def _proj_dil(h, w, tabs, wk, wqvt):
    b, s, _ = h.shape
    tm = TM_IN
    cos, sin = tabs
    tab_spec = pl.BlockSpec((tm, LANES), lambda i, bi: (i, 0))
    out_shapes, out_specs = [], []
    for _, d in DIL_PAIRS:
        out_shapes.append(jax.ShapeDtypeStruct((b, d, s // d, ZB_GROUP), BF16))
        out_specs.append(pl.BlockSpec((None, d, tm // d, ZB_GROUP), lambda i, bi: (bi, 0, i, 0)))
    out_shapes += [jax.ShapeDtypeStruct((b, NA_WIDTH, s), BF16), jax.ShapeDtypeStruct((b, s, NA_WIDTH), BF16),
                   jax.ShapeDtypeStruct((b, s // NA_QTOK, NA_WIDTH, NA_QTOK), BF16)]
    out_specs += [pl.BlockSpec((None, NA_WIDTH, tm), lambda i, bi: (bi, 0, i)),
                  pl.BlockSpec((None, tm, NA_WIDTH), lambda i, bi: (bi, i, 0)),
                  pl.BlockSpec((None, tm // NA_QTOK, NA_WIDTH, NA_QTOK), lambda i, bi: (bi, i, 0, 0))]
    res = pl.pallas_call(
        _proj_dil_kernel,
        grid=(s // tm, b),
        in_specs=[pl.BlockSpec((None, tm, D_MODEL), lambda i, bi: (bi, i, 0)),
                  pl.BlockSpec((D_MODEL, ZB_WIDTH), lambda i, bi: (0, 0)),
                  tab_spec, tab_spec,
                  pl.BlockSpec((D_MODEL, NA_WIDTH), lambda i, bi: (0, 0)),
                  pl.BlockSpec((2 * NA_WIDTH, D_MODEL), lambda i, bi: (0, 0))],
        out_specs=out_specs,
        out_shape=out_shapes,
        scratch_shapes=[pltpu.VMEM((ZB_GROUP // LANES, tm, LANES), F32)] * 3,
        compiler_params=_cparams(2),
        name="proj_dil_na",
    )(h, w, cos, sin, wk, wqvt)
    return res[:3], res[3:]


def _mla_prep_kernel(h_ref, wa_ref, gq_ref, gkv_ref, wqt_ref, wk_ref, wvt_ref,
                     cq_ref, sq_ref, ck_ref, sk_ref, q_ref, k_ref, vt_ref):
    def norm(x32, g):
        return (x32 * lax.rsqrt(jnp.mean(x32 * x32, axis=-1, keepdims=True) + NORM_EPS) * g).astype(BF16)

    za = jnp.dot(h_ref[...], wa_ref[...], preferred_element_type=F32)
    cqn = norm(za[:, 0:MLA_Q_RANK], gq_ref[...])
    ckvn = norm(za[:, MLA_Q_RANK:MLA_Q_RANK + MLA_KV_RANK], gkv_ref[...])
    kr = za[:, MLA_Q_RANK + MLA_KV_RANK:ZA_WIDTH]
    kr = _rope_lanes(kr, ck_ref[...], sk_ref[...])
    qt = _nt_dot(wqt_ref[...], cqn)
    k = jnp.dot(ckvn, wk_ref[...], preferred_element_type=F32)
    cq, sq = cq_ref[...], sq_ref[...]
    vt = _nt_dot(wvt_ref[...], ckvn).astype(BF16)
    half = MLA_HEAD_PAD // 2
    for h in range(MLA_HEADS):
        cols = slice(h * MLA_HEAD_PAD, (h + 1) * MLA_HEAD_PAD)
        x = qt[cols]
        q_ref[h] = (x * cq + jnp.concatenate([x[half:], x[:half]], axis=0) * sq).astype(BF16)
        k_ref[h] = (k[:, cols] + kr).astype(BF16)
        vt_ref[h] = vt[h * MLA_V:(h + 1) * MLA_V, :]


def _mla_prep(h, wa, g_q, g_kv, wqt, wk, wvt, qtabs_t, ktabs):
    b, s, _ = h.shape
    tm = MLA_KC
    hp = MLA_HEADS * MLA_HEAD_PAD
    tab_spec = pl.BlockSpec((tm, LANES), lambda i, bi: (i, 0))
    tab_t_spec = pl.BlockSpec((LANES, tm), lambda i, bi: (0, i))
    const = lambda shape: pl.BlockSpec(shape, lambda i, bi: (0,) * len(shape))
    return pl.pallas_call(
        _mla_prep_kernel,
        grid=(s // tm, b),
        in_specs=[pl.BlockSpec((None, tm, D_MODEL), lambda i, bi: (bi, i, 0)), const((D_MODEL, ZA_WIDTH)),
                  const((1, MLA_Q_RANK)), const((1, MLA_KV_RANK)),
                  const((hp, MLA_Q_RANK)), const((MLA_KV_RANK, hp)), const((MLA_WIDTH, MLA_KV_RANK)),
                  tab_t_spec, tab_t_spec, tab_spec, tab_spec],
        out_specs=[pl.BlockSpec((None, MLA_HEADS, MLA_HEAD_PAD, tm), lambda i, bi: (bi, 0, 0, i)),
                   pl.BlockSpec((None, MLA_HEADS, tm, MLA_HEAD_PAD), lambda i, bi: (bi, 0, i, 0)),
                   pl.BlockSpec((None, None, MLA_HEADS, MLA_V, tm), lambda i, bi: (bi, i, 0, 0, 0))],
        out_shape=[jax.ShapeDtypeStruct((b, MLA_HEADS, MLA_HEAD_PAD, s), BF16),
                   jax.ShapeDtypeStruct((b, MLA_HEADS, s, MLA_HEAD_PAD), BF16),
                   jax.ShapeDtypeStruct((b, s // tm, MLA_HEADS, MLA_V, tm), BF16)],
        compiler_params=_cparams(2),
        name="mla_prep",
    )(h, wa, g_q.reshape(1, -1), g_kv.reshape(1, -1), wqt, wk, wvt, *qtabs_t, *ktabs)


MLA_ONES_ROWS = 16


def _mla_attn_kernel(q_ref, k_ref, vt_ref, o_ref, s0_ref, s1_ref, ot_ref):
    tq = q_ref.shape[2]
    n_chunks, _, _, kc = vt_ref.shape
    ones = jnp.ones((MLA_ONES_ROWS, kc), BF16)
    acc_rows = MLA_V + MLA_ONES_ROWS
    s_refs = (s0_ref, s1_ref)

    def stage(h_score, h_value, m_value, parity):
        def body(j, carry):
            mx, acc = carry
            start = pl.multiple_of(j * kc, kc)
            if h_score is not None:
                st = jnp.dot(k_ref[h_score, pl.ds(start, kc), :], q_ref[h_score],
                             preferred_element_type=F32)
                s_refs[parity][pl.ds(start, kc), :] = st
                mx = jnp.maximum(mx, jnp.max(st.reshape(kc // 8, 8, tq), axis=0))
            if h_value is not None:
                p = jnp.exp2(s_refs[1 - parity][pl.ds(start, kc), :] - m_value).astype(BF16)
                vt = jnp.concatenate([vt_ref[j, h_value], ones], axis=0)
                acc = acc + jnp.dot(vt, p, preferred_element_type=F32)
            return mx, acc

        init = (jnp.full((8, tq), NEG_INF, F32), jnp.zeros((acc_rows, tq), F32))
        mx, acc = lax.fori_loop(0, n_chunks, body, init, unroll=MLA_UNROLL)
        if h_value is not None:
            ot_ref[h_value] = acc[:MLA_V] / acc[MLA_V:MLA_V + 1]
        return jnp.max(mx, axis=0, keepdims=True)

    def stage_pair(u, m):
        m = stage(2 * u + 1, 2 * u, m, 1)
        return stage(2 * u + 2, 2 * u + 1, m, 0)

    m = stage(0, None, None, 0)
    m = lax.fori_loop(0, MLA_HEADS // 2 - 1, stage_pair, m)
    m = stage(MLA_HEADS - 1, MLA_HEADS - 2, m, 1)
    stage(None, MLA_HEADS - 1, m, 0)
    for pr in range(MLA_HEADS // 2):
        pair = jnp.concatenate([ot_ref[2 * pr], ot_ref[2 * pr + 1]], axis=0)
        o_ref[:, 2 * pr * MLA_V:(2 * pr + 2) * MLA_V] = pair.T.astype(BF16)


def _mla_attn(qt, k, vt):
    b, _, s, _ = k.shape
    n_chunks, kc = vt.shape[1], vt.shape[4]
    tq = min(MLA_TQ, s)
    return pl.pallas_call(
        _mla_attn_kernel,
        grid=(b, s // tq),
        in_specs=[pl.BlockSpec((None, MLA_HEADS, MLA_HEAD_PAD, tq), lambda bi, i: (bi, 0, 0, i)),
                  pl.BlockSpec((None, MLA_HEADS, s, MLA_HEAD_PAD), lambda bi, i: (bi, 0, 0, 0)),
                  pl.BlockSpec((None, n_chunks, MLA_HEADS, MLA_V, kc), lambda bi, i: (bi, 0, 0, 0, 0))],
        out_specs=pl.BlockSpec((None, tq, MLA_WIDTH), lambda bi, i: (bi, i, 0)),
        out_shape=jax.ShapeDtypeStruct((b, s, MLA_WIDTH), BF16),
        scratch_shapes=[pltpu.VMEM((s, tq), F32), pltpu.VMEM((s, tq), F32), pltpu.VMEM((MLA_HEADS, MLA_V, tq), F32)],
        compiler_params=_cparams(2),
        name="mla_attn",
    )(qt, k, vt)


def _dil_attn_kernel(qkv_ref, o_ref, lse_ref, s_ref, *, tq, win):
    d, length, _ = qkv_ref.shape
    tiles = length // tq
    n_total = d * tiles
    heads = range(DIL_HPG)
    key_rel = lax.broadcasted_iota(jnp.int32, (tq, win), 1) - lax.broadcasted_iota(jnp.int32, (tq, win), 0)
    lane = lax.broadcasted_iota(jnp.int32, (tq, LANES), 1)
    low_half = lane < DIL_HD
    qk_head = (lane // (DIL_HD // 2)) % 2
    head_lanes = (qk_head == 0, qk_head == 1)

    def coords(n):
        r = n // tiles
        q0 = pl.multiple_of((n - r * tiles) * tq, tq)
        start = pl.multiple_of(jnp.clip(q0 - DIL_BAND, 0, length - win), DIL_BAND)
        return r, q0, start

    def score_step(n):
        r, q0, start = coords(n)
        valid = jnp.abs(key_rel + (start - q0)) <= DIL_BAND
        ms = []
        for h in heads:
            cols = slice((h // 2) * LANES, (h // 2 + 1) * LANES)
            q = qkv_ref[r, pl.ds(q0, tq), cols]
            k = qkv_ref[r, pl.ds(start, win), DIL_WIDTH + cols.start:DIL_WIDTH + cols.stop]
            qm = jnp.where(head_lanes[h % 2], q, jnp.zeros_like(q))
            sc = jnp.where(valid, _nt_dot(qm, k), NEG_INF)
            s_ref[n % 2, h] = sc
            ms.append(jnp.max(sc, axis=-1, keepdims=True))
        return tuple(ms)

    def value_step(n, ms):
        r, q0, start = coords(n)
        res = []
        for h in heads:
            cols = slice(2 * DIL_WIDTH + (h // 2) * LANES, 2 * DIL_WIDTH + (h // 2 + 1) * LANES)
            p = jnp.exp2(s_ref[n % 2, h] - ms[h])
            l = jnp.sum(p, axis=-1, keepdims=True)
            o = jnp.dot(p.astype(BF16), qkv_ref[r, pl.ds(start, win), cols], preferred_element_type=F32) / l
            res.append((o, ms[h] + jnp.log2(l)))
        for pr in range(DIL_HPG // 2):
            cols = slice(pr * LANES, (pr + 1) * LANES)
            o_ref[r, pl.ds(q0, tq), cols] = jnp.where(low_half, res[2 * pr][0], res[2 * pr + 1][0]).astype(BF16)
            lse_ref[r, pl.ds(q0, tq), cols] = jnp.where(low_half, res[2 * pr][1], res[2 * pr + 1][1])

    def body(n, ms):
        ms_next = score_step(jnp.minimum(n + 1, n_total - 1))
        value_step(n, ms)
        return ms_next

    lax.fori_loop(0, n_total, body, score_step(0), unroll=4)


def _dil_attn(zb, d):
    b, _, length, _ = zb.shape
    tq = min(2 * DIL_BAND, length)
    win = min(4 * DIL_BAND, length)
    whole = lambda width: pl.BlockSpec((None, d, length, width), lambda bi: (bi, 0, 0, 0))
    return pl.pallas_call(
        functools.partial(_dil_attn_kernel, tq=tq, win=win),
        grid=(b,),
        in_specs=[whole(ZB_GROUP)],
        out_specs=[whole(DIL_WIDTH), whole(DIL_WIDTH)],
        out_shape=[jax.ShapeDtypeStruct((b, d, length, DIL_WIDTH), BF16),
                   jax.ShapeDtypeStruct((b, d, length, DIL_WIDTH), F32)],
        scratch_shapes=[pltpu.VMEM((2, DIL_HPG, tq, win), F32)],
        compiler_params=_cparams(1),
        name=f"dil_attn_d{d}",
    )(zb)


NA_QROWS = 4
NA_UROWS = NA_QROWS + NA_KH
NA_QTOK = NA_QROWS * GRID_W
NA_UTOK = NA_UROWS * GRID_W


def _na_projection(h, wk_ref, wqvt_ref, qt_ref, k_ref, vt_ref):
    k_ref[...] = jnp.dot(h, wk_ref[...], preferred_element_type=F32).astype(BF16)
    qvt = _nt_dot(wqvt_ref[...], h)
    qt_ref[...] = (qvt[:NA_WIDTH] * LOG2_E).astype(BF16)
    for c in range(vt_ref.shape[0]):
        vt_ref[c] = qvt[NA_WIDTH:, c * NA_QTOK:(c + 1) * NA_QTOK].astype(BF16)


NA_VARIANTS = ((0, lambda i: 0), (-NA_KH // 2, lambda i: i), (-NA_KH, lambda i: NA_KH // 2))


def _na_bias_kernel(rpb_ref, t_ref):
    h = pl.program_id(0)
    shape = (GRID_W, LANES)
    w = lax.broadcasted_iota(jnp.int32, shape, 0)
    lane = lax.broadcasted_iota(jnp.int32, shape, 1)
    c = lane & (GRID_W - 1)
    first = lane < GRID_W
    cs = jnp.clip(c - NA_KW // 2, 0, GRID_W - NA_KW)
    inside = (w >= cs) & (w < cs + NA_KW)
    off = w - c + NA_KW - 1
    neg = jnp.full(shape, NEG_INF, F32)
    tiles = {(None, None): neg}
    for v, (delta, lo) in enumerate(NA_VARIANTS):
        for jk in range(NA_UROWS):
            for ip in range(NA_QROWS // 2):
                ro = []
                for i in (2 * ip, 2 * ip + 1):
                    ok = lo(i) <= jk < lo(i) + NA_KH
                    ro.append(jk - i + NA_KH - 1 + delta if ok else None)
                ro = tuple(ro)
                if ro not in tiles:
                    def body(kk, acc, ro=ro):
                        a = NEG_INF if ro[0] is None else rpb_ref[h, ro[0], kk]
                        b = NEG_INF if ro[1] is None else rpb_ref[h, ro[1], kk]
                        return jnp.where(off == kk, jnp.where(first, a, b), acc)

                    acc = lax.fori_loop(0, 2 * NA_KW - 1, body, neg)
                    tiles[ro] = jnp.where(inside, acc * LOG2_E, NEG_INF)
                t_ref[v, jk * GRID_W:(jk + 1) * GRID_W, ip * LANES:(ip + 1) * LANES] = tiles[ro]


def _na_bias(rpb_l):
    nv = len(NA_VARIANTS)
    return pl.pallas_call(
        _na_bias_kernel,
        grid=(NA_HEADS,),
        in_specs=[pl.BlockSpec(memory_space=pltpu.SMEM)],
        out_specs=pl.BlockSpec((nv, None, NA_UTOK, NA_QTOK), lambda h: (0, h, 0, 0)),
        out_shape=jax.ShapeDtypeStruct((nv, NA_HEADS, NA_UTOK, NA_QTOK), F32),
        compiler_params=_cparams(1),
        name="na_bias",
    )(rpb_l)


def _na_attn_kernel(q_ref, k_ref, vt_ref, t_ref, o_ref, s_ref, *, nblk):
    c0 = jnp.clip(pl.program_id(1) - 1, 0, nblk - NA_UROWS // NA_QROWS)
    k0 = pl.multiple_of(c0 * NA_QTOK, NA_QTOK)
    row = lax.broadcasted_iota(jnp.int32, (LANES, NA_QTOK), 0)
    head_rows = (row < NA_HD, row >= NA_HD)
    ones = jnp.ones((MLA_ONES_ROWS, NA_UTOK), BF16)

    def scores(h):
        cols = slice((h // 2) * LANES, (h // 2 + 1) * LANES)
        qt = q_ref[cols, :]
        qm = jnp.where(head_rows[h % 2], qt, jnp.zeros_like(qt))
        st = jnp.dot(k_ref[pl.ds(k0, NA_UTOK), cols], qm, preferred_element_type=F32) + t_ref[h]
        s_ref[h % 2] = st
        return jnp.max(jnp.max(st.reshape(NA_UTOK // 8, 8, NA_QTOK), axis=0), axis=0, keepdims=True)

    def values(h, m):
        pt = jnp.exp2(s_ref[h % 2] - m).astype(BF16)
        vt = jnp.concatenate([vt_ref[c0 + c, h * NA_HD:(h + 1) * NA_HD, :] for c in range(NA_UROWS // NA_QROWS)],
                             axis=1)
        acc = jnp.dot(jnp.concatenate([vt, ones], axis=0), pt, preferred_element_type=F32)
        return acc[:NA_HD] / acc[NA_HD:NA_HD + 1]

    m_next = scores(0)
    o_prev = None
    for h in range(NA_HEADS):
        m = m_next
        if h + 1 < NA_HEADS:
            m_next = scores(h + 1)
        o_t = values(h, m)
        if h % 2 == 1:
            o_ref[:, (h - 1) * NA_HD:(h + 1) * NA_HD] = jnp.concatenate([o_prev, o_t], axis=0).T.astype(BF16)
        o_prev = o_t


def _na_attn(qt, k, vt, table):
    b, s, _ = k.shape
    nblk = s // NA_QTOK
    last = nblk - 1
    variant = lambda bi, a: (jnp.where(a == 0, 0, jnp.where(a == last, 2, 1)), 0, 0, 0)
    return pl.pallas_call(
        functools.partial(_na_attn_kernel, nblk=nblk),
        grid=(b, nblk),
        in_specs=[pl.BlockSpec((None, NA_WIDTH, NA_QTOK), lambda bi, a: (bi, 0, a)),
                  pl.BlockSpec((None, s, NA_WIDTH), lambda bi, a: (bi, 0, 0)),
                  pl.BlockSpec((None, nblk, NA_WIDTH, NA_QTOK), lambda bi, a: (bi, 0, 0, 0)),
                  pl.BlockSpec((None, NA_HEADS, NA_UTOK, NA_QTOK), variant)],
        out_specs=pl.BlockSpec((None, NA_QTOK, NA_WIDTH), lambda bi, a: (bi, a, 0)),
        out_shape=jax.ShapeDtypeStruct((b, s, NA_WIDTH), BF16),
        scratch_shapes=[pltpu.VMEM((2, NA_UTOK, NA_QTOK), F32)],
        compiler_params=_cparams(2),
        name="na_attn",
    )(qt, k, vt, table)


def _out_kernel(*refs, emit_h):
    (x_ref, oa_ref, hin_ref, oc_ref,
     ob0_ref, ob1_ref, ob2_ref, ls0_ref, ls1_ref, ls2_ref,
     wg_ref, wm_ref, wpa_ref, wpb_ref, wpc_ref, wout_ref, gpost_ref, mod_ref) = refs[:18]
    rest = refs[18:]
    if emit_h:
        gnext_ref, modn_ref, y_ref, h_ref, so_ref, sl_ref = rest
    else:
        y_ref, so_ref, sl_ref = rest
    tm = x_ref.shape[0]

    for gi, (ob_ref, ls_ref, (_, d)) in enumerate(zip((ob0_ref, ob1_ref, ob2_ref), (ls0_ref, ls1_ref, ls2_ref),
                                                     DIL_PAIRS)):
        for r in range(d):
            rows = slice(None) if d == 1 else pl.ds(r, tm // d, stride=d)
            for c in range(DIL_WIDTH // LANES):
                cols = slice(c * LANES, (c + 1) * LANES)
                so_ref[gi, c, rows, :] = ob_ref[r, :, cols].astype(F32)
                sl_ref[gi, c, rows, :] = ls_ref[r, :, cols]
    slabs = range(DIL_WIDTH // LANES)
    lse = [jnp.concatenate([sl_ref[gi, c] for c in slabs], axis=1) for gi in range(3)]
    o_g = [jnp.concatenate([so_ref[gi, c] for c in slabs], axis=1) for gi in range(3)]
    mx = jnp.maximum(jnp.maximum(lse[0], lse[1]), lse[2])
    e = [jnp.exp2(x - mx) for x in lse]
    o_b = (e[0] * o_g[0] + e[1] * o_g[1] + e[2] * o_g[2]) / (e[0] + e[1] + e[2])

    hin = hin_ref[...]

    def gated(o, lo, hi):
        g = jnp.dot(hin, wg_ref[:, lo:hi], preferred_element_type=F32)
        return (o * (g * (1.0 + jnp.tanh(g)))).astype(BF16)

    def merged(idx, act, w_ref):
        t = jnp.tanh(jnp.dot(hin, wm_ref[:, idx * D_MODEL:(idx + 1) * D_MODEL], preferred_element_type=F32))
        return (1.0 + t) * jnp.dot(act, w_ref[...], preferred_element_type=F32)

    a = gated(oa_ref[...].astype(F32), 0, MLA_WIDTH)
    bb = gated(o_b, MLA_WIDTH, MLA_WIDTH + DIL_WIDTH)
    c = gated(oc_ref[...].astype(F32), MLA_WIDTH + DIL_WIDTH, GATE_WIDTH)
    mixed = merged(0, a, wpa_ref) + merged(1, bb, wpb_ref) + merged(2, c, wpc_ref)
    out = jnp.dot(mixed.astype(BF16), wout_ref[...], preferred_element_type=F32)
    gain = gpost_ref[...] * mod_ref[2:3, :]
    y = x_ref[...] + (out * lax.rsqrt(jnp.mean(out * out, axis=-1, keepdims=True) + NORM_EPS)) * gain
    y_ref[...] = y
    if emit_h:
        h_ref[...] = _modulated_norm(y, gnext_ref[...], modn_ref[...]).astype(BF16)


def _out_layer(x, oa, h_in, oc, obs, lses, wg, wm, wpa, wpb, wpc, wout, g_post, mod, g_next, mod_next):
    b, s, _ = x.shape
    tm = TM_OUT
    emit_h = g_next is not None
    tok = lambda n: pl.BlockSpec((None, tm, n), lambda bi, i: (bi, i, 0))
    const = lambda shape: pl.BlockSpec(shape, lambda bi, i: (0,) * len(shape))
    modspec = pl.BlockSpec((None, 3, D_MODEL), lambda bi, i: (bi, 0, 0))
    cls = [pl.BlockSpec((None, d, tm // d, DIL_WIDTH), lambda bi, i: (bi, 0, i, 0)) for _, d in DIL_PAIRS]
    in_specs = [tok(D_MODEL), tok(MLA_WIDTH), tok(D_MODEL), tok(NA_WIDTH),
                *cls, *cls,
                const((D_MODEL, GATE_WIDTH)), const((D_MODEL, 3 * D_MODEL)), const((MLA_WIDTH, D_MODEL)), const((DIL_WIDTH, D_MODEL)), const((NA_WIDTH, D_MODEL)),
                const((D_MODEL, D_MODEL)), const((1, D_MODEL)), modspec]
    args = [x, oa, h_in, oc, *obs, *lses, wg, wm, wpa, wpb, wpc, wout, g_post.reshape(1, D_MODEL), mod]
    out_specs = [tok(D_MODEL)]
    out_shape = [jax.ShapeDtypeStruct((b, s, D_MODEL), F32)]
    if emit_h:
        in_specs += [const((1, D_MODEL)), modspec]
        args += [g_next.reshape(1, D_MODEL), mod_next]
        out_specs.append(tok(D_MODEL))
        out_shape.append(jax.ShapeDtypeStruct((b, s, D_MODEL), BF16))
    res = pl.pallas_call(
        functools.partial(_out_kernel, emit_h=emit_h),
        grid=(b, s // tm),
        in_specs=in_specs,
        out_specs=out_specs,
        out_shape=out_shape,
        scratch_shapes=[pltpu.VMEM((3, DIL_WIDTH // LANES, tm, LANES), F32)] * 2,
        compiler_params=_cparams(2),
        name="out_layer",
    )(*args)
    return (res[0], res[1]) if emit_h else (res[0], None)


def _rope_tables(s, head_dim, lane_dim, scale):
    half = head_dim // 2
    inv = ROPE_THETA ** (-jnp.arange(half, dtype=F32) * 2.0 / head_dim)
    ang = jnp.arange(s, dtype=F32)[:, None] * inv[None, :]
    cos, sin = jnp.cos(ang), jnp.sin(ang)
    lane_dim = np.asarray(lane_dim)
    in_rope = lane_dim >= 0
    idx = np.where(in_rope, lane_dim % half, 0)
    sign = np.where(lane_dim < half, -1.0, 1.0).astype(np.float32)
    cos_t = jnp.where(in_rope[None, :], cos[:, idx], 1.0) * scale
    sin_t = jnp.where(in_rope[None, :], sin[:, idx] * sign[None, :], 0.0) * scale
    return cos_t.astype(F32), sin_t.astype(F32)


_LANE = np.arange(LANES)
DIL_LANE_DIM = _LANE % (DIL_HD // 2) + (DIL_HD // 2) * (_LANE // (LANES // 2))
MLA_LANE_DIM = np.where(_LANE < MLA_ROPE // 2, _LANE,
                        np.where((_LANE >= LANES // 2) & (_LANE < LANES // 2 + MLA_ROPE // 2),
                                 _LANE - LANES // 2 + MLA_ROPE // 2, -1))


def _pair_rotary_order(w):
    lead = w.shape[:-1]
    w = w.reshape(*lead, DIL_HPG // 2, 2, 2, DIL_HD // 2)
    return jnp.swapaxes(w, -3, -2).reshape(*lead, DIL_WIDTH)


def _mla_head_slab(nope, rope):
    lead = (nope if nope is not None else rope).shape[:-1]
    dtype = (nope if nope is not None else rope).dtype
    z = lambda n: jnp.zeros(lead + (n,), dtype)
    r = MLA_ROPE // 2
    n0 = LANES // 2 - r
    parts = [z(r) if rope is None else rope[..., :r],
             z(n0) if nope is None else nope[..., :n0],
             z(r) if rope is None else rope[..., r:],
             z(MLA_NOPE - n0) if nope is None else nope[..., n0:],
             z(LANES - MLA_NOPE - MLA_ROPE)]
    return jnp.concatenate(parts, axis=-1)


def _layout_weights(w_in, w_uq, w_ukv):
    w_in, w_uq, w_ukv = w_in.astype(BF16), w_uq.astype(BF16), w_ukv.astype(BF16)
    c = [0] + [int(v) for v in _CUTS]
    cq, ckv, kr, gate_a, qkv_b, gate_b, qkv_c, gate_c, merge = [w_in[:, :, c[i]:c[i + 1]] for i in range(9)]
    w_a = jnp.concatenate([cq, ckv, _mla_head_slab(None, kr)], axis=-1)
    w_g = 0.5 * jnp.concatenate([gate_a, gate_b, gate_c], axis=-1)
    merge = 0.5 * merge
    qkv_b = qkv_b.reshape(DEPTH, D_MODEL, 3, len(DIL_PAIRS), DIL_WIDTH)
    qkv_b = qkv_b * jnp.array([DIL_HD ** -0.5, 1.0, 1.0], w_in.dtype)[None, None, :, None, None]
    qkv_b = jnp.concatenate([_pair_rotary_order(qkv_b[:, :, :2]), qkv_b[:, :, 2:]], axis=2)
    w_b = qkv_b.transpose(0, 1, 3, 2, 4).reshape(DEPTH, D_MODEL, ZB_WIDTH)
    qkv_c = qkv_c.reshape(DEPTH, D_MODEL, 3, NA_WIDTH)
    qkv_c = qkv_c * jnp.array([NA_HD ** -0.5, 1.0, 1.0], w_in.dtype)[None, None, :, None]
    w_ck = qkv_c[:, :, 1]
    w_cqvt = jnp.concatenate([qkv_c[:, :, 0], qkv_c[:, :, 2]], axis=-1).transpose(0, 2, 1)
    uq = w_uq.reshape(DEPTH, MLA_Q_RANK, MLA_HEADS, MLA_DQK)
    w_q = _mla_head_slab(uq[..., :MLA_NOPE], uq[..., MLA_NOPE:]).reshape(DEPTH, MLA_Q_RANK, MLA_HEADS * MLA_HEAD_PAD)
    ukv = w_ukv.reshape(DEPTH, MLA_KV_RANK, MLA_HEADS, MLA_NOPE + MLA_V)
    w_k = _mla_head_slab(ukv[..., :MLA_NOPE], None).reshape(DEPTH, MLA_KV_RANK, MLA_HEADS * MLA_HEAD_PAD)
    w_vt = ukv[..., MLA_NOPE:].reshape(DEPTH, MLA_KV_RANK, MLA_WIDTH).transpose(0, 2, 1)
    bf = lambda w: w.astype(BF16)
    return dict(a=bf(w_a), g=bf(w_g), b=bf(w_b), ck=bf(w_ck), cqvt=bf(w_cqvt), m=bf(merge),
                qt=bf(w_q).transpose(0, 2, 1), k=bf(w_k), vt=bf(w_vt))


def _trunk(x, c, p):
    b, s, _ = x.shape
    mods = _ada(c, p["w_ada"], p["b_ada"]).reshape(DEPTH, b, 3, D_MODEL)
    dil_tabs = _rope_tables(s, DIL_HD, DIL_LANE_DIM, 1.0)
    q_tabs_t = tuple(t.T for t in _rope_tables(s, MLA_ROPE, MLA_LANE_DIM, MLA_DQK ** -0.5 * LOG2_E))
    k_tabs = _rope_tables(s, MLA_ROPE, MLA_LANE_DIM, 1.0)
    w = p["w"]
    h = _prenorm(x, p["g_pre"][0], mods[0])
    for l in range(DEPTH):
        zbs, (qc, kc, vtc) = _proj_dil(h, w["b"][l], dil_tabs, w["ck"][l], w["cqvt"][l])
        q, k, vt = _mla_prep(h, w["a"][l], p["g_q"][l], p["g_kv"][l], w["qt"][l], w["k"][l], w["vt"][l],
                             q_tabs_t, k_tabs)
        oa = _mla_attn(q, k, vt)
        dil = [_dil_attn(zb, d) for zb, (_, d) in zip(zbs, DIL_PAIRS)]
        oc = _na_attn(qc, kc, vtc, p["na_bias"][l])
        last = l == DEPTH - 1
        x, h = _out_layer(x, oa, h, oc, [o for o, _ in dil], [ls for _, ls in dil], w["g"][l], w["m"][l],
                          p["w_pa"][l], p["w_pb"][l], p["w_pc"][l], p["w_out"][l], p["g_post"][l], mods[l],
                          None if last else p["g_pre"][l + 1], None if last else mods[l + 1])
    return x


def _prepare(w_ada, b_ada, g_pre, g_post, w_in, g_q, w_uq, g_kv, w_ukv, rpb, w_pa, w_pb, w_pc, w_out):
    return dict(w_ada=w_ada.astype(BF16), b_ada=b_ada, g_pre=g_pre, g_post=g_post, g_q=g_q, g_kv=g_kv,
                w=_layout_weights(w_in, w_uq, w_ukv),
                na_bias=[_na_bias(rpb[l]) for l in range(DEPTH)],
                w_pa=(0.5 * w_pa).astype(BF16), w_pb=(0.5 * w_pb).astype(BF16), w_pc=(0.5 * w_pc).astype(BF16),
                w_out=w_out.astype(BF16))


def kernel(x_prompt, x_sample, c_prompt, c_sample, w_ada, b_ada, g_pre, g_post, w_in, g_q, w_uq, g_kv, w_ukv, rpb,
           w_pa, w_pb, w_pc, w_out):
    p = _prepare(w_ada, b_ada, g_pre, g_post, w_in, g_q, w_uq, g_kv, w_ukv, rpb, w_pa, w_pb, w_pc, w_out)
    return (_trunk(x_prompt, c_prompt, p), _trunk(x_sample, c_sample, p))
```
